```python
import jax, jax.numpy as jnp
from jax import lax
import numpy as np

D_MODEL = 1024
BATCH = 1
SEQ = 16384
DEPTH = 1
DEC_BATCH = 8
DEC_SEQ = 16
PAST_LEN = 1024

CHUNK = 64
N_META = 16
D_LRU = 512
N_LRU_HEADS = 8
LRU_HEAD = D_LRU // N_LRU_HEADS
LRU_C = 8.0
CONV_LRU = 4
D_POOL = 512
POOL_WINDOWS = (2, 4, 8, 16)
N_POOL_GROUPS = len(POOL_WINDOWS)
POOL_GROUP = D_POOL // N_POOL_GROUPS
POOL_BUF = max(POOL_WINDOWS) - 1
D_MIX = D_LRU + D_POOL
D_IN = 2 * D_LRU + D_POOL
D_FF = 3 * D_MODEL
CONV_FFN = 3
EPS = 1e-6

kernel_name = "hymba_rglru_pool_convffn_stream_step"


def rmsnorm(x, g):
    xf = x.astype(jnp.float32)
    y = xf * lax.rsqrt(jnp.mean(xf * xf, axis=-1, keepdims=True) + EPS) * g.astype(jnp.float32)
    return y.astype(x.dtype)


def causal_dwconv(x, buf, w, b):
    width = w.shape[0]
    T = x.shape[1]
    xp = jnp.concatenate([buf.astype(x.dtype), x], axis=1)
    y = xp[:, 0:T] * w[0]
    for k in range(1, width):
        y = y + xp[:, k:k + T] * w[k]
    return y + b, xp[:, xp.shape[1] - (width - 1):]


def rg_lru(x, h0, wa, ba, wx, bx, lam):
    B, T, _ = x.shape
    xh = x.reshape(B, T, N_LRU_HEADS, LRU_HEAD)
    r = jax.nn.sigmoid(jnp.einsum('bthi,hij->bthj', xh, wa).reshape(B, T, D_LRU) + ba)
    i = jax.nn.sigmoid(jnp.einsum('bthi,hij->bthj', xh, wx).reshape(B, T, D_LRU) + bx)
    log_a = (LRU_C * r.astype(jnp.float32)) * jax.nn.log_sigmoid(lam.astype(jnp.float32))
    a = jnp.exp(log_a)
    mult = jnp.sqrt(-jnp.expm1(2.0 * log_a))
    u = mult * (i * x).astype(jnp.float32)

    def combine(left, right):
        a1, b1 = left
        a2, b2 = right
        return a1 * a2, a2 * b1 + b2

    A, Bc = lax.associative_scan(combine, (a, u), axis=1)
    h = A * h0.astype(jnp.float32)[:, None, :] + Bc
    return h.astype(x.dtype), h[:, -1].astype(h0.dtype)


def pool_mixer(x, buf, start_pos, w_pool, b_pool, scale):
    B, T, _ = x.shape
    xf = x.astype(jnp.float32)
    xcat = jnp.concatenate([buf.astype(jnp.float32), xf], axis=1)
    cs = jnp.concatenate([jnp.zeros((B, 1, D_POOL), jnp.float32), jnp.cumsum(xcat, axis=1)], axis=1)
    pos = start_pos + jnp.arange(T)
    means = []
    for g, w in enumerate(POOL_WINDOWS):
        lo, hi = g * POOL_GROUP, (g + 1) * POOL_GROUP
        s = cs[:, POOL_BUF + 1:POOL_BUF + 1 + T, lo:hi] - cs[:, POOL_BUF + 1 - w:POOL_BUF + 1 - w + T, lo:hi]
        cnt = jnp.minimum(pos + 1, w).astype(jnp.float32)[None, :, None]
        means.append(s / cnt)
    m = jnp.concatenate(means, axis=-1) - xf
    mg = m.reshape(B, T, N_POOL_GROUPS, POOL_GROUP)
    y = jnp.einsum('btgc,gcd->btgd', mg, w_pool.astype(jnp.float32)).reshape(B, T, D_POOL)
    y = (y + b_pool.astype(jnp.float32)) * scale.astype(jnp.float32)
    return y.astype(x.dtype), xcat[:, xcat.shape[1] - POOL_BUF:].astype(buf.dtype)


def layer(x, start_pos, st_h, st_conv, st_pool, st_ffn, norm_mix_g, w_in, conv_lru_w, conv_lru_b,
          gate_a_w, gate_a_b, gate_x_w, gate_x_b, lru_lambda, pool_w, pool_b, pool_scale,
          gn_lru, gn_pool, w_out, norm_ffn_g, w_up, ffn_conv_w, ffn_conv_b, w_down):
    u = rmsnorm(x, norm_mix_g) @ w_in
    u_lru = u[..., :D_LRU]
    g_lru = u[..., D_LRU:2 * D_LRU]
    u_pool = u[..., 2 * D_LRU:]
    c, new_conv = causal_dwconv(u_lru, st_conv, conv_lru_w, conv_lru_b)
    h, new_h = rg_lru(c, st_h, gate_a_w, gate_a_b, gate_x_w, gate_x_b, lru_lambda)
    y_lru = h * jax.nn.gelu(g_lru)
    y_pool, new_pool = pool_mixer(u_pool, st_pool, start_pos, pool_w, pool_b, pool_scale)
    mix = jnp.concatenate([rmsnorm(y_lru, gn_lru), rmsnorm(y_pool, gn_pool)], axis=-1) @ w_out
    x = x + mix
    up = rmsnorm(x, norm_ffn_g) @ w_up
    upc, new_ffn = causal_dwconv(up, st_ffn, ffn_conv_w, ffn_conv_b)
    x = x + (jax.nn.gelu(upc[..., :D_FF]) * upc[..., D_FF:]) @ w_down
    return x, new_h, new_conv, new_pool, new_ffn


def run_trunk(x, start_pos, st_h, st_conv, st_pool, st_ffn, weights):
    hs, convs, pools, ffns = [], [], [], []
    for l in range(DEPTH):
        wl = [w[l] for w in weights]
        x, nh, nc, npool, nf = layer(x, start_pos, st_h[l], st_conv[l], st_pool[l], st_ffn[l], *wl)
        hs.append(nh)
        convs.append(nc)
        pools.append(npool)
        ffns.append(nf)
    return x, jnp.stack(hs), jnp.stack(convs), jnp.stack(pools), jnp.stack(ffns)


def setup_inputs(seed: int = 0) -> dict:
    key = jax.random.key(seed)
    ks = jax.random.split(key, 32)
    f32 = jnp.float32
    nrm = lambda k, shape, s: jax.random.normal(k, shape, f32) * s
    u = jax.random.uniform(ks[0], (DEPTH, D_LRU), f32, 0.9, 0.999)
    a0 = u ** (1.0 / LRU_C)
    lru_lambda = jnp.log(a0) - jnp.log1p(-a0)
    return {
        "x_prompt": nrm(ks[1], (BATCH, SEQ, D_MODEL), 1.0),
        "x_sample": nrm(ks[2], (DEC_BATCH, DEC_SEQ, D_MODEL), 1.0),
        "state_lru_h": nrm(ks[3], (DEPTH, DEC_BATCH, D_LRU), 0.5),
        "state_lru_conv": nrm(ks[4], (DEPTH, DEC_BATCH, CONV_LRU - 1, D_LRU), 1.0),
        "state_pool": nrm(ks[5], (DEPTH, DEC_BATCH, POOL_BUF, D_POOL), 1.0),
        "state_ffn_conv": nrm(ks[6], (DEPTH, DEC_BATCH, CONV_FFN - 1, 2 * D_FF), 1.0),
        "meta_tokens": nrm(ks[7], (N_META, D_MODEL), 1.0),
        "norm_mix_g": 1.0 + nrm(ks[8], (DEPTH, D_MODEL), 0.02),
        "w_in": nrm(ks[9], (DEPTH, D_MODEL, D_IN), D_MODEL ** -0.5),
        "conv_lru_w": nrm(ks[10], (DEPTH, CONV_LRU, D_LRU), CONV_LRU ** -0.5),
        "conv_lru_b": nrm(ks[11], (DEPTH, D_LRU), 0.01),
        "gate_a_w": nrm(ks[12], (DEPTH, N_LRU_HEADS, LRU_HEAD, LRU_HEAD), LRU_HEAD ** -0.5),
        "gate_a_b": nrm(ks[13], (DEPTH, D_LRU), 0.01),
        "gate_x_w": nrm(ks[14], (DEPTH, N_LRU_HEADS, LRU_HEAD, LRU_HEAD), LRU_HEAD ** -0.5),
        "gate_x_b": nrm(ks[15], (DEPTH, D_LRU), 0.01),
        "lru_lambda": lru_lambda,
        "pool_w": nrm(ks[16], (DEPTH, N_POOL_GROUPS, POOL_GROUP, POOL_GROUP), POOL_GROUP ** -0.5),
        "pool_b": nrm(ks[17], (DEPTH, D_POOL), 0.01),
        "pool_scale": 1.0 + nrm(ks[18], (DEPTH, D_POOL), 0.02),
        "gn_lru": 1.0 + nrm(ks[19], (DEPTH, D_LRU), 0.02),
        "gn_pool": 1.0 + nrm(ks[20], (DEPTH, D_POOL), 0.02),
        "w_out": nrm(ks[21], (DEPTH, D_MIX, D_MODEL), D_MIX ** -0.5),
        "norm_ffn_g": 1.0 + nrm(ks[22], (DEPTH, D_MODEL), 0.02),
        "w_up": nrm(ks[23], (DEPTH, D_MODEL, 2 * D_FF), D_MODEL ** -0.5),
        "ffn_conv_w": nrm(ks[24], (DEPTH, CONV_FFN, 2 * D_FF), CONV_FFN ** -0.5),
        "ffn_conv_b": nrm(ks[25], (DEPTH, 2 * D_FF), 0.01),
        "w_down": nrm(ks[26], (DEPTH, D_FF, D_MODEL), D_FF ** -0.5),
        "final_norm_g": 1.0 + nrm(ks[27], (D_MODEL,), 0.02),
    }


def reference(x_prompt, x_sample, state_lru_h, state_lru_conv, state_pool, state_ffn_conv,
              meta_tokens, norm_mix_g, w_in, conv_lru_w, conv_lru_b, gate_a_w, gate_a_b,
              gate_x_w, gate_x_b, lru_lambda, pool_w, pool_b, pool_scale, gn_lru, gn_pool,
              w_out, norm_ffn_g, w_up, ffn_conv_w, ffn_conv_b, w_down, final_norm_g):
    weights = (norm_mix_g, w_in, conv_lru_w, conv_lru_b, gate_a_w, gate_a_b, gate_x_w, gate_x_b,
               lru_lambda, pool_w, pool_b, pool_scale, gn_lru, gn_pool, w_out, norm_ffn_g,
               w_up, ffn_conv_w, ffn_conv_b, w_down)
    dt = x_prompt.dtype
    B = x_prompt.shape[0]
    meta = jnp.broadcast_to(meta_tokens.astype(dt)[None], (B, N_META, D_MODEL))
    xp = jnp.concatenate([meta, x_prompt], axis=1)
    z_h = jnp.zeros((DEPTH, B, D_LRU), state_lru_h.dtype)
    z_conv = jnp.zeros((DEPTH, B, CONV_LRU - 1, D_LRU), state_lru_conv.dtype)
    z_pool = jnp.zeros((DEPTH, B, POOL_BUF, D_POOL), state_pool.dtype)
    z_ffn = jnp.zeros((DEPTH, B, CONV_FFN - 1, 2 * D_FF), state_ffn_conv.dtype)
    hp, p_lru_h, p_lru_conv, p_pool, p_ffn_conv = run_trunk(xp, 0, z_h, z_conv, z_pool, z_ffn, weights)
    y_prompt = rmsnorm(hp, final_norm_g)[:, N_META:]
    hs, s_lru_h, s_lru_conv, s_pool, s_ffn_conv = run_trunk(
        x_sample, PAST_LEN, state_lru_h, state_lru_conv, state_pool, state_ffn_conv, weights)
    y_sample = rmsnorm(hs, final_norm_g)
    return (y_prompt, y_sample, p_lru_h, p_lru_conv, p_pool, p_ffn_conv,
            s_lru_h, s_lru_conv, s_pool, s_ffn_conv)
```

```python
import functools

import jax
import jax.numpy as jnp
from jax import lax
from jax.experimental import pallas as pl
from jax.experimental.pallas import tpu as pltpu

D_MODEL = 1024
N_META = 16
PAST_LEN = 1024
D_LRU = 512
N_LRU_HEADS = 8
LRU_HEAD = D_LRU // N_LRU_HEADS
LRU_C = 8.0
CONV_LRU = 4
D_POOL = 512
POOL_WINDOWS = (2, 4, 8, 16)
POOL_GROUP = D_POOL // len(POOL_WINDOWS)
POOL_BUF = max(POOL_WINDOWS) - 1
D_FF = 3 * D_MODEL
CONV_FFN = 3
EPS = 1e-6

SUBLANES = 8
LANES = 128
MXU_DIM = 256
VMEM_BYTES_V7X = 64 * 1024 * 1024

LRU_HDR = SUBLANES
POOL_HDR = 2 * SUBLANES
FFN_HDR = SUBLANES
FFN_CHUNK = 512
PROMPT_TILE = 256

F32 = jnp.float32
BF16 = jnp.bfloat16


def _rmsnorm(x, g):
    return x * lax.rsqrt(jnp.mean(x * x, axis=-1, keepdims=True) + EPS) * g


def _dot(a, b):
    return jnp.dot(a, b, preferred_element_type=F32)


def _stream_kernel(
    x_ref, h0_ref, conv0_ref, pool0_ref, ffn0_ref,
    norm_mix_g_ref, w_in_ref, conv_w_ref, conv_b_ref, w_gate_ref, ba_ref, bx_ref, lam_ref,
    w_pool_ref, pool_b_ref, pool_scale_ref, gn_lru_ref, gn_pool_ref, w_out_ref,
    norm_ffn_g_ref, w_up_ref, ffn_w_ref, ffn_b_ref, w_down_ref, final_g_ref,
    y_ref, h_out_ref, conv_out_ref, pool_out_ref, ffn_out_ref,
    h_carry, lru_carry, pool_carry, ffn_carry, ext_lru, ext_pool, ext_up,
    scan_a, scan_b, scan_h,
    *, n_streams, rows, start_pos):
    S, T = n_streams, rows
    step = pl.program_id(0)

    @pl.when(step == 0)
    def _load_state():
        h_carry[...] = h0_ref[...]
        lru_carry[...] = jnp.zeros_like(lru_carry)
        pool_carry[...] = jnp.zeros_like(pool_carry)
        ffn_carry[...] = jnp.zeros_like(ffn_carry)
        for s in range(S):
            lru_carry[s, LRU_HDR - (CONV_LRU - 1):, :] = conv0_ref[s]
            pool_carry[s, POOL_HDR - POOL_BUF:, :] = pool0_ref[s]
            ffn_carry[s, FFN_HDR - (CONV_FFN - 1):, :] = ffn0_ref[s]

    x = x_ref[...]

    u = _dot(_rmsnorm(x, norm_mix_g_ref[...]).astype(BF16), w_in_ref[...])
    u_lru = u[:, :D_LRU]
    g_lru = u[:, D_LRU:2 * D_LRU]
    u_pool = u[:, 2 * D_LRU:]

    lam = lam_ref[...]
    c_log_sig = LRU_C * (jnp.minimum(lam, 0.0) - jnp.log1p(jnp.exp(-jnp.abs(lam))))

    row_in_group = lax.broadcasted_iota(jnp.int32, (T, D_LRU), 0) & (SUBLANES - 1)

    y_lru_parts = []
    m_parts = []
    for s in range(S):
        r0 = s * T
        ext_lru[s, :LRU_HDR, :] = lru_carry[s]
        ext_lru[s, LRU_HDR:, :] = u_lru[r0:r0 + T]
        lru_carry[s] = ext_lru[s, T:, :]
        c = conv_b_ref[...] + u_lru[r0:r0 + T] * conv_w_ref[CONV_LRU - 1:CONV_LRU, :]
        for k in range(1, CONV_LRU):
            c = c + (ext_lru[s, LRU_HDR - k:LRU_HDR - k + T, :]
                     * conv_w_ref[CONV_LRU - 1 - k:CONV_LRU - k, :])

        cb = c.astype(BF16)
        gates = [_dot(cb[:, q * MXU_DIM:(q + 1) * MXU_DIM], w_gate_ref[q])
                 for q in range(D_LRU // MXU_DIM)]
        r = jax.nn.sigmoid(jnp.concatenate([g[:, :MXU_DIM] for g in gates], axis=1) + ba_ref[...])
        i = jax.nn.sigmoid(jnp.concatenate([g[:, MXU_DIM:] for g in gates], axis=1) + bx_ref[...])
        log_a = r * c_log_sig
        a = jnp.exp(log_a)
        mult = jnp.sqrt(jnp.tanh(-log_a) * (a * a + 1.0))
        b = mult * (i * c)

        for d in (1, 2, 4):
            a_prev = pltpu.roll(a, d, axis=0)
            b_prev = pltpu.roll(b, d, axis=0)
            inside = row_in_group >= d
            b = jnp.where(inside, a * b_prev + b, b)
            a = jnp.where(inside, a * a_prev, a)
        scan_a[r0:r0 + T, :] = a
        scan_b[r0:r0 + T, :] = b
        h = h_carry[s:s + 1, :]
        for g in range(T // SUBLANES):
            lo = r0 + g * SUBLANES
            hi = lo + SUBLANES
            scan_h[lo:hi, :] = scan_a[lo:hi, :] * h + scan_b[lo:hi, :]
            h = scan_a[hi - 1:hi, :] * h + scan_b[hi - 1:hi, :]
        h_carry[s:s + 1, :] = h
        y_lru_parts.append(scan_h[r0:r0 + T, :] * jax.nn.gelu(g_lru[r0:r0 + T]))

        ext_pool[s, :POOL_HDR, :] = pool_carry[s]
        ext_pool[s, POOL_HDR:, :] = u_pool[r0:r0 + T]
        pool_carry[s] = ext_pool[s, T:, :]
        if start_pos[s] + 1 < max(POOL_WINDOWS):
            pos1 = (lax.broadcasted_iota(jnp.int32, (T, POOL_GROUP), 0) + (start_pos[s] + 1)).astype(F32)
        means = []
        for gi, w in enumerate(POOL_WINDOWS):
            lo, hi = gi * POOL_GROUP, (gi + 1) * POOL_GROUP
            tot = u_pool[r0:r0 + T, lo:hi]
            for k in range(1, w):
                tot = tot + ext_pool[s, POOL_HDR - k:POOL_HDR - k + T, lo:hi]
            if start_pos[s] + 1 < max(POOL_WINDOWS):
                means.append(tot / jnp.minimum(pos1, float(w)))
            else:
                means.append(tot * (1.0 / w))
        m_parts.append(jnp.concatenate(means, axis=1) - u_pool[r0:r0 + T])

    y_lru = y_lru_parts[0] if S == 1 else jnp.concatenate(y_lru_parts, axis=0)
    m = (m_parts[0] if S == 1 else jnp.concatenate(m_parts, axis=0)).astype(BF16)

    y_pool = jnp.concatenate(
        [_dot(m[:, q * MXU_DIM:(q + 1) * MXU_DIM], w_pool_ref[q]) for q in range(D_POOL // MXU_DIM)],
        axis=1)
    y_pool = (y_pool + pool_b_ref[...]) * pool_scale_ref[...]

    mix_in = jnp.concatenate(
        [_rmsnorm(y_lru, gn_lru_ref[...]), _rmsnorm(y_pool, gn_pool_ref[...])], axis=1)
    x1 = x + _dot(mix_in.astype(BF16), w_out_ref[...])

    xn = _rmsnorm(x1, norm_ffn_g_ref[...]).astype(BF16)
    acc = jnp.zeros((S * T, D_MODEL), F32)
    for j in range(D_FF // FFN_CHUNK):
        halves = []
        for half in range(2):
            c0 = half * D_FF + j * FFN_CHUNK
            cols = slice(c0, c0 + FFN_CHUNK)
            up = _dot(xn, w_up_ref[:, cols])
            parts = []
            for s in range(S):
                r0 = s * T
                ext_up[half, s, :FFN_HDR, :] = ffn_carry[s, :, cols]
                ext_up[half, s, FFN_HDR:, :] = up[r0:r0 + T]
                ffn_carry[s, :, cols] = ext_up[half, s, T:, :]
                upc = ffn_b_ref[:, cols] + up[r0:r0 + T] * ffn_w_ref[CONV_FFN - 1:CONV_FFN, cols]
                for k in range(1, CONV_FFN):
                    upc = upc + (ext_up[half, s, FFN_HDR - k:FFN_HDR - k + T, :]
                                 * ffn_w_ref[CONV_FFN - 1 - k:CONV_FFN - k, cols])
                parts.append(upc)
            halves.append(parts[0] if S == 1 else jnp.concatenate(parts, axis=0))
        act = (jax.nn.gelu(halves[0]) * halves[1]).astype(BF16)
        acc = acc + _dot(act, w_down_ref[j * FFN_CHUNK:(j + 1) * FFN_CHUNK, :])

    y_ref[...] = _rmsnorm(x1 + acc, final_g_ref[...])

    @pl.when(step == pl.num_programs(0) - 1)
    def _store_state():
        h_out_ref[...] = h_carry[...]
        for s in range(S):
            conv_out_ref[s] = lru_carry[s, LRU_HDR - (CONV_LRU - 1):, :]
            pool_out_ref[s] = pool_carry[s, POOL_HDR - POOL_BUF:, :]
            ffn_out_ref[s] = ffn_carry[s, FFN_HDR - (CONV_FFN - 1):, :]


def _vmem_limit(n_streams, rows, weights):
    m = n_streams * rows
    weight_bytes = sum(w.size * w.dtype.itemsize for w in weights)
    io_bytes = 2 * 2 * m * D_MODEL * 4
    scratch_rows = n_streams * (LRU_HDR + POOL_HDR + 2 * rows) + 3 * m
    scratch_bytes = (scratch_rows * D_LRU + 2 * n_streams * (FFN_HDR + rows) * FFN_CHUNK
                     + n_streams * FFN_HDR * 2 * D_FF) * 4
    live_values = 16 * m * D_MODEL * 4
    return min(weight_bytes + io_bytes + scratch_bytes + live_values, VMEM_BYTES_V7X - (4 << 20))


def _run_streams(x, h0, conv0, pool0, ffn0, weights, *, n_streams, rows, start_pos, name):
    S, T = n_streams, rows
    m = S * T
    n_steps = x.shape[0] // m
    whole = lambda a: pl.BlockSpec(a.shape, lambda i, _n=a.ndim: (0,) * _n,
                                   pipeline_mode=pl.Buffered(1))
    state_shapes = [
        jax.ShapeDtypeStruct((S, D_LRU), F32),
        jax.ShapeDtypeStruct((S, CONV_LRU - 1, D_LRU), F32),
        jax.ShapeDtypeStruct((S, POOL_BUF, D_POOL), F32),
        jax.ShapeDtypeStruct((S, CONV_FFN - 1, 2 * D_FF), F32),
    ]
    state_out_spec = lambda a: pl.BlockSpec(a.shape, lambda i, _n=len(a.shape): (0,) * _n)
    kernel = functools.partial(_stream_kernel, n_streams=S, rows=T, start_pos=start_pos)
    return pl.pallas_call(
        kernel,
        grid=(n_steps,),
        in_specs=[pl.BlockSpec((m, D_MODEL), lambda i: (i, 0))]
        + [whole(a) for a in (h0, conv0, pool0, ffn0)]
        + [whole(w) for w in weights],
        out_specs=[pl.BlockSpec((m, D_MODEL), lambda i: (i, 0))]
        + [state_out_spec(a) for a in state_shapes],
        out_shape=[jax.ShapeDtypeStruct(x.shape, F32)] + state_shapes,
        scratch_shapes=[
            pltpu.VMEM((S, D_LRU), F32),
            pltpu.VMEM((S, LRU_HDR, D_LRU), F32),
            pltpu.VMEM((S, POOL_HDR, D_POOL), F32),
            pltpu.VMEM((S, FFN_HDR, 2 * D_FF), F32),
            pltpu.VMEM((S, LRU_HDR + T, D_LRU), F32),
            pltpu.VMEM((S, POOL_HDR + T, D_POOL), F32),
            pltpu.VMEM((2, S, FFN_HDR + T, FFN_CHUNK), F32),
            pltpu.VMEM((m, D_LRU), F32),
            pltpu.VMEM((m, D_LRU), F32),
            pltpu.VMEM((m, D_LRU), F32),
        ],
        compiler_params=pltpu.CompilerParams(
            dimension_semantics=("arbitrary",),
            vmem_limit_bytes=_vmem_limit(S, T, weights)),
        name=name,
    )(x, h0, conv0, pool0, ffn0, *weights)


def _block_diag(blocks, per_tile):
    n, k, _ = blocks.shape
    eye = jnp.eye(per_tile, dtype=blocks.dtype)
    tiles = blocks.reshape(n // per_tile, per_tile, k, k)
    return jnp.einsum('qpij,pr->qpirj', tiles, eye).reshape(n // per_tile, per_tile * k, per_tile * k)


def kernel(x_prompt, x_sample, state_lru_h, state_lru_conv, state_pool, state_ffn_conv, meta_tokens, norm_mix_g, w_in, conv_lru_w, conv_lru_b, gate_a_w, gate_a_b, gate_x_w, gate_x_b, lru_lambda, pool_w, pool_b, pool_scale, gn_lru, gn_pool, w_out, norm_ffn_g, w_up, ffn_conv_w, ffn_conv_b, w_down, final_norm_g):
    assert x_prompt.shape[0] == 1 and norm_mix_g.shape[0] == 1
    n_sample, sample_rows = x_sample.shape[:2]
    assert sample_rows == N_META

    heads_per_tile = MXU_DIM // LRU_HEAD
    w_gate = jnp.concatenate(
        [_block_diag(gate_a_w[0], heads_per_tile), _block_diag(gate_x_w[0], heads_per_tile)],
        axis=2).astype(BF16)
    w_pool = _block_diag(pool_w[0], MXU_DIM // POOL_GROUP).astype(BF16)
    row = lambda a: a.reshape(1, -1)
    weights = (
        norm_mix_g, w_in[0].astype(BF16), conv_lru_w[0], conv_lru_b, w_gate, gate_a_b, gate_x_b,
        lru_lambda, w_pool, pool_b, pool_scale, gn_lru, gn_pool, w_out[0].astype(BF16),
        norm_ffn_g, w_up[0].astype(BF16), ffn_conv_w[0], ffn_conv_b, w_down[0].astype(BF16),
        row(final_norm_g))

    n_short = 1 + n_sample
    zeros_like_one = lambda a: jnp.zeros((1,) + a.shape[2:], a.dtype)
    with_meta = lambda a: jnp.concatenate([zeros_like_one(a), a[0]], axis=0)
    x_short = jnp.concatenate([meta_tokens, x_sample.reshape(-1, D_MODEL)], axis=0)
    y_short, h_s, conv_s, pool_s, ffn_s = _run_streams(
        x_short, with_meta(state_lru_h), with_meta(state_lru_conv), with_meta(state_pool),
        with_meta(state_ffn_conv), weights, n_streams=n_short, rows=N_META,
        start_pos=(0,) + (PAST_LEN,) * n_sample, name="short_streams")

    y_p, h_p, conv_p, pool_p, ffn_p = _run_streams(
        x_prompt[0], h_s[:1], conv_s[:1], pool_s[:1], ffn_s[:1], weights,
        n_streams=1, rows=PROMPT_TILE, start_pos=(N_META,), name="prompt_stream")

    return (y_p[None], y_short[N_META:].reshape(x_sample.shape),
            h_p[None], conv_p[None], pool_p[None], ffn_p[None],
            h_s[None, 1:], conv_s[None, 1:], pool_s[None, 1:], ffn_s[None, 1:])
```

```python
import functools

import jax
import jax.numpy as jnp
from jax import lax
from jax.experimental import pallas as pl
from jax.experimental.pallas import tpu as pltpu

D_MODEL = 1024
N_META = 16
PAST_LEN = 1024
D_LRU = 512
N_LRU_HEADS = 8
LRU_HEAD = D_LRU // N_LRU_HEADS
LRU_C = 8.0
CONV_LRU = 4
D_POOL = 512
POOL_WINDOWS = (2, 4, 8, 16)
POOL_GROUP = D_POOL // len(POOL_WINDOWS)
POOL_BUF = max(POOL_WINDOWS) - 1
D_FF = 3 * D_MODEL
CONV_FFN = 3
EPS = 1e-6

SUBLANES = 8
LANES = 128
MXU_DIM = 256
VMEM_BYTES_V7X = 64 * 1024 * 1024

LRU_HDR = SUBLANES
POOL_HDR = 2 * SUBLANES
FFN_HDR = SUBLANES
SCAN_HDR = SUBLANES
SCAN_SHIFTS = (1, 2, 4)
FFN_CHUNK = 512
FFN_CHUNKS = D_FF // FFN_CHUNK
LRU_SLABS = D_LRU // LANES
POOL_SLABS = D_POOL // LANES
CHUNK_SLABS = FFN_CHUNK // LANES
FFN_SLABS = 2 * D_FF // LANES
PROMPT_TILE = 512

F32 = jnp.float32
BF16 = jnp.bfloat16


def _rmsnorm(x, g):
    return x * lax.rsqrt(jnp.mean(x * x, axis=-1, keepdims=True) + EPS) * g


def _dot(a, b):
    return jnp.dot(a, b, preferred_element_type=F32)


def _lanes(c):
    return slice(c * LANES, (c + 1) * LANES)


def _stream_kernel(
    x_ref, h0_ref, conv0_ref, pool0_ref, ffn0_ref,
    norm_mix_g_ref, w_in_ref, conv_w_ref, conv_b_ref, w_gate_ref, ba_ref, bx_ref, lam_ref,
    w_pool_ref, pool_b_ref, pool_scale_ref, gn_lru_ref, gn_pool_ref, w_out_ref,
    norm_ffn_g_ref, w_up_ref, ffn_w_ref, ffn_b_ref, w_down_ref, final_g_ref,
    y_ref, h_out_ref, conv_out_ref, pool_out_ref, ffn_out_ref,
    h_carry, lru_carry, pool_carry, ffn_carry, ext_lru, ext_pool, ext_up,
    scan_a, scan_b, scan_h,
    *, n_streams, rows, start_pos):
    S, T = n_streams, rows
    step = pl.program_id(0)

    @pl.when(step == 0)
    def _load_state():
        lru_carry[...] = jnp.zeros_like(lru_carry)
        pool_carry[...] = jnp.zeros_like(pool_carry)
        ffn_carry[...] = jnp.zeros_like(ffn_carry)
        scan_a[:, :SCAN_HDR, :] = jnp.ones((LRU_SLABS, SCAN_HDR, LANES), F32)
        scan_b[:, :SCAN_HDR, :] = jnp.zeros((LRU_SLABS, SCAN_HDR, LANES), F32)
        for s in range(S):
            h_carry[s] = jnp.broadcast_to(h0_ref[s:s + 1, :], (SUBLANES, D_LRU))
            for c in range(LRU_SLABS):
                lru_carry[s, c, LRU_HDR - (CONV_LRU - 1):, :] = conv0_ref[s, :, _lanes(c)]
            for c in range(POOL_SLABS):
                pool_carry[s, c, POOL_HDR - POOL_BUF:, :] = pool0_ref[s, :, _lanes(c)]
            for c in range(FFN_SLABS):
                ffn_carry[s, c, FFN_HDR - (CONV_FFN - 1):, :] = ffn0_ref[s, :, _lanes(c)]

    x = x_ref[...]

    u = _dot(_rmsnorm(x, norm_mix_g_ref[...]).astype(BF16), w_in_ref[...])
    u_lru = u[:, :D_LRU]
    g_lru = u[:, D_LRU:2 * D_LRU]
    u_pool = u[:, 2 * D_LRU:]

    lam = lam_ref[...]
    c_log_sig = LRU_C * (jnp.minimum(lam, 0.0) - jnp.log1p(jnp.exp(-jnp.abs(lam))))

    y_lru_parts = []
    m_parts = []
    for s in range(S):
        r0 = s * T
        c_slabs = []
        for c in range(LRU_SLABS):
            u_c = u_lru[r0:r0 + T, _lanes(c)]
            ext_lru[s, c, :LRU_HDR, :] = lru_carry[s, c]
            ext_lru[s, c, LRU_HDR:, :] = u_c
            lru_carry[s, c] = ext_lru[s, c, T:, :]
            acc_c = conv_b_ref[:, _lanes(c)] + u_c * conv_w_ref[CONV_LRU - 1:CONV_LRU, _lanes(c)]
            for k in range(1, CONV_LRU):
                acc_c = acc_c + (ext_lru[s, c, LRU_HDR - k:LRU_HDR - k + T, :]
                                 * conv_w_ref[CONV_LRU - 1 - k:CONV_LRU - k, _lanes(c)])
            c_slabs.append(acc_c)
        c = jnp.concatenate(c_slabs, axis=1)

        cb = c.astype(BF16)
        gates = [_dot(cb[:, q * MXU_DIM:(q + 1) * MXU_DIM], w_gate_ref[q])
                 for q in range(D_LRU // MXU_DIM)]
        r = jax.nn.sigmoid(jnp.concatenate([g[:, :MXU_DIM] for g in gates], axis=1) + ba_ref[...])
        i = jax.nn.sigmoid(jnp.concatenate([g[:, MXU_DIM:] for g in gates], axis=1) + bx_ref[...])
        log_a = r * c_log_sig
        a = jnp.exp(log_a)
        mult = jnp.sqrt(jnp.tanh(-log_a) * (a * a + 1.0))
        b = mult * (i * c)

        for d in SCAN_SHIFTS:
            a_prev, b_prev = [], []
            for cc in range(LRU_SLABS):
                scan_a[cc, SCAN_HDR:, :] = a[:, _lanes(cc)]
                scan_b[cc, SCAN_HDR:, :] = b[:, _lanes(cc)]
                a_prev.append(scan_a[cc, SCAN_HDR - d:SCAN_HDR - d + T, :])
                b_prev.append(scan_b[cc, SCAN_HDR - d:SCAN_HDR - d + T, :])
            b = a * jnp.concatenate(b_prev, axis=1) + b
            a = a * jnp.concatenate(a_prev, axis=1)
        h = h_carry[s]
        for g in range(T // SUBLANES):
            lo = g * SUBLANES
            h = a[lo:lo + SUBLANES] * h + b[lo:lo + SUBLANES]
            scan_h[r0 + lo:r0 + lo + SUBLANES, :] = h
        h_carry[s] = jnp.broadcast_to(scan_h[r0 + T - 1:r0 + T, :], (SUBLANES, D_LRU))
        y_lru_parts.append(scan_h[r0:r0 + T, :] * jax.nn.gelu(g_lru[r0:r0 + T]))

        ramp_up = start_pos[s] + 1 < max(POOL_WINDOWS)
        if ramp_up:
            pos1 = (lax.broadcasted_iota(jnp.int32, (T, POOL_GROUP), 0) + (start_pos[s] + 1)).astype(F32)
        means = []
        for gi, w in enumerate(POOL_WINDOWS):
            tot = u_pool[r0:r0 + T, _lanes(gi)]
            ext_pool[s, gi, :POOL_HDR, :] = pool_carry[s, gi]
            ext_pool[s, gi, POOL_HDR:, :] = tot
            pool_carry[s, gi] = ext_pool[s, gi, T:, :]
            for k in range(1, w):
                tot = tot + ext_pool[s, gi, POOL_HDR - k:POOL_HDR - k + T, :]
            means.append(tot / jnp.minimum(pos1, float(w)) if ramp_up else tot * (1.0 / w))
        m_parts.append(jnp.concatenate(means, axis=1) - u_pool[r0:r0 + T])

    y_lru = y_lru_parts[0] if S == 1 else jnp.concatenate(y_lru_parts, axis=0)
    m = (m_parts[0] if S == 1 else jnp.concatenate(m_parts, axis=0)).astype(BF16)

    y_pool = jnp.concatenate(
        [_dot(m[:, q * MXU_DIM:(q + 1) * MXU_DIM], w_pool_ref[q]) for q in range(D_POOL // MXU_DIM)],
        axis=1)
    y_pool = (y_pool + pool_b_ref[...]) * pool_scale_ref[...]

    mix_in = jnp.concatenate(
        [_rmsnorm(y_lru, gn_lru_ref[...]), _rmsnorm(y_pool, gn_pool_ref[...])], axis=1)
    x1 = x + _dot(mix_in.astype(BF16), w_out_ref[...])

    xn = _rmsnorm(x1, norm_ffn_g_ref[...]).astype(BF16)
    acc = jnp.zeros((S * T, D_MODEL), F32)
    for j in range(FFN_CHUNKS):
        halves = []
        for half in range(2):
            up = _dot(xn, w_up_ref[half * FFN_CHUNKS + j])
            parts = []
            for s in range(S):
                r0 = s * T
                slabs = []
                for c in range(CHUNK_SLABS):
                    gs = (half * FFN_CHUNKS + j) * CHUNK_SLABS + c
                    ext = ext_up.at[j % 2, half, s, c]
                    up_c = up[r0:r0 + T, _lanes(c)]
                    ext[:FFN_HDR, :] = ffn_carry[s, gs]
                    ext[FFN_HDR:, :] = up_c
                    ffn_carry[s, gs] = ext[T:, :]
                    upc = ffn_b_ref[:, _lanes(gs)] + up_c * ffn_w_ref[CONV_FFN - 1:CONV_FFN, _lanes(gs)]
                    for k in range(1, CONV_FFN):
                        upc = upc + (ext[FFN_HDR - k:FFN_HDR - k + T, :]
                                     * ffn_w_ref[CONV_FFN - 1 - k:CONV_FFN - k, _lanes(gs)])
                    slabs.append(upc)
                parts.append(jnp.concatenate(slabs, axis=1))
            halves.append(parts[0] if S == 1 else jnp.concatenate(parts, axis=0))
        act = (jax.nn.gelu(halves[0]) * halves[1]).astype(BF16)
        acc = acc + _dot(act, w_down_ref[j])

    y_ref[...] = _rmsnorm(x1 + acc, final_g_ref[...])

    @pl.when(step == pl.num_programs(0) - 1)
    def _store_state():
        for s in range(S):
            h_out_ref[s:s + 1, :] = h_carry[s, :1, :]
            for c in range(LRU_SLABS):
                conv_out_ref[s, :, _lanes(c)] = lru_carry[s, c, LRU_HDR - (CONV_LRU - 1):, :]
            for c in range(POOL_SLABS):
                pool_out_ref[s, :, _lanes(c)] = pool_carry[s, c, POOL_HDR - POOL_BUF:, :]
            for c in range(FFN_SLABS):
                ffn_out_ref[s, :, _lanes(c)] = ffn_carry[s, c, FFN_HDR - (CONV_FFN - 1):, :]


def _scratch_shapes(n_streams, rows):
    S, T = n_streams, rows
    return [
        pltpu.VMEM((S, SUBLANES, D_LRU), F32),
        pltpu.VMEM((S, LRU_SLABS, LRU_HDR, LANES), F32),
        pltpu.VMEM((S, POOL_SLABS, POOL_HDR, LANES), F32),
        pltpu.VMEM((S, FFN_SLABS, FFN_HDR, LANES), F32),
        pltpu.VMEM((S, LRU_SLABS, LRU_HDR + T, LANES), F32),
        pltpu.VMEM((S, POOL_SLABS, POOL_HDR + T, LANES), F32),
        pltpu.VMEM((2, 2, S, CHUNK_SLABS, FFN_HDR + T, LANES), F32),
        pltpu.VMEM((LRU_SLABS, SCAN_HDR + T, LANES), F32),
        pltpu.VMEM((LRU_SLABS, SCAN_HDR + T, LANES), F32),
        pltpu.VMEM((S * T, D_LRU), F32),
    ]


def _vmem_limit(n_streams, rows, weights):
    m = n_streams * rows
    weight_bytes = sum(w.size * w.dtype.itemsize for w in weights)
    io_bytes = 2 * 2 * m * D_MODEL * 4
    scratch_bytes = 0
    for sc in _scratch_shapes(n_streams, rows):
        n = 4
        for d in sc.shape:
            n *= d
        scratch_bytes += n
    live_values = 10 * m * D_MODEL * 4
    return min(weight_bytes + io_bytes + scratch_bytes + live_values, VMEM_BYTES_V7X - (6 << 20))


def _run_streams(x, h0, conv0, pool0, ffn0, weights, *, n_streams, rows, start_pos, name):
    S, T = n_streams, rows
    m = S * T
    n_steps = x.shape[0] // m
    whole = lambda a: pl.BlockSpec(a.shape, lambda i, _n=a.ndim: (0,) * _n,
                                   pipeline_mode=pl.Buffered(1))
    state_shapes = [
        jax.ShapeDtypeStruct((S, D_LRU), F32),
        jax.ShapeDtypeStruct((S, CONV_LRU - 1, D_LRU), F32),
        jax.ShapeDtypeStruct((S, POOL_BUF, D_POOL), F32),
        jax.ShapeDtypeStruct((S, CONV_FFN - 1, 2 * D_FF), F32),
    ]
    state_out_spec = lambda a: pl.BlockSpec(a.shape, lambda i, _n=len(a.shape): (0,) * _n)
    kernel = functools.partial(_stream_kernel, n_streams=S, rows=T, start_pos=start_pos)
    return pl.pallas_call(
        kernel,
        grid=(n_steps,),
        in_specs=[pl.BlockSpec((m, D_MODEL), lambda i: (i, 0))]
        + [whole(a) for a in (h0, conv0, pool0, ffn0)]
        + [whole(w) for w in weights],
        out_specs=[pl.BlockSpec((m, D_MODEL), lambda i: (i, 0))]
        + [state_out_spec(a) for a in state_shapes],
        out_shape=[jax.ShapeDtypeStruct(x.shape, F32)] + state_shapes,
        scratch_shapes=_scratch_shapes(S, T),
        compiler_params=pltpu.CompilerParams(
            dimension_semantics=("arbitrary",),
            vmem_limit_bytes=_vmem_limit(S, T, weights)),
        name=name,
    )(x, h0, conv0, pool0, ffn0, *weights)


def _block_diag(blocks, per_tile):
    n, k, _ = blocks.shape
    eye = jnp.eye(per_tile, dtype=blocks.dtype)
    tiles = blocks.reshape(n // per_tile, per_tile, k, k)
    return jnp.einsum('qpij,pr->qpirj', tiles, eye).reshape(n // per_tile, per_tile * k, per_tile * k)


def kernel(x_prompt, x_sample, state_lru_h, state_lru_conv, state_pool, state_ffn_conv, meta_tokens, norm_mix_g, w_in, conv_lru_w, conv_lru_b, gate_a_w, gate_a_b, gate_x_w, gate_x_b, lru_lambda, pool_w, pool_b, pool_scale, gn_lru, gn_pool, w_out, norm_ffn_g, w_up, ffn_conv_w, ffn_conv_b, w_down, final_norm_g):
    assert x_prompt.shape[0] == 1 and norm_mix_g.shape[0] == 1
    n_sample, sample_rows = x_sample.shape[:2]
    assert sample_rows == N_META

    heads_per_tile = MXU_DIM // LRU_HEAD
    w_gate = jnp.concatenate(
        [_block_diag(gate_a_w[0], heads_per_tile), _block_diag(gate_x_w[0], heads_per_tile)],
        axis=2).astype(BF16)
    w_pool = _block_diag(pool_w[0], MXU_DIM // POOL_GROUP).astype(BF16)
    w_up_blocks = (w_up[0].astype(BF16).reshape(D_MODEL, 2 * FFN_CHUNKS, FFN_CHUNK).transpose(1, 0, 2))
    w_down_blocks = w_down[0].astype(BF16).reshape(FFN_CHUNKS, FFN_CHUNK, D_MODEL)
    row = lambda a: a.reshape(1, -1)
    weights = (
        norm_mix_g, w_in[0].astype(BF16), conv_lru_w[0], conv_lru_b, w_gate, gate_a_b, gate_x_b,
        lru_lambda, w_pool, pool_b, pool_scale, gn_lru, gn_pool, w_out[0].astype(BF16),
        norm_ffn_g, w_up_blocks, ffn_conv_w[0], ffn_conv_b, w_down_blocks, row(final_norm_g))

    n_short = 1 + n_sample
    zeros_like_one = lambda a: jnp.zeros((1,) + a.shape[2:], a.dtype)
    with_meta = lambda a: jnp.concatenate([zeros_like_one(a), a[0]], axis=0)
    x_short = jnp.concatenate([meta_tokens, x_sample.reshape(-1, D_MODEL)], axis=0)
    y_short, h_s, conv_s, pool_s, ffn_s = _run_streams(
        x_short, with_meta(state_lru_h), with_meta(state_lru_conv), with_meta(state_pool),
        with_meta(state_ffn_conv), weights, n_streams=n_short, rows=N_META,
        start_pos=(0,) + (PAST_LEN,) * n_sample, name="short_streams")

    y_p, h_p, conv_p, pool_p, ffn_p = _run_streams(
        x_prompt[0], h_s[:1], conv_s[:1], pool_s[:1], ffn_s[:1], weights,
        n_streams=1, rows=PROMPT_TILE, start_pos=(N_META,), name="prompt_stream")

    return (y_p[None], y_short[N_META:].reshape(x_sample.shape),
            h_p[None], conv_p[None], pool_p[None], ffn_p[None],
            h_s[None, 1:], conv_s[None, 1:], pool_s[None, 1:], ffn_s[None, 1:])
```

```python
import functools

import jax
import jax.numpy as jnp
from jax import lax
from jax.experimental import pallas as pl
from jax.experimental.pallas import tpu as pltpu

D_MODEL = 1024
N_META = 16
PAST_LEN = 1024
D_LRU = 512
N_LRU_HEADS = 8
LRU_HEAD = D_LRU // N_LRU_HEADS
LRU_C = 8.0
CONV_LRU = 4
D_POOL = 512
POOL_WINDOWS = (2, 4, 8, 16)
POOL_GROUP = D_POOL // len(POOL_WINDOWS)
POOL_BUF = max(POOL_WINDOWS) - 1
D_FF = 3 * D_MODEL
CONV_FFN = 3
EPS = 1e-6

SUBLANES = 8
LANES = 128
MXU_DIM = 256
VMEM_BYTES_V7X = 64 * 1024 * 1024

LRU_HDR = SUBLANES
POOL_HDR = 2 * SUBLANES
FFN_HDR = SUBLANES
SCAN_HDR = SUBLANES
SCAN_SHIFTS = (1, 2, 4)
FFN_CHUNK = 512
FFN_CHUNKS = D_FF // FFN_CHUNK
LRU_SLABS = D_LRU // LANES
POOL_SLABS = D_POOL // LANES
CHUNK_SLABS = FFN_CHUNK // LANES
FFN_SLABS = 2 * D_FF // LANES
PROMPT_TILE = 512

F32 = jnp.float32
BF16 = jnp.bfloat16


def _rmsnorm(x, g):
    return x * lax.rsqrt(jnp.mean(x * x, axis=-1, keepdims=True) + EPS) * g


def _dot(a, b):
    return jnp.dot(a, b, preferred_element_type=F32)


def _lanes(c):
    return slice(c * LANES, (c + 1) * LANES)


def _stream_kernel(*refs, n_streams, rows, start_pos, n_x, n_fresh, y_skip, out_groups):
    S, T = n_streams, rows
    refs = list(refs)
    take = lambda n: [refs.pop(0) for _ in range(n)]
    x_refs = take(n_x)
    h0_ref, conv0_ref, pool0_ref, ffn0_ref = take(4)
    (norm_mix_g_ref, w_in_ref, conv_w_ref, conv_b_ref, w_gate_ref, ba_ref, bx_ref, lam_ref,
     w_pool_ref, pool_b_ref, pool_scale_ref, gn_lru_ref, gn_pool_ref, w_out_ref,
     norm_ffn_g_ref, w_up_ref, ffn_w_ref, ffn_b_ref, w_down_ref, final_g_ref) = take(20)
    y_ref, = take(1)
    state_out_refs = [take(4) for _ in out_groups]
    (h_carry, lru_carry, pool_carry, ffn_carry, ext_lru, ext_pool, ext_up,
     scan_a, scan_b, scan_h) = refs
    step = pl.program_id(0)

    @pl.when(step == 0)
    def _load_state():
        lru_carry[...] = jnp.zeros_like(lru_carry)
        pool_carry[...] = jnp.zeros_like(pool_carry)
        ffn_carry[...] = jnp.zeros_like(ffn_carry)
        scan_a[:, :SCAN_HDR, :] = jnp.ones((LRU_SLABS, SCAN_HDR, LANES), F32)
        scan_b[:, :SCAN_HDR, :] = jnp.zeros((LRU_SLABS, SCAN_HDR, LANES), F32)
        for s in range(n_fresh):
            h_carry[s] = jnp.zeros((SUBLANES, D_LRU), F32)
        for s in range(n_fresh, S):
            p = s - n_fresh
            h_carry[s] = jnp.broadcast_to(h0_ref[p:p + 1, :], (SUBLANES, D_LRU))
            for c in range(LRU_SLABS):
                lru_carry[s, c, LRU_HDR - (CONV_LRU - 1):, :] = conv0_ref[p, :, _lanes(c)]
            for c in range(POOL_SLABS):
                pool_carry[s, c, POOL_HDR - POOL_BUF:, :] = pool0_ref[p, :, _lanes(c)]
            for c in range(FFN_SLABS):
                ffn_carry[s, c, FFN_HDR - (CONV_FFN - 1):, :] = ffn0_ref[p, :, _lanes(c)]

    x = x_refs[0][...] if n_x == 1 else jnp.concatenate([r[...] for r in x_refs], axis=0)

    u = _dot(_rmsnorm(x, norm_mix_g_ref[...]).astype(BF16), w_in_ref[...])
    u_lru = u[:, :D_LRU]
    g_lru = u[:, D_LRU:2 * D_LRU]
    u_pool = u[:, 2 * D_LRU:]

    lam = lam_ref[...]
    c_log_sig = LRU_C * (jnp.minimum(lam, 0.0) - jnp.log1p(jnp.exp(-jnp.abs(lam))))

    y_lru_parts = []
    m_parts = []
    for s in range(S):
        r0 = s * T
        c_slabs = []
        for c in range(LRU_SLABS):
            u_c = u_lru[r0:r0 + T, _lanes(c)]
            ext_lru[s, c, :LRU_HDR, :] = lru_carry[s, c]
            ext_lru[s, c, LRU_HDR:, :] = u_c
            lru_carry[s, c] = ext_lru[s, c, T:, :]
            acc_c = conv_b_ref[:, _lanes(c)] + u_c * conv_w_ref[CONV_LRU - 1:CONV_LRU, _lanes(c)]
            for k in range(1, CONV_LRU):
                acc_c = acc_c + (ext_lru[s, c, LRU_HDR - k:LRU_HDR - k + T, :]
                                 * conv_w_ref[CONV_LRU - 1 - k:CONV_LRU - k, _lanes(c)])
            c_slabs.append(acc_c)
        c = jnp.concatenate(c_slabs, axis=1)

        cb = c.astype(BF16)
        gates = [_dot(cb[:, q * MXU_DIM:(q + 1) * MXU_DIM], w_gate_ref[q])
                 for q in range(D_LRU // MXU_DIM)]
        r = jax.nn.sigmoid(jnp.concatenate([g[:, :MXU_DIM] for g in gates], axis=1) + ba_ref[...])
        i = jax.nn.sigmoid(jnp.concatenate([g[:, MXU_DIM:] for g in gates], axis=1) + bx_ref[...])
        log_a = r * c_log_sig
        a = jnp.exp(log_a)
        mult = jnp.sqrt(jnp.tanh(-log_a) * (a * a + 1.0))
        b = mult * (i * c)

        for d in SCAN_SHIFTS:
            a_prev, b_prev = [], []
            for cc in range(LRU_SLABS):
                scan_a[cc, SCAN_HDR:, :] = a[:, _lanes(cc)]
                scan_b[cc, SCAN_HDR:, :] = b[:, _lanes(cc)]
                a_prev.append(scan_a[cc, SCAN_HDR - d:SCAN_HDR - d + T, :])
                b_prev.append(scan_b[cc, SCAN_HDR - d:SCAN_HDR - d + T, :])
            b = a * jnp.concatenate(b_prev, axis=1) + b
            a = a * jnp.concatenate(a_prev, axis=1)
        h = h_carry[s]
        for g in range(T // SUBLANES):
            lo = g * SUBLANES
            h = a[lo:lo + SUBLANES] * h + b[lo:lo + SUBLANES]
            scan_h[r0 + lo:r0 + lo + SUBLANES, :] = h
        h_carry[s] = jnp.broadcast_to(scan_h[r0 + T - 1:r0 + T, :], (SUBLANES, D_LRU))
        y_lru_parts.append(scan_h[r0:r0 + T, :] * jax.nn.gelu(g_lru[r0:r0 + T]))

        ramp_up = start_pos[s] + 1 < max(POOL_WINDOWS)
        if ramp_up:
            pos1 = (lax.broadcasted_iota(jnp.int32, (T, POOL_GROUP), 0) + (start_pos[s] + 1)).astype(F32)
        means = []
        for gi, w in enumerate(POOL_WINDOWS):
            tot = u_pool[r0:r0 + T, _lanes(gi)]
            ext_pool[s, gi, :POOL_HDR, :] = pool_carry[s, gi]
            ext_pool[s, gi, POOL_HDR:, :] = tot
            pool_carry[s, gi] = ext_pool[s, gi, T:, :]
            for k in range(1, w):
                tot = tot + ext_pool[s, gi, POOL_HDR - k:POOL_HDR - k + T, :]
            means.append(tot / jnp.minimum(pos1, float(w)) if ramp_up else tot * (1.0 / w))
        m_parts.append(jnp.concatenate(means, axis=1) - u_pool[r0:r0 + T])

    y_lru = y_lru_parts[0] if S == 1 else jnp.concatenate(y_lru_parts, axis=0)
    m = (m_parts[0] if S == 1 else jnp.concatenate(m_parts, axis=0)).astype(BF16)

    y_pool = jnp.concatenate(
        [_dot(m[:, q * MXU_DIM:(q + 1) * MXU_DIM], w_pool_ref[q]) for q in range(D_POOL // MXU_DIM)],
        axis=1)
    y_pool = (y_pool + pool_b_ref[...]) * pool_scale_ref[...]

    mix_in = jnp.concatenate(
        [_rmsnorm(y_lru, gn_lru_ref[...]), _rmsnorm(y_pool, gn_pool_ref[...])], axis=1)
    x1 = x + _dot(mix_in.astype(BF16), w_out_ref[...])

    xn = _rmsnorm(x1, norm_ffn_g_ref[...]).astype(BF16)
    up_pair = lambda j: [_dot(xn, w_up_ref[half * FFN_CHUNKS + j]) for half in range(2)]
    acc = jnp.zeros((S * T, D_MODEL), F32)
    ups = up_pair(0)
    for j in range(FFN_CHUNKS):
        ups_next = up_pair(j + 1) if j + 1 < FFN_CHUNKS else None
        halves = []
        for half in range(2):
            up = ups[half]
            parts = []
            for s in range(S):
                r0 = s * T
                slabs = []
                for c in range(CHUNK_SLABS):
                    gs = (half * FFN_CHUNKS + j) * CHUNK_SLABS + c
                    ext = ext_up.at[j % 2, half, s, c]
                    up_c = up[r0:r0 + T, _lanes(c)]
                    ext[:FFN_HDR, :] = ffn_carry[s, gs]
                    ext[FFN_HDR:, :] = up_c
                    ffn_carry[s, gs] = ext[T:, :]
                    upc = ffn_b_ref[:, _lanes(gs)] + up_c * ffn_w_ref[CONV_FFN - 1:CONV_FFN, _lanes(gs)]
                    for k in range(1, CONV_FFN):
                        upc = upc + (ext[FFN_HDR - k:FFN_HDR - k + T, :]
                                     * ffn_w_ref[CONV_FFN - 1 - k:CONV_FFN - k, _lanes(gs)])
                    slabs.append(upc)
                parts.append(jnp.concatenate(slabs, axis=1))
            halves.append(parts[0] if S == 1 else jnp.concatenate(parts, axis=0))
        act = (jax.nn.gelu(halves[0]) * halves[1]).astype(BF16)
        acc = acc + _dot(act, w_down_ref[j])
        ups = ups_next

    y = _rmsnorm(x1 + acc, final_g_ref[...])
    y_ref[...] = y[y_skip:]

    @pl.when(step == pl.num_programs(0) - 1)
    def _store_state():
        for (first, count), (h_out_ref, conv_out_ref, pool_out_ref, ffn_out_ref) in zip(
                out_groups, state_out_refs):
            for p in range(count):
                s = first + p
                h_out_ref[p:p + 1, :] = h_carry[s, :1, :]
                for c in range(LRU_SLABS):
                    conv_out_ref[p, :, _lanes(c)] = lru_carry[s, c, LRU_HDR - (CONV_LRU - 1):, :]
                for c in range(POOL_SLABS):
                    pool_out_ref[p, :, _lanes(c)] = pool_carry[s, c, POOL_HDR - POOL_BUF:, :]
                for c in range(FFN_SLABS):
                    ffn_out_ref[p, :, _lanes(c)] = ffn_carry[s, c, FFN_HDR - (CONV_FFN - 1):, :]


def _scratch_shapes(n_streams, rows):
    S, T = n_streams, rows
    return [
        pltpu.VMEM((S, SUBLANES, D_LRU), F32),
        pltpu.VMEM((S, LRU_SLABS, LRU_HDR, LANES), F32),
        pltpu.VMEM((S, POOL_SLABS, POOL_HDR, LANES), F32),
        pltpu.VMEM((S, FFN_SLABS, FFN_HDR, LANES), F32),
        pltpu.VMEM((S, LRU_SLABS, LRU_HDR + T, LANES), F32),
        pltpu.VMEM((S, POOL_SLABS, POOL_HDR + T, LANES), F32),
        pltpu.VMEM((2, 2, S, CHUNK_SLABS, FFN_HDR + T, LANES), F32),
        pltpu.VMEM((LRU_SLABS, SCAN_HDR + T, LANES), F32),
        pltpu.VMEM((LRU_SLABS, SCAN_HDR + T, LANES), F32),
        pltpu.VMEM((S * T, D_LRU), F32),
    ]


def _vmem_limit(n_streams, rows, weights):
    m = n_streams * rows
    weight_bytes = sum(w.size * w.dtype.itemsize for w in weights)
    io_bytes = 2 * 2 * m * D_MODEL * 4
    scratch_bytes = 0
    for sc in _scratch_shapes(n_streams, rows):
        n = 4
        for d in sc.shape:
            n *= d
        scratch_bytes += n
    live_values = 10 * m * D_MODEL * 4
    return min(weight_bytes + io_bytes + scratch_bytes + live_values, VMEM_BYTES_V7X - (6 << 20))


def _state_shapes(count):
    return [
        jax.ShapeDtypeStruct((count, D_LRU), F32),
        jax.ShapeDtypeStruct((count, CONV_LRU - 1, D_LRU), F32),
        jax.ShapeDtypeStruct((count, POOL_BUF, D_POOL), F32),
        jax.ShapeDtypeStruct((count, CONV_FFN - 1, 2 * D_FF), F32),
    ]


def _run_streams(xs, states, weights, *, n_streams, rows, start_pos, n_fresh, y_skip, out_groups,
                 tiled, name):
    S, T = n_streams, rows
    m = S * T
    n_steps = xs[0].shape[0] // m if tiled else 1
    const = lambda shape: pl.BlockSpec(shape, lambda i, _n=len(shape): (0,) * _n)
    whole = lambda a: pl.BlockSpec(a.shape, lambda i, _n=a.ndim: (0,) * _n,
                                   pipeline_mode=pl.Buffered(1))
    x_specs = ([pl.BlockSpec((m, D_MODEL), lambda i: (i, 0))] if tiled else [whole(a) for a in xs])
    y_rows = n_steps * m - y_skip
    y_spec = pl.BlockSpec((m, D_MODEL), lambda i: (i, 0)) if tiled else const((y_rows, D_MODEL))
    out_shape = [jax.ShapeDtypeStruct((y_rows, D_MODEL), F32)]
    for _, count in out_groups:
        out_shape += _state_shapes(count)
    kernel = functools.partial(
        _stream_kernel, n_streams=S, rows=T, start_pos=start_pos, n_x=len(xs), n_fresh=n_fresh,
        y_skip=y_skip, out_groups=out_groups)
    outs = pl.pallas_call(
        kernel,
        grid=(n_steps,),
        in_specs=x_specs + [whole(a) for a in states] + [whole(w) for w in weights],
        out_specs=[y_spec] + [const(o.shape) for o in out_shape[1:]],
        out_shape=out_shape,
        scratch_shapes=_scratch_shapes(S, T),
        compiler_params=pltpu.CompilerParams(
            dimension_semantics=("arbitrary",),
            vmem_limit_bytes=_vmem_limit(S, T, weights)),
        name=name,
    )(*xs, *states, *weights)
    return outs[0], [outs[1 + 4 * g:5 + 4 * g] for g in range(len(out_groups))]


def _cast_kernel(*refs):
    n = len(refs) // 2
    for src_ref, dst_ref in zip(refs[:n], refs[n:]):
        dst_ref[...] = src_ref[...].astype(dst_ref.dtype).reshape(dst_ref.shape)


def _cast_to_bf16(name, n_steps, arrays, in_blocks, in_maps, out_shapes, out_blocks, out_maps):
    return pl.pallas_call(
        _cast_kernel,
        grid=(n_steps,),
        in_specs=[pl.BlockSpec(b, m) for b, m in zip(in_blocks, in_maps)],
        out_specs=[pl.BlockSpec(b, m) for b, m in zip(out_blocks, out_maps)],
        out_shape=[jax.ShapeDtypeStruct(s, BF16) for s in out_shapes],
        compiler_params=pltpu.CompilerParams(dimension_semantics=("arbitrary",)),
        name=name,
    )(*arrays)


def _block_diag(blocks, per_tile):
    n, k, _ = blocks.shape
    eye = jnp.eye(per_tile, dtype=blocks.dtype)
    tiles = blocks.reshape(n // per_tile, per_tile, k, k)
    return jnp.einsum('qpij,pr->qpirj', tiles, eye).reshape(n // per_tile, per_tile * k, per_tile * k)


def kernel(x_prompt, x_sample, state_lru_h, state_lru_conv, state_pool, state_ffn_conv, meta_tokens, norm_mix_g, w_in, conv_lru_w, conv_lru_b, gate_a_w, gate_a_b, gate_x_w, gate_x_b, lru_lambda, pool_w, pool_b, pool_scale, gn_lru, gn_pool, w_out, norm_ffn_g, w_up, ffn_conv_w, ffn_conv_b, w_down, final_norm_g):
    assert x_prompt.shape[0] == 1 and norm_mix_g.shape[0] == 1
    n_sample, sample_rows = x_sample.shape[:2]
    assert sample_rows == N_META

    heads_per_tile = MXU_DIM // LRU_HEAD
    w_gate = jnp.concatenate(
        [_block_diag(gate_a_w[0], heads_per_tile), _block_diag(gate_x_w[0], heads_per_tile)],
        axis=2).astype(BF16)
    w_pool = _block_diag(pool_w[0], MXU_DIM // POOL_GROUP).astype(BF16)
    n_blocks = 2 * FFN_CHUNKS
    down_rows = D_FF // n_blocks
    w_up_blocks, w_down_bf = _cast_to_bf16(
        "cast_ffn_weights", n_blocks, (w_up[0], w_down[0]),
        in_blocks=[(D_MODEL, FFN_CHUNK), (down_rows, D_MODEL)],
        in_maps=[lambda j: (0, j), lambda j: (j, 0)],
        out_shapes=[(n_blocks, D_MODEL, FFN_CHUNK), (D_FF, D_MODEL)],
        out_blocks=[(1, D_MODEL, FFN_CHUNK), (down_rows, D_MODEL)],
        out_maps=[lambda j: (j, 0, 0), lambda j: (j, 0)])
    mix_steps = 4
    mix_rows = D_MODEL // mix_steps
    w_in_bf, w_out_bf = _cast_to_bf16(
        "cast_mixer_weights", mix_steps, (w_in[0], w_out[0]),
        in_blocks=[(mix_rows, w_in.shape[2]), (mix_rows, D_MODEL)],
        in_maps=[lambda j: (j, 0), lambda j: (j, 0)],
        out_shapes=[w_in.shape[1:], w_out.shape[1:]],
        out_blocks=[(mix_rows, w_in.shape[2]), (mix_rows, D_MODEL)],
        out_maps=[lambda j: (j, 0), lambda j: (j, 0)])
    w_down_blocks = w_down_bf.reshape(FFN_CHUNKS, FFN_CHUNK, D_MODEL)
    row = lambda a: a.reshape(1, -1)
    weights = (
        norm_mix_g, w_in_bf, conv_lru_w[0], conv_lru_b, w_gate, gate_a_b, gate_x_b,
        lru_lambda, w_pool, pool_b, pool_scale, gn_lru, gn_pool, w_out_bf,
        norm_ffn_g, w_up_blocks, ffn_conv_w[0], ffn_conv_b, w_down_blocks, row(final_norm_g))

    y_s, (meta_state, sample_state) = _run_streams(
        (meta_tokens, x_sample.reshape(-1, D_MODEL)),
        (state_lru_h[0], state_lru_conv[0], state_pool[0], state_ffn_conv[0]), weights,
        n_streams=1 + n_sample, rows=N_META, start_pos=(0,) + (PAST_LEN,) * n_sample, n_fresh=1,
        y_skip=N_META, out_groups=((0, 1), (1, n_sample)), tiled=False, name="short_streams")

    y_p, (prompt_state,) = _run_streams(
        (x_prompt[0],), meta_state, weights, n_streams=1, rows=PROMPT_TILE, start_pos=(N_META,),
        n_fresh=0, y_skip=0, out_groups=((0, 1),), tiled=True, name="prompt_stream")

    return (y_p[None], y_s.reshape(x_sample.shape), *[a[None] for a in prompt_state],
            *[a[None] for a in sample_state])
```

```python
import functools

import jax
import jax.numpy as jnp
from jax import lax
from jax.experimental import pallas as pl
from jax.experimental.pallas import tpu as pltpu

D_MODEL = 1024
N_META = 16
PAST_LEN = 1024
D_LRU = 512
N_LRU_HEADS = 8
LRU_HEAD = D_LRU // N_LRU_HEADS
LRU_C = 8.0
CONV_LRU = 4
D_POOL = 512
POOL_WINDOWS = (2, 4, 8, 16)
POOL_GROUP = D_POOL // len(POOL_WINDOWS)
POOL_BUF = max(POOL_WINDOWS) - 1
D_FF = 3 * D_MODEL
CONV_FFN = 3
EPS = 1e-6

SUBLANES = 8
LANES = 128
MXU_DIM = 256
VMEM_BYTES_V7X = 64 * 1024 * 1024

LRU_HDR = SUBLANES
POOL_HDR = 2 * SUBLANES
FFN_HDR = SUBLANES
SCAN_HDR = SUBLANES
SCAN_SHIFTS = (1, 2, 4)
FFN_CHUNK = 512
FFN_CHUNKS = D_FF // FFN_CHUNK
LRU_SLABS = D_LRU // LANES
POOL_SLABS = D_POOL // LANES
CHUNK_SLABS = FFN_CHUNK // LANES
FFN_SLABS = 2 * D_FF // LANES
PROMPT_TILE = 512

VEC_ROWS = SUBLANES
VEC_LAYOUT = {}


def _vec_layout():
    for k in range(CONV_FFN):
        VEC_LAYOUT[f"ffn_w{k}"] = (k, 0, 2 * D_FF)
    VEC_LAYOUT["ffn_b"] = (CONV_FFN, 0, 2 * D_FF)
    col = 0
    for name, width in (("norm_mix_g", D_MODEL), ("norm_ffn_g", D_MODEL), ("final_g", D_MODEL),
                        ("conv_b", D_LRU), ("ba", D_LRU), ("bx", D_LRU), ("lam", D_LRU),
                        ("pool_b", D_POOL), ("pool_scale", D_POOL)):
        VEC_LAYOUT[name] = (CONV_FFN + 1, col, width)
        col += width
    assert col == 2 * D_FF
    col = 0
    for name, width in ([(f"conv_w{k}", D_LRU) for k in range(CONV_LRU)]
                        + [("gn_lru", D_LRU), ("gn_pool", D_POOL)]):
        VEC_LAYOUT[name] = (CONV_FFN + 2, col, width)
        col += width
    assert col <= 2 * D_FF and CONV_FFN + 2 < VEC_ROWS


_vec_layout()


def _pack_vectors(named):
    rows = []
    for r in range(VEC_ROWS):
        parts = sorted((c0, name) for name, (row, c0, _) in VEC_LAYOUT.items() if row == r)
        pieces = [named[name].reshape(-1) for _, name in parts]
        used = sum(p.shape[0] for p in pieces)
        if used < 2 * D_FF:
            pieces.append(jnp.zeros((2 * D_FF - used,), F32))
        rows.append(jnp.concatenate(pieces))
    return jnp.stack(rows)

F32 = jnp.float32
BF16 = jnp.bfloat16


def _rmsnorm(x, g):
    return x * lax.rsqrt(jnp.mean(x * x, axis=-1, keepdims=True) + EPS) * g


def _dot(a, b):
    return jnp.dot(a, b, preferred_element_type=F32)


GELU_C0 = (2.0 / jnp.pi) ** 0.5
GELU_C1 = 0.044715 * GELU_C0


def _gelu(x):
    half = 0.5 * x
    return half + half * jnp.tanh(x * (GELU_C0 + GELU_C1 * (x * x)))


def _lanes(c):
    return slice(c * LANES, (c + 1) * LANES)


def _stream_kernel(*refs, n_streams, rows, start_pos, n_x, n_fresh, y_skip, out_groups):
    S, T = n_streams, rows
    refs = list(refs)
    take = lambda n: [refs.pop(0) for _ in range(n)]
    x_refs = take(n_x)
    h0_ref, conv0_ref, pool0_ref, ffn0_ref = take(4)
    vec_ref, w_in_ref, w_gate_ref, w_pool_ref, w_out_ref, w_up_ref, w_down_ref = take(7)
    y_ref, = take(1)
    state_out_refs = [take(4) for _ in out_groups]
    (h_carry, lru_carry, pool_carry, ffn_carry, ext_lru, ext_pool, ext_up,
     scan_a, scan_b, scan_h) = refs
    step = pl.program_id(0)

    def vec(name, slab=None):
        row, c0, width = VEC_LAYOUT[name]
        if slab is None:
            return vec_ref[row:row + 1, c0:c0 + width]
        return vec_ref[row:row + 1, c0 + slab * LANES:c0 + (slab + 1) * LANES]

    @pl.when(step == 0)
    def _load_state():
        lru_carry[...] = jnp.zeros_like(lru_carry)
        pool_carry[...] = jnp.zeros_like(pool_carry)
        ffn_carry[...] = jnp.zeros_like(ffn_carry)
        scan_a[:, :SCAN_HDR, :] = jnp.ones((LRU_SLABS, SCAN_HDR, LANES), F32)
        scan_b[:, :SCAN_HDR, :] = jnp.zeros((LRU_SLABS, SCAN_HDR, LANES), F32)
        for s in range(n_fresh):
            h_carry[s] = jnp.zeros((SUBLANES, D_LRU), F32)
        for s in range(n_fresh, S):
            p = s - n_fresh
            h_carry[s] = jnp.broadcast_to(h0_ref[p:p + 1, :], (SUBLANES, D_LRU))
            for c in range(LRU_SLABS):
                lru_carry[s, c, LRU_HDR - (CONV_LRU - 1):, :] = conv0_ref[p, :, _lanes(c)]
            for c in range(POOL_SLABS):
                pool_carry[s, c, POOL_HDR - POOL_BUF:, :] = pool0_ref[p, :, _lanes(c)]
            for c in range(FFN_SLABS):
                ffn_carry[s, c, FFN_HDR - (CONV_FFN - 1):, :] = ffn0_ref[p, :, _lanes(c)]

    x = x_refs[0][...] if n_x == 1 else jnp.concatenate([r[...] for r in x_refs], axis=0)

    u = _dot(_rmsnorm(x, vec("norm_mix_g")).astype(BF16), w_in_ref[...])
    u_lru = u[:, :D_LRU]
    g_lru = u[:, D_LRU:2 * D_LRU]
    u_pool = u[:, 2 * D_LRU:]

    lam = vec("lam")
    c_log_sig = LRU_C * (jnp.minimum(lam, 0.0) - jnp.log1p(jnp.exp(-jnp.abs(lam))))

    y_lru_parts = []
    m_parts = []
    for s in range(S):
        r0 = s * T
        c_slabs = []
        for c in range(LRU_SLABS):
            u_c = u_lru[r0:r0 + T, _lanes(c)]
            ext_lru[s, c, :LRU_HDR, :] = lru_carry[s, c]
            ext_lru[s, c, LRU_HDR:, :] = u_c
            lru_carry[s, c] = ext_lru[s, c, T:, :]
            acc_c = vec("conv_b", c) + u_c * vec(f"conv_w{CONV_LRU - 1}", c)
            for k in range(1, CONV_LRU):
                acc_c = acc_c + (ext_lru[s, c, LRU_HDR - k:LRU_HDR - k + T, :]
                                 * vec(f"conv_w{CONV_LRU - 1 - k}", c))
            c_slabs.append(acc_c)
        c = jnp.concatenate(c_slabs, axis=1)

        cb = c.astype(BF16)
        gates = [_dot(cb[:, q * MXU_DIM:(q + 1) * MXU_DIM], w_gate_ref[q])
                 for q in range(D_LRU // MXU_DIM)]
        r = jax.nn.sigmoid(jnp.concatenate([g[:, :MXU_DIM] for g in gates], axis=1) + vec("ba"))
        i = jax.nn.sigmoid(jnp.concatenate([g[:, MXU_DIM:] for g in gates], axis=1) + vec("bx"))
        log_a = r * c_log_sig
        a = jnp.exp(log_a)
        mult = jnp.sqrt(jnp.tanh(-log_a) * (a * a + 1.0))
        b = mult * (i * c)

        for d in SCAN_SHIFTS:
            a_prev, b_prev = [], []
            for cc in range(LRU_SLABS):
                scan_a[cc, SCAN_HDR:, :] = a[:, _lanes(cc)]
                scan_b[cc, SCAN_HDR:, :] = b[:, _lanes(cc)]
                a_prev.append(scan_a[cc, SCAN_HDR - d:SCAN_HDR - d + T, :])
                b_prev.append(scan_b[cc, SCAN_HDR - d:SCAN_HDR - d + T, :])
            b = a * jnp.concatenate(b_prev, axis=1) + b
            a = a * jnp.concatenate(a_prev, axis=1)
        h = h_carry[s]
        for g in range(T // SUBLANES):
            lo = g * SUBLANES
            h = a[lo:lo + SUBLANES] * h + b[lo:lo + SUBLANES]
            scan_h[r0 + lo:r0 + lo + SUBLANES, :] = h
        h_carry[s] = jnp.broadcast_to(scan_h[r0 + T - 1:r0 + T, :], (SUBLANES, D_LRU))
        y_lru_parts.append(scan_h[r0:r0 + T, :] * _gelu(g_lru[r0:r0 + T]))

        ramp_up = start_pos[s] + 1 < max(POOL_WINDOWS)
        if ramp_up:
            pos1 = (lax.broadcasted_iota(jnp.int32, (T, POOL_GROUP), 0) + (start_pos[s] + 1)).astype(F32)
        means = []
        for gi, w in enumerate(POOL_WINDOWS):
            tot = u_pool[r0:r0 + T, _lanes(gi)]
            ext_pool[s, gi, :POOL_HDR, :] = pool_carry[s, gi]
            ext_pool[s, gi, POOL_HDR:, :] = tot
            pool_carry[s, gi] = ext_pool[s, gi, T:, :]
            for k in range(1, w):
                tot = tot + ext_pool[s, gi, POOL_HDR - k:POOL_HDR - k + T, :]
            means.append(tot / jnp.minimum(pos1, float(w)) if ramp_up else tot * (1.0 / w))
        m_parts.append(jnp.concatenate(means, axis=1) - u_pool[r0:r0 + T])

    y_lru = y_lru_parts[0] if S == 1 else jnp.concatenate(y_lru_parts, axis=0)
    m = (m_parts[0] if S == 1 else jnp.concatenate(m_parts, axis=0)).astype(BF16)

    y_pool = jnp.concatenate(
        [_dot(m[:, q * MXU_DIM:(q + 1) * MXU_DIM], w_pool_ref[q]) for q in range(D_POOL // MXU_DIM)],
        axis=1)
    y_pool = (y_pool + vec("pool_b")) * vec("pool_scale")

    mix_in = jnp.concatenate(
        [_rmsnorm(y_lru, vec("gn_lru")), _rmsnorm(y_pool, vec("gn_pool"))], axis=1)
    x1 = x + _dot(mix_in.astype(BF16), w_out_ref[...])

    xn = _rmsnorm(x1, vec("norm_ffn_g")).astype(BF16)
    up_pair = lambda j: [_dot(xn, w_up_ref[half * FFN_CHUNKS + j]) for half in range(2)]
    acc = jnp.zeros((S * T, D_MODEL), F32)
    ups = up_pair(0)
    for j in range(FFN_CHUNKS):
        ups_next = up_pair(j + 1) if j + 1 < FFN_CHUNKS else None
        halves = []
        for half in range(2):
            up = ups[half]
            parts = []
            for s in range(S):
                r0 = s * T
                slabs = []
                for c in range(CHUNK_SLABS):
                    gs = (half * FFN_CHUNKS + j) * CHUNK_SLABS + c
                    ext = ext_up.at[j % 2, half, s, c]
                    up_c = up[r0:r0 + T, _lanes(c)]
                    ext[:FFN_HDR, :] = ffn_carry[s, gs]
                    ext[FFN_HDR:, :] = up_c
                    ffn_carry[s, gs] = ext[T:, :]
                    upc = vec("ffn_b", gs) + up_c * vec(f"ffn_w{CONV_FFN - 1}", gs)
                    for k in range(1, CONV_FFN):
                        upc = upc + ext[FFN_HDR - k:FFN_HDR - k + T, :] * vec(f"ffn_w{CONV_FFN - 1 - k}", gs)
                    slabs.append(upc)
                parts.append(jnp.concatenate(slabs, axis=1))
            halves.append(parts[0] if S == 1 else jnp.concatenate(parts, axis=0))
        act = (_gelu(halves[0]) * halves[1]).astype(BF16)
        acc = acc + _dot(act, w_down_ref[j])
        ups = ups_next

    y = _rmsnorm(x1 + acc, vec("final_g"))
    y_ref[...] = y[y_skip:]

    @pl.when(step == pl.num_programs(0) - 1)
    def _store_state():
        for (first, count), (h_out_ref, conv_out_ref, pool_out_ref, ffn_out_ref) in zip(
                out_groups, state_out_refs):
            for p in range(count):
                s = first + p
                h_out_ref[p:p + 1, :] = h_carry[s, :1, :]
                for c in range(LRU_SLABS):
                    conv_out_ref[p, :, _lanes(c)] = lru_carry[s, c, LRU_HDR - (CONV_LRU - 1):, :]
                for c in range(POOL_SLABS):
                    pool_out_ref[p, :, _lanes(c)] = pool_carry[s, c, POOL_HDR - POOL_BUF:, :]
                for c in range(FFN_SLABS):
                    ffn_out_ref[p, :, _lanes(c)] = ffn_carry[s, c, FFN_HDR - (CONV_FFN - 1):, :]


def _scratch_shapes(n_streams, rows):
    S, T = n_streams, rows
    return [
        pltpu.VMEM((S, SUBLANES, D_LRU), F32),
        pltpu.VMEM((S, LRU_SLABS, LRU_HDR, LANES), F32),
        pltpu.VMEM((S, POOL_SLABS, POOL_HDR, LANES), F32),
        pltpu.VMEM((S, FFN_SLABS, FFN_HDR, LANES), F32),
        pltpu.VMEM((S, LRU_SLABS, LRU_HDR + T, LANES), F32),
        pltpu.VMEM((S, POOL_SLABS, POOL_HDR + T, LANES), F32),
        pltpu.VMEM((2, 2, S, CHUNK_SLABS, FFN_HDR + T, LANES), F32),
        pltpu.VMEM((LRU_SLABS, SCAN_HDR + T, LANES), F32),
        pltpu.VMEM((LRU_SLABS, SCAN_HDR + T, LANES), F32),
        pltpu.VMEM((S * T, D_LRU), F32),
    ]


def _vmem_limit(n_streams, rows, weights):
    m = n_streams * rows
    weight_bytes = sum(w.size * w.dtype.itemsize for w in weights)
    io_bytes = 2 * 2 * m * D_MODEL * 4
    scratch_bytes = 0
    for sc in _scratch_shapes(n_streams, rows):
        n = 4
        for d in sc.shape:
            n *= d
        scratch_bytes += n
    live_values = 10 * m * D_MODEL * 4
    return min(weight_bytes + io_bytes + scratch_bytes + live_values, VMEM_BYTES_V7X - (6 << 20))


def _state_shapes(count):
    return [
        jax.ShapeDtypeStruct((count, D_LRU), F32),
        jax.ShapeDtypeStruct((count, CONV_LRU - 1, D_LRU), F32),
        jax.ShapeDtypeStruct((count, POOL_BUF, D_POOL), F32),
        jax.ShapeDtypeStruct((count, CONV_FFN - 1, 2 * D_FF), F32),
    ]


def _run_streams(xs, states, weights, *, n_streams, rows, start_pos, n_fresh, y_skip, out_groups,
                 tiled, name):
    S, T = n_streams, rows
    m = S * T
    n_steps = xs[0].shape[0] // m if tiled else 1
    const = lambda shape: pl.BlockSpec(shape, lambda i, _n=len(shape): (0,) * _n)
    whole = lambda a: pl.BlockSpec(a.shape, lambda i, _n=a.ndim: (0,) * _n,
                                   pipeline_mode=pl.Buffered(1))
    x_specs = ([pl.BlockSpec((m, D_MODEL), lambda i: (i, 0))] if tiled else [whole(a) for a in xs])
    y_rows = n_steps * m - y_skip
    y_spec = pl.BlockSpec((m, D_MODEL), lambda i: (i, 0)) if tiled else const((y_rows, D_MODEL))
    out_shape = [jax.ShapeDtypeStruct((y_rows, D_MODEL), F32)]
    for _, count in out_groups:
        out_shape += _state_shapes(count)
    kernel = functools.partial(
        _stream_kernel, n_streams=S, rows=T, start_pos=start_pos, n_x=len(xs), n_fresh=n_fresh,
        y_skip=y_skip, out_groups=out_groups)
    outs = pl.pallas_call(
        kernel,
        grid=(n_steps,),
        in_specs=x_specs + [whole(a) for a in states] + [whole(w) for w in weights],
        out_specs=[y_spec] + [const(o.shape) for o in out_shape[1:]],
        out_shape=out_shape,
        scratch_shapes=_scratch_shapes(S, T),
        compiler_params=pltpu.CompilerParams(
            dimension_semantics=("arbitrary",),
            vmem_limit_bytes=_vmem_limit(S, T, weights)),
        name=name,
    )(*xs, *states, *weights)
    return outs[0], [outs[1 + 4 * g:5 + 4 * g] for g in range(len(out_groups))]


def _cast_kernel(*refs):
    n = len(refs) // 2
    for src_ref, dst_ref in zip(refs[:n], refs[n:]):
        dst_ref[...] = src_ref[...].astype(dst_ref.dtype).reshape(dst_ref.shape)


def _cast_to_bf16(name, n_steps, arrays, in_blocks, in_maps, out_shapes, out_blocks, out_maps):
    return pl.pallas_call(
        _cast_kernel,
        grid=(n_steps,),
        in_specs=[pl.BlockSpec(b, m) for b, m in zip(in_blocks, in_maps)],
        out_specs=[pl.BlockSpec(b, m) for b, m in zip(out_blocks, out_maps)],
        out_shape=[jax.ShapeDtypeStruct(s, BF16) for s in out_shapes],
        compiler_params=pltpu.CompilerParams(dimension_semantics=("arbitrary",)),
        name=name,
    )(*arrays)


def _block_diag(blocks, per_tile):
    n, k, _ = blocks.shape
    eye = jnp.eye(per_tile, dtype=blocks.dtype)
    tiles = blocks.reshape(n // per_tile, per_tile, k, k)
    return jnp.einsum('qpij,pr->qpirj', tiles, eye).reshape(n // per_tile, per_tile * k, per_tile * k)


def kernel(x_prompt, x_sample, state_lru_h, state_lru_conv, state_pool, state_ffn_conv, meta_tokens, norm_mix_g, w_in, conv_lru_w, conv_lru_b, gate_a_w, gate_a_b, gate_x_w, gate_x_b, lru_lambda, pool_w, pool_b, pool_scale, gn_lru, gn_pool, w_out, norm_ffn_g, w_up, ffn_conv_w, ffn_conv_b, w_down, final_norm_g):
    assert x_prompt.shape[0] == 1 and norm_mix_g.shape[0] == 1
    n_sample, sample_rows = x_sample.shape[:2]
    assert sample_rows == N_META

    heads_per_tile = MXU_DIM // LRU_HEAD
    w_gate = jnp.concatenate(
        [_block_diag(gate_a_w[0], heads_per_tile), _block_diag(gate_x_w[0], heads_per_tile)],
        axis=2).astype(BF16)
    w_pool = _block_diag(pool_w[0], MXU_DIM // POOL_GROUP).astype(BF16)
    n_blocks = 2 * FFN_CHUNKS
    down_rows = D_FF // n_blocks
    w_up_blocks, w_down_bf = _cast_to_bf16(
        "cast_ffn_weights", n_blocks, (w_up[0], w_down[0]),
        in_blocks=[(D_MODEL, FFN_CHUNK), (down_rows, D_MODEL)],
        in_maps=[lambda j: (0, j), lambda j: (j, 0)],
        out_shapes=[(n_blocks, D_MODEL, FFN_CHUNK), (D_FF, D_MODEL)],
        out_blocks=[(1, D_MODEL, FFN_CHUNK), (down_rows, D_MODEL)],
        out_maps=[lambda j: (j, 0, 0), lambda j: (j, 0)])
    mix_steps = 4
    mix_rows = D_MODEL // mix_steps
    w_in_bf, w_out_bf = _cast_to_bf16(
        "cast_mixer_weights", mix_steps, (w_in[0], w_out[0]),
        in_blocks=[(mix_rows, w_in.shape[2]), (mix_rows, D_MODEL)],
        in_maps=[lambda j: (j, 0), lambda j: (j, 0)],
        out_shapes=[w_in.shape[1:], w_out.shape[1:]],
        out_blocks=[(mix_rows, w_in.shape[2]), (mix_rows, D_MODEL)],
        out_maps=[lambda j: (j, 0), lambda j: (j, 0)])
    w_down_blocks = w_down_bf.reshape(FFN_CHUNKS, FFN_CHUNK, D_MODEL)
    vectors = _pack_vectors({
        **{f"ffn_w{k}": ffn_conv_w[0, k] for k in range(CONV_FFN)},
        **{f"conv_w{k}": conv_lru_w[0, k] for k in range(CONV_LRU)},
        "ffn_b": ffn_conv_b, "norm_mix_g": norm_mix_g, "norm_ffn_g": norm_ffn_g, "final_g": final_norm_g,
        "conv_b": conv_lru_b, "ba": gate_a_b, "bx": gate_x_b, "lam": lru_lambda, "pool_b": pool_b,
        "pool_scale": pool_scale, "gn_lru": gn_lru, "gn_pool": gn_pool})
    weights = (vectors, w_in_bf, w_gate, w_pool, w_out_bf, w_up_blocks, w_down_blocks)

    y_s, (meta_state, sample_state) = _run_streams(
        (meta_tokens, x_sample.reshape(-1, D_MODEL)),
        (state_lru_h[0], state_lru_conv[0], state_pool[0], state_ffn_conv[0]), weights,
        n_streams=1 + n_sample, rows=N_META, start_pos=(0,) + (PAST_LEN,) * n_sample, n_fresh=1,
        y_skip=N_META, out_groups=((0, 1), (1, n_sample)), tiled=False, name="short_streams")

    y_p, (prompt_state,) = _run_streams(
        (x_prompt[0],), meta_state, weights, n_streams=1, rows=PROMPT_TILE, start_pos=(N_META,),
        n_fresh=0, y_skip=0, out_groups=((0, 1),), tiled=True, name="prompt_stream")

    return (y_p[None], y_s.reshape(x_sample.shape), *[a[None] for a in prompt_state],
            *[a[None] for a in sample_state])
```

```python
import functools

import jax
import jax.numpy as jnp
from jax import lax
from jax.experimental import pallas as pl
from jax.experimental.pallas import tpu as pltpu

D_MODEL = 1024
N_META = 16
PAST_LEN = 1024
D_LRU = 512
N_LRU_HEADS = 8
LRU_HEAD = D_LRU // N_LRU_HEADS
LRU_C = 8.0
CONV_LRU = 4
D_POOL = 512
POOL_WINDOWS = (2, 4, 8, 16)
POOL_GROUP = D_POOL // len(POOL_WINDOWS)
POOL_BUF = max(POOL_WINDOWS) - 1
D_FF = 3 * D_MODEL
CONV_FFN = 3
EPS = 1e-6

SUBLANES = 8
LANES = 128
MXU_DIM = 256
VMEM_BYTES_V7X = 64 * 1024 * 1024

LRU_HDR = SUBLANES
POOL_HDR = 2 * SUBLANES
FFN_HDR = SUBLANES
SCAN_HDR = SUBLANES
SCAN_SHIFTS = (1, 2, 4)
FFN_CHUNK = 512
FFN_CHUNKS = D_FF // FFN_CHUNK
LRU_SLABS = D_LRU // LANES
POOL_SLABS = D_POOL // LANES
CHUNK_SLABS = FFN_CHUNK // LANES
FFN_SLABS = 2 * D_FF // LANES
PROMPT_TILE = 512

VEC_ROWS = SUBLANES
VEC_LAYOUT = {}


def _vec_layout():
    for k in range(CONV_FFN):
        VEC_LAYOUT[f"ffn_w{k}"] = (k, 0, 2 * D_FF)
    VEC_LAYOUT["ffn_b"] = (CONV_FFN, 0, 2 * D_FF)
    col = 0
    for name, width in (("norm_mix_g", D_MODEL), ("norm_ffn_g", D_MODEL), ("final_g", D_MODEL),
                        ("conv_b", D_LRU), ("ba", D_LRU), ("bx", D_LRU), ("lam", D_LRU),
                        ("pool_b", D_POOL), ("pool_scale", D_POOL)):
        VEC_LAYOUT[name] = (CONV_FFN + 1, col, width)
        col += width
    assert col == 2 * D_FF
    col = 0
    for name, width in ([(f"conv_w{k}", D_LRU) for k in range(CONV_LRU)]
                        + [("gn_lru", D_LRU), ("gn_pool", D_POOL)]):
        VEC_LAYOUT[name] = (CONV_FFN + 2, col, width)
        col += width
    assert col <= 2 * D_FF and CONV_FFN + 2 < VEC_ROWS


_vec_layout()


F32 = jnp.float32
BF16 = jnp.bfloat16


def _rmsnorm(x, g):
    return x * lax.rsqrt(jnp.mean(x * x, axis=-1, keepdims=True) + EPS) * g


def _dot(a, b):
    return jnp.dot(a, b, preferred_element_type=F32)


GELU_C0 = (2.0 / jnp.pi) ** 0.5
GELU_C1 = 0.044715 * GELU_C0


def _gelu(x):
    half = 0.5 * x
    return half + half * jnp.tanh(x * (GELU_C0 + GELU_C1 * (x * x)))


def _lanes(c):
    return slice(c * LANES, (c + 1) * LANES)


def _stream_kernel(*refs, n_streams, rows, start_pos, n_x, n_fresh, y_skip, out_groups):
    S, T = n_streams, rows
    refs = list(refs)
    take = lambda n: [refs.pop(0) for _ in range(n)]
    x_refs = take(n_x)
    h0_ref, conv0_ref, pool0_ref, ffn0_ref = take(4)
    vec_ref, w_in_ref, w_gate_ref, w_pool_ref, w_out_ref, w_up_ref, w_down_ref = take(7)
    y_ref, = take(1)
    state_out_refs = [take(4) for _ in out_groups]
    (h_carry, lru_carry, pool_carry, ffn_carry, ext_lru, ext_pool, ext_up,
     scan_a, scan_b, scan_h) = refs
    step = pl.program_id(0)

    def vec(name, slab=None):
        row, c0, width = VEC_LAYOUT[name]
        if slab is None:
            return vec_ref[row:row + 1, c0:c0 + width]
        return vec_ref[row:row + 1, c0 + slab * LANES:c0 + (slab + 1) * LANES]

    @pl.when(step == 0)
    def _load_state():
        lru_carry[...] = jnp.zeros_like(lru_carry)
        pool_carry[...] = jnp.zeros_like(pool_carry)
        ffn_carry[...] = jnp.zeros_like(ffn_carry)
        scan_a[:, :SCAN_HDR, :] = jnp.ones((LRU_SLABS, SCAN_HDR, LANES), F32)
        scan_b[:, :SCAN_HDR, :] = jnp.zeros((LRU_SLABS, SCAN_HDR, LANES), F32)
        for s in range(n_fresh):
            h_carry[s] = jnp.zeros((SUBLANES, D_LRU), F32)
        for s in range(n_fresh, S):
            p = s - n_fresh
            h_carry[s] = jnp.broadcast_to(h0_ref[0, p:p + 1, :], (SUBLANES, D_LRU))
            for c in range(LRU_SLABS):
                lru_carry[s, c, LRU_HDR - (CONV_LRU - 1):, :] = conv0_ref[0, p, :, _lanes(c)]
            for c in range(POOL_SLABS):
                pool_carry[s, c, POOL_HDR - POOL_BUF:, :] = pool0_ref[0, p, :, _lanes(c)]
            for c in range(FFN_SLABS):
                ffn_carry[s, c, FFN_HDR - (CONV_FFN - 1):, :] = ffn0_ref[0, p, :, _lanes(c)]

    x = jnp.concatenate([r[...].reshape(-1, D_MODEL) for r in x_refs], axis=0)

    u = _dot(_rmsnorm(x, vec("norm_mix_g")).astype(BF16), w_in_ref[...])
    u_lru = u[:, :D_LRU]
    g_lru = u[:, D_LRU:2 * D_LRU]
    u_pool = u[:, 2 * D_LRU:]

    lam = vec("lam")
    c_log_sig = LRU_C * (jnp.minimum(lam, 0.0) - jnp.log1p(jnp.exp(-jnp.abs(lam))))

    y_lru_parts = []
    m_parts = []
    for s in range(S):
        r0 = s * T
        c_slabs = []
        for c in range(LRU_SLABS):
            u_c = u_lru[r0:r0 + T, _lanes(c)]
            ext_lru[s, c, :LRU_HDR, :] = lru_carry[s, c]
            ext_lru[s, c, LRU_HDR:, :] = u_c
            lru_carry[s, c] = ext_lru[s, c, T:, :]
            acc_c = vec("conv_b", c) + u_c * vec(f"conv_w{CONV_LRU - 1}", c)
            for k in range(1, CONV_LRU):
                acc_c = acc_c + (ext_lru[s, c, LRU_HDR - k:LRU_HDR - k + T, :]
                                 * vec(f"conv_w{CONV_LRU - 1 - k}", c))
            c_slabs.append(acc_c)
        c = jnp.concatenate(c_slabs, axis=1)

        cb = c.astype(BF16)
        gates = [_dot(cb[:, q * MXU_DIM:(q + 1) * MXU_DIM], w_gate_ref[q])
                 for q in range(D_LRU // MXU_DIM)]
        r = jax.nn.sigmoid(jnp.concatenate([g[:, :MXU_DIM] for g in gates], axis=1) + vec("ba"))
        i = jax.nn.sigmoid(jnp.concatenate([g[:, MXU_DIM:] for g in gates], axis=1) + vec("bx"))
        log_a = r * c_log_sig
        a = jnp.exp(log_a)
        one_minus_a2 = jnp.tanh(log_a) * (-1.0 - a * a)
        mult = jnp.where(one_minus_a2 > 0.0, one_minus_a2 * lax.rsqrt(one_minus_a2), 0.0)
        b = mult * (i * c)

        for d in SCAN_SHIFTS:
            a_prev, b_prev = [], []
            for cc in range(LRU_SLABS):
                scan_a[cc, SCAN_HDR:, :] = a[:, _lanes(cc)]
                scan_b[cc, SCAN_HDR:, :] = b[:, _lanes(cc)]
                a_prev.append(scan_a[cc, SCAN_HDR - d:SCAN_HDR - d + T, :])
                b_prev.append(scan_b[cc, SCAN_HDR - d:SCAN_HDR - d + T, :])
            b = a * jnp.concatenate(b_prev, axis=1) + b
            a = a * jnp.concatenate(a_prev, axis=1)
        h = h_carry[s]
        for g in range(T // SUBLANES):
            lo = g * SUBLANES
            h = a[lo:lo + SUBLANES] * h + b[lo:lo + SUBLANES]
            scan_h[r0 + lo:r0 + lo + SUBLANES, :] = h
        h_carry[s] = jnp.broadcast_to(scan_h[r0 + T - 1:r0 + T, :], (SUBLANES, D_LRU))
        y_lru_parts.append(scan_h[r0:r0 + T, :] * _gelu(g_lru[r0:r0 + T]))

        ramp_up = start_pos[s] + 1 < max(POOL_WINDOWS)
        if ramp_up:
            pos1 = (lax.broadcasted_iota(jnp.int32, (T, POOL_GROUP), 0) + (start_pos[s] + 1)).astype(F32)
        means = []
        for gi, w in enumerate(POOL_WINDOWS):
            tot = u_pool[r0:r0 + T, _lanes(gi)]
            ext_pool[s, gi, :POOL_HDR, :] = pool_carry[s, gi]
            ext_pool[s, gi, POOL_HDR:, :] = tot
            pool_carry[s, gi] = ext_pool[s, gi, T:, :]
            for k in range(1, w):
                tot = tot + ext_pool[s, gi, POOL_HDR - k:POOL_HDR - k + T, :]
            means.append(tot / jnp.minimum(pos1, float(w)) if ramp_up else tot * (1.0 / w))
        m_parts.append(jnp.concatenate(means, axis=1) - u_pool[r0:r0 + T])

    y_lru = y_lru_parts[0] if S == 1 else jnp.concatenate(y_lru_parts, axis=0)
    m = (m_parts[0] if S == 1 else jnp.concatenate(m_parts, axis=0)).astype(BF16)

    y_pool = jnp.concatenate(
        [_dot(m[:, q * MXU_DIM:(q + 1) * MXU_DIM], w_pool_ref[q]) for q in range(D_POOL // MXU_DIM)],
        axis=1)
    y_pool = (y_pool + vec("pool_b")) * vec("pool_scale")

    mix_in = jnp.concatenate(
        [_rmsnorm(y_lru, vec("gn_lru")), _rmsnorm(y_pool, vec("gn_pool"))], axis=1)
    x1 = x + _dot(mix_in.astype(BF16), w_out_ref[...])

    xn = _rmsnorm(x1, vec("norm_ffn_g")).astype(BF16)
    up_pair = lambda j: [_dot(xn, w_up_ref[half * FFN_CHUNKS + j]) for half in range(2)]
    acc = jnp.zeros((S * T, D_MODEL), F32)
    ups = up_pair(0)
    for j in range(FFN_CHUNKS):
        ups_next = up_pair(j + 1) if j + 1 < FFN_CHUNKS else None
        halves = []
        for half in range(2):
            up = ups[half]
            parts = []
            for s in range(S):
                r0 = s * T
                slabs = []
                for c in range(CHUNK_SLABS):
                    gs = (half * FFN_CHUNKS + j) * CHUNK_SLABS + c
                    ext = ext_up.at[j % 2, half, s, c]
                    up_c = up[r0:r0 + T, _lanes(c)]
                    ext[:FFN_HDR, :] = ffn_carry[s, gs]
                    ext[FFN_HDR:, :] = up_c
                    ffn_carry[s, gs] = ext[T:, :]
                    upc = vec("ffn_b", gs) + up_c * vec(f"ffn_w{CONV_FFN - 1}", gs)
                    for k in range(1, CONV_FFN):
                        upc = upc + ext[FFN_HDR - k:FFN_HDR - k + T, :] * vec(f"ffn_w{CONV_FFN - 1 - k}", gs)
                    slabs.append(upc)
                parts.append(jnp.concatenate(slabs, axis=1))
            halves.append(parts[0] if S == 1 else jnp.concatenate(parts, axis=0))
        act = (_gelu(halves[0]) * halves[1]).astype(BF16)
        acc = acc + _dot(act, w_down_ref[j])
        ups = ups_next

    y = _rmsnorm(x1 + acc, vec("final_g"))
    y_ref[...] = y[y_skip:].reshape(y_ref.shape)

    @pl.when(step == pl.num_programs(0) - 1)
    def _store_state():
        for (first, count), (h_out_ref, conv_out_ref, pool_out_ref, ffn_out_ref) in zip(
                out_groups, state_out_refs):
            for p in range(count):
                s = first + p
                h_out_ref[0, p:p + 1, :] = h_carry[s, :1, :]
                for c in range(LRU_SLABS):
                    conv_out_ref[0, p, :, _lanes(c)] = lru_carry[s, c, LRU_HDR - (CONV_LRU - 1):, :]
                for c in range(POOL_SLABS):
                    pool_out_ref[0, p, :, _lanes(c)] = pool_carry[s, c, POOL_HDR - POOL_BUF:, :]
                for c in range(FFN_SLABS):
                    ffn_out_ref[0, p, :, _lanes(c)] = ffn_carry[s, c, FFN_HDR - (CONV_FFN - 1):, :]


def _scratch_shapes(n_streams, rows):
    S, T = n_streams, rows
    return [
        pltpu.VMEM((S, SUBLANES, D_LRU), F32),
        pltpu.VMEM((S, LRU_SLABS, LRU_HDR, LANES), F32),
        pltpu.VMEM((S, POOL_SLABS, POOL_HDR, LANES), F32),
        pltpu.VMEM((S, FFN_SLABS, FFN_HDR, LANES), F32),
        pltpu.VMEM((S, LRU_SLABS, LRU_HDR + T, LANES), F32),
        pltpu.VMEM((S, POOL_SLABS, POOL_HDR + T, LANES), F32),
        pltpu.VMEM((2, 2, S, CHUNK_SLABS, FFN_HDR + T, LANES), F32),
        pltpu.VMEM((LRU_SLABS, SCAN_HDR + T, LANES), F32),
        pltpu.VMEM((LRU_SLABS, SCAN_HDR + T, LANES), F32),
        pltpu.VMEM((S * T, D_LRU), F32),
    ]


def _vmem_limit(n_streams, rows, weights):
    m = n_streams * rows
    weight_bytes = sum(w.size * w.dtype.itemsize for w in weights)
    io_bytes = 2 * 2 * m * D_MODEL * 4
    scratch_bytes = 0
    for sc in _scratch_shapes(n_streams, rows):
        n = 4
        for d in sc.shape:
            n *= d
        scratch_bytes += n
    live_values = 10 * m * D_MODEL * 4
    return min(weight_bytes + io_bytes + scratch_bytes + live_values, VMEM_BYTES_V7X - (6 << 20))


def _state_shapes(count):
    return [
        jax.ShapeDtypeStruct((1, count, D_LRU), F32),
        jax.ShapeDtypeStruct((1, count, CONV_LRU - 1, D_LRU), F32),
        jax.ShapeDtypeStruct((1, count, POOL_BUF, D_POOL), F32),
        jax.ShapeDtypeStruct((1, count, CONV_FFN - 1, 2 * D_FF), F32),
    ]


def _run_streams(xs, states, weights, y_shape, *, n_streams, rows, start_pos, n_fresh, y_skip,
                 out_groups, tiled, name):
    S, T = n_streams, rows
    m = S * T
    n_steps = xs[0].shape[1] // m if tiled else 1
    const = lambda shape: pl.BlockSpec(shape, lambda i, _n=len(shape): (0,) * _n)
    whole = lambda a: pl.BlockSpec(a.shape, lambda i, _n=a.ndim: (0,) * _n,
                                   pipeline_mode=pl.Buffered(1))
    tile_spec = pl.BlockSpec((None, m, D_MODEL), lambda i: (0, i, 0))
    x_specs = [tile_spec] if tiled else [whole(a) for a in xs]
    out_shape = [jax.ShapeDtypeStruct(y_shape, F32)]
    for _, count in out_groups:
        out_shape += _state_shapes(count)
    kernel = functools.partial(
        _stream_kernel, n_streams=S, rows=T, start_pos=start_pos, n_x=len(xs), n_fresh=n_fresh,
        y_skip=y_skip, out_groups=out_groups)
    outs = pl.pallas_call(
        kernel,
        grid=(n_steps,),
        in_specs=x_specs + [whole(a) for a in states] + [whole(w) for w in weights],
        out_specs=[tile_spec if tiled else const(y_shape)] + [const(o.shape) for o in out_shape[1:]],
        out_shape=out_shape,
        scratch_shapes=_scratch_shapes(S, T),
        compiler_params=pltpu.CompilerParams(
            dimension_semantics=("arbitrary",),
            vmem_limit_bytes=_vmem_limit(S, T, weights)),
        name=name,
    )(*xs, *states, *weights)
    return outs[0], [outs[1 + 4 * g:5 + 4 * g] for g in range(len(out_groups))]


def _cast_ffn_kernel(w_up_ref, w_down_ref, up_out_ref, down_out_ref):
    up_out_ref[0] = w_up_ref[0].astype(BF16)
    down_out_ref[...] = w_down_ref[0].astype(BF16)


def _prepare_mixer_kernel(*refs):
    (w_in_ref, w_out_ref, gate_a_ref, gate_x_ref, pool_w_ref), refs = refs[:5], refs[5:]
    n_vec = len(VEC_SOURCES)
    vec_refs, (w_in_out, w_out_out, w_gate_out, w_pool_out, vec_out) = refs[:n_vec], refs[n_vec:]
    w_in_out[...] = w_in_ref[0].astype(BF16)
    w_out_out[...] = w_out_ref[0].astype(BF16)

    @pl.when(pl.program_id(0) == 0)
    def _small_weights():
        heads = MXU_DIM // LRU_HEAD
        w_gate_out[...] = jnp.zeros_like(w_gate_out)
        for q in range(D_LRU // MXU_DIM):
            for p in range(heads):
                rows_ = slice(p * LRU_HEAD, (p + 1) * LRU_HEAD)
                for half, src_ref in enumerate((gate_a_ref, gate_x_ref)):
                    c0 = half * MXU_DIM + p * LRU_HEAD
                    w_gate_out[q, rows_, c0:c0 + LRU_HEAD] = src_ref[0, q * heads + p].astype(BF16)
        groups = MXU_DIM // POOL_GROUP
        w_pool_out[...] = jnp.zeros_like(w_pool_out)
        for q in range(D_POOL // MXU_DIM):
            for p in range(groups):
                blk = slice(p * POOL_GROUP, (p + 1) * POOL_GROUP)
                w_pool_out[q, blk, blk] = pool_w_ref[0, q * groups + p].astype(BF16)
        vec_out[...] = jnp.zeros_like(vec_out)
        for (name, pick), ref in zip(VEC_SOURCES, vec_refs):
            row, c0, width = VEC_LAYOUT[name]
            vec_out[row:row + 1, c0:c0 + width] = pick(ref)


VEC_SOURCES = (
    [(f"ffn_w{k}", lambda r, k=k: r[0, k:k + 1, :]) for k in range(CONV_FFN)]
    + [(f"conv_w{k}", lambda r, k=k: r[0, k:k + 1, :]) for k in range(CONV_LRU)]
    + [(name, lambda r: r[...]) for name in (
        "ffn_b", "norm_mix_g", "norm_ffn_g", "final_g", "conv_b", "ba", "bx", "lam", "pool_b",
        "pool_scale", "gn_lru", "gn_pool")])


def kernel(x_prompt, x_sample, state_lru_h, state_lru_conv, state_pool, state_ffn_conv, meta_tokens, norm_mix_g, w_in, conv_lru_w, conv_lru_b, gate_a_w, gate_a_b, gate_x_w, gate_x_b, lru_lambda, pool_w, pool_b, pool_scale, gn_lru, gn_pool, w_out, norm_ffn_g, w_up, ffn_conv_w, ffn_conv_b, w_down, final_norm_g):
    assert x_prompt.shape[0] == 1 and norm_mix_g.shape[0] == 1
    n_sample, sample_rows = x_sample.shape[:2]
    assert sample_rows == N_META

    n_blocks = 2 * FFN_CHUNKS
    down_rows = D_FF // n_blocks
    w_up_blocks, w_down_bf = pl.pallas_call(
        _cast_ffn_kernel,
        grid=(n_blocks,),
        in_specs=[pl.BlockSpec((1, D_MODEL, FFN_CHUNK), lambda j: (0, 0, j)),
                  pl.BlockSpec((1, down_rows, D_MODEL), lambda j: (0, j, 0))],
        out_specs=[pl.BlockSpec((1, D_MODEL, FFN_CHUNK), lambda j: (j, 0, 0)),
                   pl.BlockSpec((down_rows, D_MODEL), lambda j: (j, 0))],
        out_shape=[jax.ShapeDtypeStruct((n_blocks, D_MODEL, FFN_CHUNK), BF16),
                   jax.ShapeDtypeStruct((D_FF, D_MODEL), BF16)],
        compiler_params=pltpu.CompilerParams(dimension_semantics=("arbitrary",)),
        name="cast_ffn_weights",
    )(w_up, w_down)
    w_down_blocks = w_down_bf.reshape(FFN_CHUNKS, FFN_CHUNK, D_MODEL)

    vec_params = {
        **{f"ffn_w{k}": ffn_conv_w for k in range(CONV_FFN)},
        **{f"conv_w{k}": conv_lru_w for k in range(CONV_LRU)},
        "ffn_b": ffn_conv_b, "norm_mix_g": norm_mix_g, "norm_ffn_g": norm_ffn_g,
        "final_g": final_norm_g.reshape(1, D_MODEL), "conv_b": conv_lru_b, "ba": gate_a_b, "bx": gate_x_b,
        "lam": lru_lambda, "pool_b": pool_b, "pool_scale": pool_scale, "gn_lru": gn_lru, "gn_pool": gn_pool}
    vec_args = [vec_params[name] for name, _ in VEC_SOURCES]
    mix_steps = 4
    mix_rows = D_MODEL // mix_steps
    d_in = w_in.shape[2]
    const = lambda shape: pl.BlockSpec(shape, lambda j, _n=len(shape): (0,) * _n)
    w_in_bf, w_out_bf, w_gate, w_pool, vectors = pl.pallas_call(
        _prepare_mixer_kernel,
        grid=(mix_steps,),
        in_specs=[pl.BlockSpec((1, mix_rows, d_in), lambda j: (0, j, 0)),
                  pl.BlockSpec((1, mix_rows, D_MODEL), lambda j: (0, j, 0)),
                  const(gate_a_w.shape), const(gate_x_w.shape), const(pool_w.shape)]
        + [const(a.shape) for a in vec_args],
        out_specs=[pl.BlockSpec((mix_rows, d_in), lambda j: (j, 0)),
                   pl.BlockSpec((mix_rows, D_MODEL), lambda j: (j, 0)),
                   const((D_LRU // MXU_DIM, MXU_DIM, 2 * MXU_DIM)),
                   const((D_POOL // MXU_DIM, MXU_DIM, MXU_DIM)),
                   const((VEC_ROWS, 2 * D_FF))],
        out_shape=[jax.ShapeDtypeStruct((D_MODEL, d_in), BF16),
                   jax.ShapeDtypeStruct((D_MODEL, D_MODEL), BF16),
                   jax.ShapeDtypeStruct((D_LRU // MXU_DIM, MXU_DIM, 2 * MXU_DIM), BF16),
                   jax.ShapeDtypeStruct((D_POOL // MXU_DIM, MXU_DIM, MXU_DIM), BF16),
                   jax.ShapeDtypeStruct((VEC_ROWS, 2 * D_FF), F32)],
        compiler_params=pltpu.CompilerParams(dimension_semantics=("arbitrary",)),
        name="prepare_mixer_weights",
    )(w_in, w_out, gate_a_w, gate_x_w, pool_w, *vec_args)
    weights = (vectors, w_in_bf, w_gate, w_pool, w_out_bf, w_up_blocks, w_down_blocks)

    y_s, (meta_state, sample_state) = _run_streams(
        (meta_tokens, x_sample), (state_lru_h, state_lru_conv, state_pool, state_ffn_conv), weights,
        x_sample.shape, n_streams=1 + n_sample, rows=N_META,
        start_pos=(0,) + (PAST_LEN,) * n_sample, n_fresh=1, y_skip=N_META,
        out_groups=((0, 1), (1, n_sample)), tiled=False, name="short_streams")

    y_p, (prompt_state,) = _run_streams(
        (x_prompt,), meta_state, weights, x_prompt.shape, n_streams=1, rows=PROMPT_TILE,
        start_pos=(N_META,), n_fresh=0, y_skip=0, out_groups=((0, 1),), tiled=True,
        name="prompt_stream")

    return (y_p, y_s, *prompt_state, *sample_state)
```

```python
import functools

import jax
import jax.numpy as jnp
from jax import lax
from jax.experimental import pallas as pl
from jax.experimental.pallas import tpu as pltpu

D_MODEL = 1024
N_META = 16
PAST_LEN = 1024
D_LRU = 512
N_LRU_HEADS = 8
LRU_HEAD = D_LRU // N_LRU_HEADS
LRU_C = 8.0
CONV_LRU = 4
D_POOL = 512
POOL_WINDOWS = (2, 4, 8, 16)
POOL_GROUP = D_POOL // len(POOL_WINDOWS)
POOL_BUF = max(POOL_WINDOWS) - 1
D_FF = 3 * D_MODEL
CONV_FFN = 3
EPS = 1e-6

SUBLANES = 8
LANES = 128
MXU_DIM = 256
VMEM_BYTES_V7X = 64 * 1024 * 1024

LRU_HDR = SUBLANES
POOL_PAD = SUBLANES
POOL_KEEP = 2 * SUBLANES
POOL_HDR = POOL_PAD + POOL_KEEP
FFN_HDR = SUBLANES
SCAN_HDR = SUBLANES
SCAN_SHIFTS = (1, 2, 4)
FFN_CHUNK = 512
FFN_CHUNKS = D_FF // FFN_CHUNK
LRU_SLABS = D_LRU // LANES
POOL_SLABS = D_POOL // LANES
CHUNK_SLABS = FFN_CHUNK // LANES
FFN_SLABS = 2 * D_FF // LANES
PROMPT_TILE = 512

VEC_ROWS = SUBLANES
VEC_LAYOUT = {}


def _vec_layout():
    for k in range(CONV_FFN):
        VEC_LAYOUT[f"ffn_w{k}"] = (k, 0, 2 * D_FF)
    VEC_LAYOUT["ffn_b"] = (CONV_FFN, 0, 2 * D_FF)
    col = 0
    for name, width in (("final_g", D_MODEL), ("conv_b", D_LRU), ("ba", D_LRU), ("bx", D_LRU),
                        ("lam", D_LRU), ("pool_b", D_POOL), ("pool_scale", D_POOL)):
        VEC_LAYOUT[name] = (CONV_FFN + 1, col, width)
        col += width
    assert col <= 2 * D_FF
    col = 0
    for name, width in [(f"conv_w{k}", D_LRU) for k in range(CONV_LRU)]:
        VEC_LAYOUT[name] = (CONV_FFN + 2, col, width)
        col += width
    assert col <= 2 * D_FF and CONV_FFN + 2 < VEC_ROWS


_vec_layout()


F32 = jnp.float32
BF16 = jnp.bfloat16


def _rmsnorm(x, g=None):
    y = x * lax.rsqrt(jnp.mean(x * x, axis=-1, keepdims=True) + EPS)
    return y if g is None else y * g


def _dot(a, b):
    return jnp.dot(a, b, preferred_element_type=F32)


GELU_C0 = (2.0 / jnp.pi) ** 0.5
GELU_C1 = 0.044715 * GELU_C0


def _gelu(x):
    half = 0.5 * x
    return half + half * jnp.tanh(x * (GELU_C0 + GELU_C1 * (x * x)))


def _lanes(c):
    return slice(c * LANES, (c + 1) * LANES)


def _stream_kernel(*refs, n_streams, rows, start_pos, n_x, n_fresh, y_skip, out_groups):
    S, T = n_streams, rows
    refs = list(refs)
    take = lambda n: [refs.pop(0) for _ in range(n)]
    x_refs = take(n_x)
    h0_ref, conv0_ref, pool0_ref, ffn0_ref = take(4)
    vec_ref, w_in_ref, w_gate_ref, w_pool_ref, w_out_ref, w_up_ref, w_down_ref = take(7)
    y_ref, = take(1)
    state_out_refs = [take(4) for _ in out_groups]
    (h_carry, lru_carry, pool_carry, ffn_carry, ext_lru, ext_pool, ext_up,
     scan_a, scan_b, scan_h, pool_tmp) = refs
    step = pl.program_id(0)

    def vec(name, slab=None):
        row, c0, width = VEC_LAYOUT[name]
        if slab is None:
            return vec_ref[row:row + 1, c0:c0 + width]
        return vec_ref[row:row + 1, c0 + slab * LANES:c0 + (slab + 1) * LANES]

    @pl.when(step == 0)
    def _load_state():
        lru_carry[...] = jnp.zeros_like(lru_carry)
        pool_carry[...] = jnp.zeros_like(pool_carry)
        ext_pool[:, :, :POOL_PAD, :] = jnp.zeros((S, POOL_SLABS, POOL_PAD, LANES), F32)
        pool_tmp[:, :POOL_PAD, :] = jnp.zeros((2, POOL_PAD, LANES), F32)
        ffn_carry[...] = jnp.zeros_like(ffn_carry)
        scan_a[:, :SCAN_HDR, :] = jnp.ones((LRU_SLABS, SCAN_HDR, LANES), F32)
        scan_b[:, :SCAN_HDR, :] = jnp.zeros((LRU_SLABS, SCAN_HDR, LANES), F32)
        for s in range(n_fresh):
            h_carry[s] = jnp.zeros((SUBLANES, D_LRU), F32)
        for s in range(n_fresh, S):
            p = s - n_fresh
            h_carry[s] = jnp.broadcast_to(h0_ref[0, p:p + 1, :], (SUBLANES, D_LRU))
            for c in range(LRU_SLABS):
                lru_carry[s, c, LRU_HDR - (CONV_LRU - 1):, :] = conv0_ref[0, p, :, _lanes(c)]
            for c in range(POOL_SLABS):
                pool_carry[s, c, POOL_KEEP - POOL_BUF:, :] = pool0_ref[0, p, :, _lanes(c)]
            for c in range(FFN_SLABS):
                ffn_carry[s, c, FFN_HDR - (CONV_FFN - 1):, :] = ffn0_ref[0, p, :, _lanes(c)]

    x = jnp.concatenate([r[...].reshape(-1, D_MODEL) for r in x_refs], axis=0)

    u = _dot(_rmsnorm(x).astype(BF16), w_in_ref[...])
    u_lru = u[:, :D_LRU]
    g_lru = u[:, D_LRU:2 * D_LRU]
    u_pool = u[:, 2 * D_LRU:]

    lam = vec("lam")
    c_log_sig = LRU_C * (jnp.minimum(lam, 0.0) - jnp.log1p(jnp.exp(-jnp.abs(lam))))

    y_lru_parts = []
    m_parts = []
    for s in range(S):
        r0 = s * T
        c_slabs = []
        for c in range(LRU_SLABS):
            u_c = u_lru[r0:r0 + T, _lanes(c)]
            ext_lru[s, c, :LRU_HDR, :] = lru_carry[s, c]
            ext_lru[s, c, LRU_HDR:, :] = u_c
            lru_carry[s, c] = ext_lru[s, c, T:, :]
            acc_c = vec("conv_b", c) + u_c * vec(f"conv_w{CONV_LRU - 1}", c)
            for k in range(1, CONV_LRU):
                acc_c = acc_c + (ext_lru[s, c, LRU_HDR - k:LRU_HDR - k + T, :]
                                 * vec(f"conv_w{CONV_LRU - 1 - k}", c))
            c_slabs.append(acc_c)
        c = jnp.concatenate(c_slabs, axis=1)

        cb = c.astype(BF16)
        gates = [_dot(cb[:, q * MXU_DIM:(q + 1) * MXU_DIM], w_gate_ref[q])
                 for q in range(D_LRU // MXU_DIM)]
        r = jax.nn.sigmoid(jnp.concatenate([g[:, :MXU_DIM] for g in gates], axis=1) + vec("ba"))
        i = jax.nn.sigmoid(jnp.concatenate([g[:, MXU_DIM:] for g in gates], axis=1) + vec("bx"))
        log_a = r * c_log_sig
        a = jnp.exp(log_a)
        one_minus_a2 = jnp.tanh(log_a) * (-1.0 - a * a)
        mult = jnp.where(one_minus_a2 > 0.0, one_minus_a2 * lax.rsqrt(one_minus_a2), 0.0)
        b = mult * (i * c)

        for d in SCAN_SHIFTS:
            a_prev, b_prev = [], []
            for cc in range(LRU_SLABS):
                scan_a[cc, SCAN_HDR:, :] = a[:, _lanes(cc)]
                scan_b[cc, SCAN_HDR:, :] = b[:, _lanes(cc)]
                a_prev.append(scan_a[cc, SCAN_HDR - d:SCAN_HDR - d + T, :])
                b_prev.append(scan_b[cc, SCAN_HDR - d:SCAN_HDR - d + T, :])
            b = a * jnp.concatenate(b_prev, axis=1) + b
            a = a * jnp.concatenate(a_prev, axis=1)
        h = h_carry[s]
        for g in range(T // SUBLANES):
            lo = g * SUBLANES
            h = a[lo:lo + SUBLANES] * h + b[lo:lo + SUBLANES]
            scan_h[r0 + lo:r0 + lo + SUBLANES, :] = h
        h_carry[s] = jnp.broadcast_to(scan_h[r0 + T - 1:r0 + T, :], (SUBLANES, D_LRU))
        y_lru_parts.append(scan_h[r0:r0 + T, :] * _gelu(g_lru[r0:r0 + T]))

        ramp_up = start_pos[s] + 1 < max(POOL_WINDOWS)
        if ramp_up:
            pos1 = (lax.broadcasted_iota(jnp.int32, (T, POOL_GROUP), 0) + (start_pos[s] + 1)).astype(F32)
        means = []
        for gi, w in enumerate(POOL_WINDOWS):
            ext_pool[s, gi, POOL_PAD:POOL_HDR, :] = pool_carry[s, gi]
            ext_pool[s, gi, POOL_HDR:, :] = u_pool[r0:r0 + T, _lanes(gi)]
            pool_carry[s, gi] = ext_pool[s, gi, T + POOL_PAD:, :]
            cur = ext_pool.at[s, gi]
            n_steps_ = w.bit_length() - 1
            for l in range(n_steps_):
                d = 1 << l
                lo = POOL_HDR if l == n_steps_ - 1 else POOL_PAD
                partial = cur[lo:, :] + cur[lo - d:POOL_HDR + T - d, :]
                if l < n_steps_ - 1:
                    cur = pool_tmp.at[l % 2]
                    cur[POOL_PAD:, :] = partial
            means.append(partial / jnp.minimum(pos1, float(w)) if ramp_up else partial * (1.0 / w))
        m_parts.append(jnp.concatenate(means, axis=1) - u_pool[r0:r0 + T])

    y_lru = y_lru_parts[0] if S == 1 else jnp.concatenate(y_lru_parts, axis=0)
    m = (m_parts[0] if S == 1 else jnp.concatenate(m_parts, axis=0)).astype(BF16)

    y_pool = jnp.concatenate(
        [_dot(m[:, q * MXU_DIM:(q + 1) * MXU_DIM], w_pool_ref[q]) for q in range(D_POOL // MXU_DIM)],
        axis=1)
    y_pool = (y_pool + vec("pool_b")) * vec("pool_scale")

    mix_in = jnp.concatenate(
        [_rmsnorm(y_lru), _rmsnorm(y_pool)], axis=1)
    x1 = x + _dot(mix_in.astype(BF16), w_out_ref[...])

    xn = _rmsnorm(x1).astype(BF16)
    up_pair = lambda j: [_dot(xn, w_up_ref[half * FFN_CHUNKS + j]) for half in range(2)]
    acc = jnp.zeros((S * T, D_MODEL), F32)
    ups = up_pair(0)
    for j in range(FFN_CHUNKS):
        ups_next = up_pair(j + 1) if j + 1 < FFN_CHUNKS else None
        halves = []
        for half in range(2):
            up = ups[half]
            parts = []
            for s in range(S):
                r0 = s * T
                slabs = []
                for c in range(CHUNK_SLABS):
                    gs = (half * FFN_CHUNKS + j) * CHUNK_SLABS + c
                    ext = ext_up.at[j % 2, half, s, c]
                    up_c = up[r0:r0 + T, _lanes(c)]
                    ext[:FFN_HDR, :] = ffn_carry[s, gs]
                    ext[FFN_HDR:, :] = up_c
                    ffn_carry[s, gs] = ext[T:, :]
                    upc = vec("ffn_b", gs) + up_c * vec(f"ffn_w{CONV_FFN - 1}", gs)
                    for k in range(1, CONV_FFN):
                        upc = upc + ext[FFN_HDR - k:FFN_HDR - k + T, :] * vec(f"ffn_w{CONV_FFN - 1 - k}", gs)
                    slabs.append(upc)
                parts.append(jnp.concatenate(slabs, axis=1))
            halves.append(parts[0] if S == 1 else jnp.concatenate(parts, axis=0))
        act = (_gelu(halves[0]) * halves[1]).astype(BF16)
        acc = acc + _dot(act, w_down_ref[j])
        ups = ups_next

    y = _rmsnorm(x1 + acc, vec("final_g"))
    y_ref[...] = y[y_skip:].reshape(y_ref.shape)

    @pl.when(step == pl.num_programs(0) - 1)
    def _store_state():
        for (first, count), (h_out_ref, conv_out_ref, pool_out_ref, ffn_out_ref) in zip(
                out_groups, state_out_refs):
            for p in range(count):
                s = first + p
                h_out_ref[0, p:p + 1, :] = h_carry[s, :1, :]
                for c in range(LRU_SLABS):
                    conv_out_ref[0, p, :, _lanes(c)] = lru_carry[s, c, LRU_HDR - (CONV_LRU - 1):, :]
                for c in range(POOL_SLABS):
                    pool_out_ref[0, p, :, _lanes(c)] = pool_carry[s, c, POOL_KEEP - POOL_BUF:, :]
                for c in range(FFN_SLABS):
                    ffn_out_ref[0, p, :, _lanes(c)] = ffn_carry[s, c, FFN_HDR - (CONV_FFN - 1):, :]


def _scratch_shapes(n_streams, rows):
    S, T = n_streams, rows
    return [
        pltpu.VMEM((S, SUBLANES, D_LRU), F32),
        pltpu.VMEM((S, LRU_SLABS, LRU_HDR, LANES), F32),
        pltpu.VMEM((S, POOL_SLABS, POOL_KEEP, LANES), F32),
        pltpu.VMEM((S, FFN_SLABS, FFN_HDR, LANES), F32),
        pltpu.VMEM((S, LRU_SLABS, LRU_HDR + T, LANES), F32),
        pltpu.VMEM((S, POOL_SLABS, POOL_HDR + T, LANES), F32),
        pltpu.VMEM((2, 2, S, CHUNK_SLABS, FFN_HDR + T, LANES), F32),
        pltpu.VMEM((LRU_SLABS, SCAN_HDR + T, LANES), F32),
        pltpu.VMEM((LRU_SLABS, SCAN_HDR + T, LANES), F32),
        pltpu.VMEM((S * T, D_LRU), F32),
        pltpu.VMEM((2, POOL_HDR + T, LANES), F32),
    ]


def _vmem_limit(n_streams, rows, weights):
    m = n_streams * rows
    weight_bytes = sum(w.size * w.dtype.itemsize for w in weights)
    io_bytes = 2 * 2 * m * D_MODEL * 4
    scratch_bytes = 0
    for sc in _scratch_shapes(n_streams, rows):
        n = 4
        for d in sc.shape:
            n *= d
        scratch_bytes += n
    live_values = 10 * m * D_MODEL * 4
    return min(weight_bytes + io_bytes + scratch_bytes + live_values, VMEM_BYTES_V7X - (6 << 20))


def _state_shapes(count):
    return [
        jax.ShapeDtypeStruct((1, count, D_LRU), F32),
        jax.ShapeDtypeStruct((1, count, CONV_LRU - 1, D_LRU), F32),
        jax.ShapeDtypeStruct((1, count, POOL_BUF, D_POOL), F32),
        jax.ShapeDtypeStruct((1, count, CONV_FFN - 1, 2 * D_FF), F32),
    ]


def _run_streams(xs, states, weights, y_shape, *, n_streams, rows, start_pos, n_fresh, y_skip,
                 out_groups, tiled, name):
    S, T = n_streams, rows
    m = S * T
    n_steps = xs[0].shape[1] // m if tiled else 1
    const = lambda shape: pl.BlockSpec(shape, lambda i, _n=len(shape): (0,) * _n)
    whole = lambda a: pl.BlockSpec(a.shape, lambda i, _n=a.ndim: (0,) * _n,
                                   pipeline_mode=pl.Buffered(1))
    tile_spec = pl.BlockSpec((None, m, D_MODEL), lambda i: (0, i, 0))
    x_specs = [tile_spec] if tiled else [whole(a) for a in xs]
    out_shape = [jax.ShapeDtypeStruct(y_shape, F32)]
    for _, count in out_groups:
        out_shape += _state_shapes(count)
    kernel = functools.partial(
        _stream_kernel, n_streams=S, rows=T, start_pos=start_pos, n_x=len(xs), n_fresh=n_fresh,
        y_skip=y_skip, out_groups=out_groups)
    outs = pl.pallas_call(
        kernel,
        grid=(n_steps,),
        in_specs=x_specs + [whole(a) for a in states] + [whole(w) for w in weights],
        out_specs=[tile_spec if tiled else const(y_shape)] + [const(o.shape) for o in out_shape[1:]],
        out_shape=out_shape,
        scratch_shapes=_scratch_shapes(S, T),
        compiler_params=pltpu.CompilerParams(
            dimension_semantics=("arbitrary",),
            vmem_limit_bytes=_vmem_limit(S, T, weights)),
        name=name,
    )(*xs, *states, *weights)
    return outs[0], [outs[1 + 4 * g:5 + 4 * g] for g in range(len(out_groups))]


def _cast_ffn_kernel(w_up_ref, w_down_ref, gain_ref, up_out_ref, down_out_ref):
    up_out_ref[0] = (w_up_ref[0] * gain_ref[...]).astype(BF16)
    down_out_ref[...] = w_down_ref[0].astype(BF16)


def _prepare_mixer_kernel(*refs):
    (w_in_ref, w_out_ref, in_gain_ref, out_gain_ref, gate_a_ref, gate_x_ref, pool_w_ref), refs = (
        refs[:7], refs[7:])
    n_vec = len(VEC_SOURCES)
    vec_refs, (w_in_out, w_out_out, w_gate_out, w_pool_out, vec_out) = refs[:n_vec], refs[n_vec:]
    w_in_out[...] = (w_in_ref[0] * in_gain_ref[...]).astype(BF16)
    w_out_out[...] = (w_out_ref[0] * out_gain_ref[...]).astype(BF16)

    @pl.when(pl.program_id(0) == 0)
    def _small_weights():
        heads = MXU_DIM // LRU_HEAD
        w_gate_out[...] = jnp.zeros_like(w_gate_out)
        for q in range(D_LRU // MXU_DIM):
            for p in range(heads):
                rows_ = slice(p * LRU_HEAD, (p + 1) * LRU_HEAD)
                for half, src_ref in enumerate((gate_a_ref, gate_x_ref)):
                    c0 = half * MXU_DIM + p * LRU_HEAD
                    w_gate_out[q, rows_, c0:c0 + LRU_HEAD] = src_ref[0, q * heads + p].astype(BF16)
        groups = MXU_DIM // POOL_GROUP
        w_pool_out[...] = jnp.zeros_like(w_pool_out)
        for q in range(D_POOL // MXU_DIM):
            for p in range(groups):
                blk = slice(p * POOL_GROUP, (p + 1) * POOL_GROUP)
                w_pool_out[q, blk, blk] = pool_w_ref[0, q * groups + p].astype(BF16)
        vec_out[...] = jnp.zeros_like(vec_out)
        for (name, pick), ref in zip(VEC_SOURCES, vec_refs):
            row, c0, width = VEC_LAYOUT[name]
            vec_out[row:row + 1, c0:c0 + width] = pick(ref)


VEC_SOURCES = (
    [(f"ffn_w{k}", lambda r, k=k: r[0, k:k + 1, :]) for k in range(CONV_FFN)]
    + [(f"conv_w{k}", lambda r, k=k: r[0, k:k + 1, :]) for k in range(CONV_LRU)]
    + [(name, lambda r: r[...]) for name in (
        "ffn_b", "final_g", "conv_b", "ba", "bx", "lam", "pool_b", "pool_scale")])


def kernel(x_prompt, x_sample, state_lru_h, state_lru_conv, state_pool, state_ffn_conv, meta_tokens, norm_mix_g, w_in, conv_lru_w, conv_lru_b, gate_a_w, gate_a_b, gate_x_w, gate_x_b, lru_lambda, pool_w, pool_b, pool_scale, gn_lru, gn_pool, w_out, norm_ffn_g, w_up, ffn_conv_w, ffn_conv_b, w_down, final_norm_g):
    assert x_prompt.shape[0] == 1 and norm_mix_g.shape[0] == 1
    n_sample, sample_rows = x_sample.shape[:2]
    assert sample_rows == N_META

    n_blocks = 2 * FFN_CHUNKS
    down_rows = D_FF // n_blocks
    w_up_blocks, w_down_bf = pl.pallas_call(
        _cast_ffn_kernel,
        grid=(n_blocks,),
        in_specs=[pl.BlockSpec((1, D_MODEL, FFN_CHUNK), lambda j: (0, 0, j)),
                  pl.BlockSpec((1, down_rows, D_MODEL), lambda j: (0, j, 0)),
                  pl.BlockSpec((D_MODEL, 1), lambda j: (0, 0))],
        out_specs=[pl.BlockSpec((1, D_MODEL, FFN_CHUNK), lambda j: (j, 0, 0)),
                   pl.BlockSpec((down_rows, D_MODEL), lambda j: (j, 0))],
        out_shape=[jax.ShapeDtypeStruct((n_blocks, D_MODEL, FFN_CHUNK), BF16),
                   jax.ShapeDtypeStruct((D_FF, D_MODEL), BF16)],
        compiler_params=pltpu.CompilerParams(dimension_semantics=("arbitrary",)),
        name="cast_ffn_weights",
    )(w_up, w_down, norm_ffn_g.reshape(D_MODEL, 1))
    w_down_blocks = w_down_bf.reshape(FFN_CHUNKS, FFN_CHUNK, D_MODEL)

    vec_params = {
        **{f"ffn_w{k}": ffn_conv_w for k in range(CONV_FFN)},
        **{f"conv_w{k}": conv_lru_w for k in range(CONV_LRU)},
        "ffn_b": ffn_conv_b, "final_g": final_norm_g.reshape(1, D_MODEL), "conv_b": conv_lru_b,
        "ba": gate_a_b, "bx": gate_x_b, "lam": lru_lambda, "pool_b": pool_b, "pool_scale": pool_scale}
    vec_args = [vec_params[name] for name, _ in VEC_SOURCES]
    mix_steps = 4
    mix_rows = D_MODEL // mix_steps
    d_in = w_in.shape[2]
    const = lambda shape: pl.BlockSpec(shape, lambda j, _n=len(shape): (0,) * _n)
    w_in_bf, w_out_bf, w_gate, w_pool, vectors = pl.pallas_call(
        _prepare_mixer_kernel,
        grid=(mix_steps,),
        in_specs=[pl.BlockSpec((1, mix_rows, d_in), lambda j: (0, j, 0)),
                  pl.BlockSpec((1, mix_rows, D_MODEL), lambda j: (0, j, 0)),
                  pl.BlockSpec((mix_rows, 1), lambda j: (j, 0)),
                  pl.BlockSpec((mix_rows, 1), lambda j: (j, 0)),
                  const(gate_a_w.shape), const(gate_x_w.shape), const(pool_w.shape)]
        + [const(a.shape) for a in vec_args],
        out_specs=[pl.BlockSpec((mix_rows, d_in), lambda j: (j, 0)),
                   pl.BlockSpec((mix_rows, D_MODEL), lambda j: (j, 0)),
                   const((D_LRU // MXU_DIM, MXU_DIM, 2 * MXU_DIM)),
                   const((D_POOL // MXU_DIM, MXU_DIM, MXU_DIM)),
                   const((VEC_ROWS, 2 * D_FF))],
        out_shape=[jax.ShapeDtypeStruct((D_MODEL, d_in), BF16),
                   jax.ShapeDtypeStruct((D_MODEL, D_MODEL), BF16),
                   jax.ShapeDtypeStruct((D_LRU // MXU_DIM, MXU_DIM, 2 * MXU_DIM), BF16),
                   jax.ShapeDtypeStruct((D_POOL // MXU_DIM, MXU_DIM, MXU_DIM), BF16),
                   jax.ShapeDtypeStruct((VEC_ROWS, 2 * D_FF), F32)],
        compiler_params=pltpu.CompilerParams(dimension_semantics=("arbitrary",)),
        name="prepare_mixer_weights",
    )(w_in, w_out, norm_mix_g.reshape(D_MODEL, 1),
      jnp.concatenate([gn_lru, gn_pool], axis=1).reshape(D_MODEL, 1), gate_a_w, gate_x_w, pool_w, *vec_args)
    weights = (vectors, w_in_bf, w_gate, w_pool, w_out_bf, w_up_blocks, w_down_blocks)

    y_s, (meta_state, sample_state) = _run_streams(
        (meta_tokens, x_sample), (state_lru_h, state_lru_conv, state_pool, state_ffn_conv), weights,
        x_sample.shape, n_streams=1 + n_sample, rows=N_META,
        start_pos=(0,) + (PAST_LEN,) * n_sample, n_fresh=1, y_skip=N_META,
        out_groups=((0, 1), (1, n_sample)), tiled=False, name="short_streams")

    y_p, (prompt_state,) = _run_streams(
        (x_prompt,), meta_state, weights, x_prompt.shape, n_streams=1, rows=PROMPT_TILE,
        start_pos=(N_META,), n_fresh=0, y_skip=0, out_groups=((0, 1),), tiled=True,
        name="prompt_stream")

    return (y_p, y_s, *prompt_state, *sample_state)
```

```python
import functools

import jax
import jax.numpy as jnp
from jax import lax
from jax.experimental import pallas as pl
from jax.experimental.pallas import tpu as pltpu

D_MODEL = 1024
N_META = 16
PAST_LEN = 1024
D_LRU = 512
N_LRU_HEADS = 8
LRU_HEAD = D_LRU // N_LRU_HEADS
LRU_C = 8.0
CONV_LRU = 4
D_POOL = 512
POOL_WINDOWS = (2, 4, 8, 16)
POOL_GROUP = D_POOL // len(POOL_WINDOWS)
POOL_BUF = max(POOL_WINDOWS) - 1
D_FF = 3 * D_MODEL
CONV_FFN = 3
EPS = 1e-6

SUBLANES = 8
LANES = 128
MXU_DIM = 256
VMEM_BYTES_V7X = 64 * 1024 * 1024

LRU_HDR = SUBLANES
POOL_PAD = SUBLANES
POOL_KEEP = 2 * SUBLANES
POOL_HDR = POOL_PAD + POOL_KEEP
FFN_HDR = SUBLANES
SCAN_HDR = SUBLANES
SCAN_SHIFTS = (1, 2, 4)
FFN_CHUNK = 512
FFN_CHUNKS = D_FF // FFN_CHUNK
LRU_SLABS = D_LRU // LANES
POOL_SLABS = D_POOL // LANES
CHUNK_SLABS = FFN_CHUNK // LANES
FFN_SLABS = 2 * D_FF // LANES
PROMPT_TILE = 512

VEC_ROWS = SUBLANES
VEC_LAYOUT = {}


def _vec_layout():
    for k in range(CONV_FFN):
        VEC_LAYOUT[f"ffn_w{k}"] = (k, 0, 2 * D_FF)
    VEC_LAYOUT["ffn_b"] = (CONV_FFN, 0, 2 * D_FF)
    col = 0
    for name, width in (("final_g", D_MODEL), ("conv_b", D_LRU), ("ba", D_LRU), ("bx", D_LRU),
                        ("lam", D_LRU), ("pool_b", D_POOL), ("pool_scale", D_POOL)):
        VEC_LAYOUT[name] = (CONV_FFN + 1, col, width)
        col += width
    assert col <= 2 * D_FF
    col = 0
    for name, width in [(f"conv_w{k}", D_LRU) for k in range(CONV_LRU)]:
        VEC_LAYOUT[name] = (CONV_FFN + 2, col, width)
        col += width
    assert col <= 2 * D_FF and CONV_FFN + 2 < VEC_ROWS


_vec_layout()


F32 = jnp.float32
BF16 = jnp.bfloat16


def _rmsnorm(x, g=None):
    y = x * lax.rsqrt(jnp.mean(x * x, axis=-1, keepdims=True) + EPS)
    return y if g is None else y * g


def _dot(a, b):
    return jnp.dot(a, b, preferred_element_type=F32)


GELU_C0 = (2.0 / jnp.pi) ** 0.5
GELU_C1 = 0.044715 * GELU_C0


def _gelu(x):
    half = 0.5 * x
    return half + half * jnp.tanh(x * (GELU_C0 + GELU_C1 * (x * x)))


def _lanes(c):
    return slice(c * LANES, (c + 1) * LANES)


def _stream_kernel(*refs, n_streams, rows, start_pos, n_x, n_fresh, y_skip, out_groups):
    S, T = n_streams, rows
    refs = list(refs)
    take = lambda n: [refs.pop(0) for _ in range(n)]
    x_refs = take(n_x)
    h0_ref, conv0_ref, pool0_ref, ffn0_ref = take(4)
    vec_ref, w_in_ref, w_gate_ref, w_pool_ref, w_out_ref, w_up_ref, w_down_ref = take(7)
    y_ref, = take(1)
    state_out_refs = [take(4) for _ in out_groups]
    (h_carry, lru_carry, pool_carry, ffn_carry, ext_lru, ext_pool, ext_up,
     scan_a, scan_b, scan_h, pool_tmp) = refs
    step = pl.program_id(0)

    def vec(name, slab=None):
        row, c0, width = VEC_LAYOUT[name]
        if slab is None:
            return vec_ref[row:row + 1, c0:c0 + width]
        return vec_ref[row:row + 1, c0 + slab * LANES:c0 + (slab + 1) * LANES]

    @pl.when(step == 0)
    def _load_state():
        lru_carry[...] = jnp.zeros_like(lru_carry)
        pool_carry[...] = jnp.zeros_like(pool_carry)
        ext_pool[:, :, :POOL_PAD, :] = jnp.zeros((S, POOL_SLABS, POOL_PAD, LANES), F32)
        pool_tmp[:, :POOL_PAD, :] = jnp.zeros((2, POOL_PAD, LANES), F32)
        ffn_carry[...] = jnp.zeros_like(ffn_carry)
        scan_a[:, :SCAN_HDR, :] = jnp.ones((LRU_SLABS, SCAN_HDR, LANES), F32)
        scan_b[:, :SCAN_HDR, :] = jnp.zeros((LRU_SLABS, SCAN_HDR, LANES), F32)
        for s in range(n_fresh):
            h_carry[s] = jnp.zeros((SUBLANES, D_LRU), F32)
        for s in range(n_fresh, S):
            p = s - n_fresh
            h_carry[s] = jnp.broadcast_to(h0_ref[0, p:p + 1, :], (SUBLANES, D_LRU))
            for c in range(LRU_SLABS):
                lru_carry[s, c, LRU_HDR - (CONV_LRU - 1):, :] = conv0_ref[0, p, :, _lanes(c)]
            for c in range(POOL_SLABS):
                pool_carry[s, c, POOL_KEEP - POOL_BUF:, :] = pool0_ref[0, p, :, _lanes(c)]
            for c in range(FFN_SLABS):
                ffn_carry[s, c, FFN_HDR - (CONV_FFN - 1):, :] = ffn0_ref[0, p, :, _lanes(c)]

    x = jnp.concatenate([r[...].reshape(-1, D_MODEL) for r in x_refs], axis=0)

    u = _dot(_rmsnorm(x).astype(BF16), w_in_ref[...])
    u_lru = u[:, :D_LRU]
    g_lru = u[:, D_LRU:2 * D_LRU]
    u_pool = u[:, 2 * D_LRU:]

    lam = vec("lam")
    c_log_sig = LRU_C * (jnp.minimum(lam, 0.0) - jnp.log1p(jnp.exp(-jnp.abs(lam))))

    y_lru_parts = []
    m_parts = []
    for s in range(S):
        r0 = s * T
        c_slabs = []
        for c in range(LRU_SLABS):
            u_c = u_lru[r0:r0 + T, _lanes(c)]
            ext_lru[s, c, :LRU_HDR, :] = lru_carry[s, c]
            ext_lru[s, c, LRU_HDR:, :] = u_c
            lru_carry[s, c] = ext_lru[s, c, T:, :]
            acc_c = vec("conv_b", c) + u_c * vec(f"conv_w{CONV_LRU - 1}", c)
            for k in range(1, CONV_LRU):
                acc_c = acc_c + (ext_lru[s, c, LRU_HDR - k:LRU_HDR - k + T, :]
                                 * vec(f"conv_w{CONV_LRU - 1 - k}", c))
            c_slabs.append(acc_c)
        c = jnp.concatenate(c_slabs, axis=1)

        cb = c.astype(BF16)
        gates = [_dot(cb[:, q * MXU_DIM:(q + 1) * MXU_DIM], w_gate_ref[q])
                 for q in range(D_LRU // MXU_DIM)]
        r = jax.nn.sigmoid(jnp.concatenate([g[:, :MXU_DIM] for g in gates], axis=1) + vec("ba"))
        i = jax.nn.sigmoid(jnp.concatenate([g[:, MXU_DIM:] for g in gates], axis=1) + vec("bx"))
        log_a = r * c_log_sig
        a = jnp.exp(log_a)
        one_minus_a2 = jnp.tanh(log_a) * (-1.0 - a * a)
        mult = jnp.where(one_minus_a2 > 0.0, one_minus_a2 * lax.rsqrt(one_minus_a2), 0.0)
        b = mult * (i * c)

        for d in SCAN_SHIFTS:
            a_prev, b_prev = [], []
            for cc in range(LRU_SLABS):
                scan_a[cc, SCAN_HDR:, :] = a[:, _lanes(cc)]
                scan_b[cc, SCAN_HDR:, :] = b[:, _lanes(cc)]
                a_prev.append(scan_a[cc, SCAN_HDR - d:SCAN_HDR - d + T, :])
                b_prev.append(scan_b[cc, SCAN_HDR - d:SCAN_HDR - d + T, :])
            b = a * jnp.concatenate(b_prev, axis=1) + b
            a = a * jnp.concatenate(a_prev, axis=1)
        h = h_carry[s]
        for g in range(T // SUBLANES):
            lo = g * SUBLANES
            h = a[lo:lo + SUBLANES] * h + b[lo:lo + SUBLANES]
            scan_h[r0 + lo:r0 + lo + SUBLANES, :] = h
        h_carry[s] = jnp.broadcast_to(scan_h[r0 + T - 1:r0 + T, :], (SUBLANES, D_LRU))
        y_lru_parts.append(scan_h[r0:r0 + T, :] * _gelu(g_lru[r0:r0 + T]))

        ramp_up = start_pos[s] + 1 < max(POOL_WINDOWS)
        if ramp_up:
            pos1 = (lax.broadcasted_iota(jnp.int32, (T, POOL_GROUP), 0) + (start_pos[s] + 1)).astype(F32)
        means = []
        for gi, w in enumerate(POOL_WINDOWS):
            ext_pool[s, gi, POOL_PAD:POOL_HDR, :] = pool_carry[s, gi]
            ext_pool[s, gi, POOL_HDR:, :] = u_pool[r0:r0 + T, _lanes(gi)]
            pool_carry[s, gi] = ext_pool[s, gi, T + POOL_PAD:, :]
            cur = ext_pool.at[s, gi]
            n_steps_ = w.bit_length() - 1
            for l in range(n_steps_):
                d = 1 << l
                lo = POOL_HDR if l == n_steps_ - 1 else POOL_PAD
                partial = cur[lo:, :] + cur[lo - d:POOL_HDR + T - d, :]
                if l < n_steps_ - 1:
                    cur = pool_tmp.at[l % 2]
                    cur[POOL_PAD:, :] = partial
            means.append(partial / jnp.minimum(pos1, float(w)) if ramp_up else partial * (1.0 / w))
        m_parts.append(jnp.concatenate(means, axis=1) - u_pool[r0:r0 + T])

    y_lru = y_lru_parts[0] if S == 1 else jnp.concatenate(y_lru_parts, axis=0)
    m = (m_parts[0] if S == 1 else jnp.concatenate(m_parts, axis=0)).astype(BF16)

    y_pool = jnp.concatenate(
        [_dot(m[:, q * MXU_DIM:(q + 1) * MXU_DIM], w_pool_ref[q]) for q in range(D_POOL // MXU_DIM)],
        axis=1)
    y_pool = (y_pool + vec("pool_b")) * vec("pool_scale")

    mix_in = jnp.concatenate(
        [_rmsnorm(y_lru), _rmsnorm(y_pool)], axis=1)
    x1 = x + _dot(mix_in.astype(BF16), w_out_ref[...])

    xn = _rmsnorm(x1).astype(BF16)
    up_pair = lambda j: [_dot(xn, w_up_ref[half * FFN_CHUNKS + j]) for half in range(2)]
    acc = jnp.zeros((S * T, D_MODEL), F32)
    ups = up_pair(0)
    for j in range(FFN_CHUNKS):
        ups_next = up_pair(j + 1) if j + 1 < FFN_CHUNKS else None
        halves = []
        for half in range(2):
            up = ups[half]
            parts = []
            for s in range(S):
                r0 = s * T
                slabs = []
                for c in range(CHUNK_SLABS):
                    gs = (half * FFN_CHUNKS + j) * CHUNK_SLABS + c
                    ext = ext_up.at[j % 2, half, s, c]
                    up_c = up[r0:r0 + T, _lanes(c)]
                    ext[:FFN_HDR, :] = ffn_carry[s, gs]
                    ext[FFN_HDR:, :] = up_c
                    ffn_carry[s, gs] = ext[T:, :]
                    upc = vec("ffn_b", gs) + up_c * vec(f"ffn_w{CONV_FFN - 1}", gs)
                    for k in range(1, CONV_FFN):
                        upc = upc + ext[FFN_HDR - k:FFN_HDR - k + T, :] * vec(f"ffn_w{CONV_FFN - 1 - k}", gs)
                    slabs.append(upc)
                parts.append(jnp.concatenate(slabs, axis=1))
            halves.append(parts[0] if S == 1 else jnp.concatenate(parts, axis=0))
        act = (_gelu(halves[0]) * halves[1]).astype(BF16)
        acc = acc + _dot(act, w_down_ref[j])
        ups = ups_next

    y = _rmsnorm(x1 + acc, vec("final_g"))
    y_ref[...] = y[y_skip:].reshape(y_ref.shape)

    @pl.when(step == pl.num_programs(0) - 1)
    def _store_state():
        for (first, count), (h_out_ref, conv_out_ref, pool_out_ref, ffn_out_ref) in zip(
                out_groups, state_out_refs):
            for p in range(count):
                s = first + p
                h_out_ref[0, p:p + 1, :] = h_carry[s, :1, :]
                for c in range(LRU_SLABS):
                    conv_out_ref[0, p, :, _lanes(c)] = lru_carry[s, c, LRU_HDR - (CONV_LRU - 1):, :]
                for c in range(POOL_SLABS):
                    pool_out_ref[0, p, :, _lanes(c)] = pool_carry[s, c, POOL_KEEP - POOL_BUF:, :]
                for c in range(FFN_SLABS):
                    ffn_out_ref[0, p, :, _lanes(c)] = ffn_carry[s, c, FFN_HDR - (CONV_FFN - 1):, :]


def _scratch_shapes(n_streams, rows):
    S, T = n_streams, rows
    return [
        pltpu.VMEM((S, SUBLANES, D_LRU), F32),
        pltpu.VMEM((S, LRU_SLABS, LRU_HDR, LANES), F32),
        pltpu.VMEM((S, POOL_SLABS, POOL_KEEP, LANES), F32),
        pltpu.VMEM((S, FFN_SLABS, FFN_HDR, LANES), F32),
        pltpu.VMEM((S, LRU_SLABS, LRU_HDR + T, LANES), F32),
        pltpu.VMEM((S, POOL_SLABS, POOL_HDR + T, LANES), F32),
        pltpu.VMEM((2, 2, S, CHUNK_SLABS, FFN_HDR + T, LANES), F32),
        pltpu.VMEM((LRU_SLABS, SCAN_HDR + T, LANES), F32),
        pltpu.VMEM((LRU_SLABS, SCAN_HDR + T, LANES), F32),
        pltpu.VMEM((S * T, D_LRU), F32),
        pltpu.VMEM((2, POOL_HDR + T, LANES), F32),
    ]


def _vmem_limit(n_streams, rows, weights):
    m = n_streams * rows
    weight_bytes = sum(w.size * w.dtype.itemsize for w in weights)
    io_bytes = 2 * 2 * m * D_MODEL * 4
    scratch_bytes = 0
    for sc in _scratch_shapes(n_streams, rows):
        n = 4
        for d in sc.shape:
            n *= d
        scratch_bytes += n
    live_values = 10 * m * D_MODEL * 4
    return min(weight_bytes + io_bytes + scratch_bytes + live_values, VMEM_BYTES_V7X - (6 << 20))


def _state_shapes(count):
    return [
        jax.ShapeDtypeStruct((1, count, D_LRU), F32),
        jax.ShapeDtypeStruct((1, count, CONV_LRU - 1, D_LRU), F32),
        jax.ShapeDtypeStruct((1, count, POOL_BUF, D_POOL), F32),
        jax.ShapeDtypeStruct((1, count, CONV_FFN - 1, 2 * D_FF), F32),
    ]


def _run_streams(xs, states, weights, y_shape, *, n_streams, rows, start_pos, n_fresh, y_skip,
                 out_groups, tiled, name):
    S, T = n_streams, rows
    m = S * T
    n_steps = xs[0].shape[1] // m if tiled else 1
    const = lambda shape: pl.BlockSpec(shape, lambda i, _n=len(shape): (0,) * _n)
    whole = lambda a: pl.BlockSpec(a.shape, lambda i, _n=a.ndim: (0,) * _n,
                                   pipeline_mode=pl.Buffered(1))
    tile_spec = pl.BlockSpec((None, m, D_MODEL), lambda i: (0, i, 0))
    x_specs = [tile_spec] if tiled else [whole(a) for a in xs]
    out_shape = [jax.ShapeDtypeStruct(y_shape, F32)]
    for _, count in out_groups:
        out_shape += _state_shapes(count)
    kernel = functools.partial(
        _stream_kernel, n_streams=S, rows=T, start_pos=start_pos, n_x=len(xs), n_fresh=n_fresh,
        y_skip=y_skip, out_groups=out_groups)
    outs = pl.pallas_call(
        kernel,
        grid=(n_steps,),
        in_specs=x_specs + [whole(a) for a in states] + [whole(w) for w in weights],
        out_specs=[tile_spec if tiled else const(y_shape)] + [const(o.shape) for o in out_shape[1:]],
        out_shape=out_shape,
        scratch_shapes=_scratch_shapes(S, T),
        compiler_params=pltpu.CompilerParams(
            dimension_semantics=("arbitrary",),
            vmem_limit_bytes=_vmem_limit(S, T, weights)),
        name=name,
    )(*xs, *states, *weights)
    return outs[0], [outs[1 + 4 * g:5 + 4 * g] for g in range(len(out_groups))]


def _row_gain(gain_row):
    return jnp.broadcast_to(gain_row, (LANES, gain_row.shape[1])).T


def _scale_rows(w, gain_row):
    g = _row_gain(gain_row)
    return jnp.concatenate([w[:, _lanes(c)] * g for c in range(w.shape[1] // LANES)], axis=1)


def _cast_ffn_kernel(w_up_ref, w_down_ref, gain_ref, up_out_ref, down_out_ref):
    up_out_ref[0] = _scale_rows(w_up_ref[0], gain_ref[...]).astype(BF16)
    down_out_ref[...] = w_down_ref[0].astype(BF16)


def _prepare_mixer_kernel(*refs):
    (w_in_ref, w_out_ref, in_gain_ref, lru_gain_ref, pool_gain_ref, gate_a_ref, gate_x_ref,
     pool_w_ref), refs = refs[:8], refs[8:]
    n_vec = len(VEC_SOURCES)
    vec_refs, (w_in_out, w_out_out, w_gate_out, w_pool_out, vec_out) = refs[:n_vec], refs[n_vec:]
    w_in_out[...] = _scale_rows(w_in_ref[0], in_gain_ref[...]).astype(BF16)
    lru_rows = pl.program_id(0) < D_LRU // w_out_out.shape[0]
    out_gain = jnp.where(lru_rows, lru_gain_ref[...], pool_gain_ref[...])
    w_out_out[...] = _scale_rows(w_out_ref[0], out_gain).astype(BF16)

    @pl.when(pl.program_id(0) == 0)
    def _small_weights():
        heads = MXU_DIM // LRU_HEAD
        w_gate_out[...] = jnp.zeros_like(w_gate_out)
        for q in range(D_LRU // MXU_DIM):
            for p in range(heads):
                rows_ = slice(p * LRU_HEAD, (p + 1) * LRU_HEAD)
                for half, src_ref in enumerate((gate_a_ref, gate_x_ref)):
                    c0 = half * MXU_DIM + p * LRU_HEAD
                    w_gate_out[q, rows_, c0:c0 + LRU_HEAD] = src_ref[0, q * heads + p].astype(BF16)
        groups = MXU_DIM // POOL_GROUP
        w_pool_out[...] = jnp.zeros_like(w_pool_out)
        for q in range(D_POOL // MXU_DIM):
            for p in range(groups):
                blk = slice(p * POOL_GROUP, (p + 1) * POOL_GROUP)
                w_pool_out[q, blk, blk] = pool_w_ref[0, q * groups + p].astype(BF16)
        vec_out[...] = jnp.zeros_like(vec_out)
        for (name, pick), ref in zip(VEC_SOURCES, vec_refs):
            row, c0, width = VEC_LAYOUT[name]
            vec_out[row:row + 1, c0:c0 + width] = pick(ref)


VEC_SOURCES = (
    [(f"ffn_w{k}", lambda r, k=k: r[0, k:k + 1, :]) for k in range(CONV_FFN)]
    + [(f"conv_w{k}", lambda r, k=k: r[0, k:k + 1, :]) for k in range(CONV_LRU)]
    + [(name, lambda r: r[...]) for name in (
        "ffn_b", "final_g", "conv_b", "ba", "bx", "lam", "pool_b", "pool_scale")])


def kernel(x_prompt, x_sample, state_lru_h, state_lru_conv, state_pool, state_ffn_conv, meta_tokens, norm_mix_g, w_in, conv_lru_w, conv_lru_b, gate_a_w, gate_a_b, gate_x_w, gate_x_b, lru_lambda, pool_w, pool_b, pool_scale, gn_lru, gn_pool, w_out, norm_ffn_g, w_up, ffn_conv_w, ffn_conv_b, w_down, final_norm_g):
    assert x_prompt.shape[0] == 1 and norm_mix_g.shape[0] == 1
    n_sample, sample_rows = x_sample.shape[:2]
    assert sample_rows == N_META

    n_blocks = 2 * FFN_CHUNKS
    down_rows = D_FF // n_blocks
    w_up_blocks, w_down_bf = pl.pallas_call(
        _cast_ffn_kernel,
        grid=(n_blocks,),
        in_specs=[pl.BlockSpec((1, D_MODEL, FFN_CHUNK), lambda j: (0, 0, j)),
                  pl.BlockSpec((1, down_rows, D_MODEL), lambda j: (0, j, 0)),
                  pl.BlockSpec((1, D_MODEL), lambda j: (0, 0))],
        out_specs=[pl.BlockSpec((1, D_MODEL, FFN_CHUNK), lambda j: (j, 0, 0)),
                   pl.BlockSpec((down_rows, D_MODEL), lambda j: (j, 0))],
        out_shape=[jax.ShapeDtypeStruct((n_blocks, D_MODEL, FFN_CHUNK), BF16),
                   jax.ShapeDtypeStruct((D_FF, D_MODEL), BF16)],
        compiler_params=pltpu.CompilerParams(dimension_semantics=("arbitrary",)),
        name="cast_ffn_weights",
    )(w_up, w_down, norm_ffn_g)
    w_down_blocks = w_down_bf.reshape(FFN_CHUNKS, FFN_CHUNK, D_MODEL)

    vec_params = {
        **{f"ffn_w{k}": ffn_conv_w for k in range(CONV_FFN)},
        **{f"conv_w{k}": conv_lru_w for k in range(CONV_LRU)},
        "ffn_b": ffn_conv_b, "final_g": final_norm_g.reshape(1, D_MODEL), "conv_b": conv_lru_b,
        "ba": gate_a_b, "bx": gate_x_b, "lam": lru_lambda, "pool_b": pool_b, "pool_scale": pool_scale}
    vec_args = [vec_params[name] for name, _ in VEC_SOURCES]
    mix_steps = 4
    mix_rows = D_MODEL // mix_steps
    group_blocks = D_LRU // mix_rows
    assert D_LRU == D_POOL and group_blocks * mix_rows == D_LRU
    d_in = w_in.shape[2]
    const = lambda shape: pl.BlockSpec(shape, lambda j, _n=len(shape): (0,) * _n)
    w_in_bf, w_out_bf, w_gate, w_pool, vectors = pl.pallas_call(
        _prepare_mixer_kernel,
        grid=(mix_steps,),
        in_specs=[pl.BlockSpec((1, mix_rows, d_in), lambda j: (0, j, 0)),
                  pl.BlockSpec((1, mix_rows, D_MODEL), lambda j: (0, j, 0)),
                  pl.BlockSpec((1, mix_rows), lambda j: (0, j)),
                  pl.BlockSpec((1, mix_rows), lambda j: (0, j % group_blocks)),
                  pl.BlockSpec((1, mix_rows), lambda j: (0, j % group_blocks)),
                  const(gate_a_w.shape), const(gate_x_w.shape), const(pool_w.shape)]
        + [const(a.shape) for a in vec_args],
        out_specs=[pl.BlockSpec((mix_rows, d_in), lambda j: (j, 0)),
                   pl.BlockSpec((mix_rows, D_MODEL), lambda j: (j, 0)),
                   const((D_LRU // MXU_DIM, MXU_DIM, 2 * MXU_DIM)),
                   const((D_POOL // MXU_DIM, MXU_DIM, MXU_DIM)),
                   const((VEC_ROWS, 2 * D_FF))],
        out_shape=[jax.ShapeDtypeStruct((D_MODEL, d_in), BF16),
                   jax.ShapeDtypeStruct((D_MODEL, D_MODEL), BF16),
                   jax.ShapeDtypeStruct((D_LRU // MXU_DIM, MXU_DIM, 2 * MXU_DIM), BF16),
                   jax.ShapeDtypeStruct((D_POOL // MXU_DIM, MXU_DIM, MXU_DIM), BF16),
                   jax.ShapeDtypeStruct((VEC_ROWS, 2 * D_FF), F32)],
        compiler_params=pltpu.CompilerParams(dimension_semantics=("arbitrary",)),
        name="prepare_mixer_weights",
    )(w_in, w_out, norm_mix_g, gn_lru, gn_pool, gate_a_w, gate_x_w, pool_w, *vec_args)
    weights = (vectors, w_in_bf, w_gate, w_pool, w_out_bf, w_up_blocks, w_down_blocks)

    y_s, (meta_state, sample_state) = _run_streams(
        (meta_tokens, x_sample), (state_lru_h, state_lru_conv, state_pool, state_ffn_conv), weights,
        x_sample.shape, n_streams=1 + n_sample, rows=N_META,
        start_pos=(0,) + (PAST_LEN,) * n_sample, n_fresh=1, y_skip=N_META,
        out_groups=((0, 1), (1, n_sample)), tiled=False, name="short_streams")

    y_p, (prompt_state,) = _run_streams(
        (x_prompt,), meta_state, weights, x_prompt.shape, n_streams=1, rows=PROMPT_TILE,
        start_pos=(N_META,), n_fresh=0, y_skip=0, out_groups=((0, 1),), tiled=True,
        name="prompt_stream")

    return (y_p, y_s, *prompt_state, *sample_state)
```

```python
import functools

import jax
import jax.numpy as jnp
from jax import lax
from jax.experimental import pallas as pl
from jax.experimental.pallas import tpu as pltpu

D_MODEL = 1024
N_META = 16
PAST_LEN = 1024
D_LRU = 512
N_LRU_HEADS = 8
LRU_HEAD = D_LRU // N_LRU_HEADS
LRU_C = 8.0
CONV_LRU = 4
D_POOL = 512
POOL_WINDOWS = (2, 4, 8, 16)
POOL_GROUP = D_POOL // len(POOL_WINDOWS)
POOL_BUF = max(POOL_WINDOWS) - 1
D_FF = 3 * D_MODEL
CONV_FFN = 3
EPS = 1e-6

SUBLANES = 8
LANES = 128
MXU_DIM = 256
VMEM_BYTES_V7X = 64 * 1024 * 1024

LRU_HDR = SUBLANES
POOL_PAD = SUBLANES
POOL_KEEP = 2 * SUBLANES
POOL_HDR = POOL_PAD + POOL_KEEP
FFN_HDR = SUBLANES
SCAN_HDR = SUBLANES
SCAN_SHIFTS = (1, 2, 4)
FFN_CHUNK = 512
FFN_CHUNKS = D_FF // FFN_CHUNK
LRU_SLABS = D_LRU // LANES
POOL_SLABS = D_POOL // LANES
CHUNK_SLABS = FFN_CHUNK // LANES
FFN_SLABS = 2 * D_FF // LANES
PROMPT_TILE = 512

VEC_ROWS = SUBLANES
VEC_LAYOUT = {}


def _vec_layout():
    for k in range(CONV_FFN):
        VEC_LAYOUT[f"ffn_w{k}"] = (k, 0, 2 * D_FF)
    VEC_LAYOUT["ffn_b"] = (CONV_FFN, 0, 2 * D_FF)
    col = 0
    for name, width in (("final_g", D_MODEL), ("conv_b", D_LRU), ("ba", D_LRU), ("bx", D_LRU),
                        ("lam", D_LRU), ("pool_b", D_POOL), ("pool_scale", D_POOL)):
        VEC_LAYOUT[name] = (CONV_FFN + 1, col, width)
        col += width
    assert col <= 2 * D_FF
    col = 0
    for name, width in [(f"conv_w{k}", D_LRU) for k in range(CONV_LRU)]:
        VEC_LAYOUT[name] = (CONV_FFN + 2, col, width)
        col += width
    assert col <= 2 * D_FF and CONV_FFN + 2 < VEC_ROWS


_vec_layout()


F32 = jnp.float32
BF16 = jnp.bfloat16


def _rmsnorm(x, g=None):
    y = x * lax.rsqrt(jnp.mean(x * x, axis=-1, keepdims=True) + EPS)
    return y if g is None else y * g


def _dot(a, b):
    return jnp.dot(a, b, preferred_element_type=F32)


GELU_C0 = (2.0 / jnp.pi) ** 0.5
GELU_C1 = 0.044715 * GELU_C0


def _gelu(x):
    half = 0.5 * x
    return half + half * jnp.tanh(x * (GELU_C0 + GELU_C1 * (x * x)))


def _lanes(c):
    return slice(c * LANES, (c + 1) * LANES)


def _stream_kernel(*refs, n_streams, rows, start_pos, n_x, n_fresh, y_skip, out_groups):
    S, T = n_streams, rows
    refs = list(refs)
    take = lambda n: [refs.pop(0) for _ in range(n)]
    x_refs = take(n_x)
    h0_ref, conv0_ref, pool0_ref, ffn0_ref = take(4)
    vec_ref, w_in_ref, w_gate_ref, w_pool_ref, w_out_ref, w_up_ref, w_down_ref = take(7)
    y_ref, = take(1)
    state_out_refs = [take(4) for _ in out_groups]
    (h_carry, lru_carry, pool_carry, ffn_carry, ext_lru, ext_pool, ext_up,
     scan_a, scan_b, scan_h, pool_tmp) = refs
    step = pl.program_id(0)

    def vec(name, slab=None):
        row, c0, width = VEC_LAYOUT[name]
        if slab is None:
            return vec_ref[row:row + 1, c0:c0 + width]
        return vec_ref[row:row + 1, c0 + slab * LANES:c0 + (slab + 1) * LANES]

    @pl.when(step == 0)
    def _load_state():
        lru_carry[...] = jnp.zeros_like(lru_carry)
        pool_carry[...] = jnp.zeros_like(pool_carry)
        ext_pool[:, :, :POOL_PAD, :] = jnp.zeros((S, POOL_SLABS, POOL_PAD, LANES), F32)
        pool_tmp[:, :POOL_PAD, :] = jnp.zeros((2, POOL_PAD, LANES), F32)
        ffn_carry[...] = jnp.zeros_like(ffn_carry)
        scan_a[:, :SCAN_HDR, :] = jnp.ones((LRU_SLABS, SCAN_HDR, LANES), F32)
        scan_b[:, :SCAN_HDR, :] = jnp.zeros((LRU_SLABS, SCAN_HDR, LANES), F32)
        for s in range(n_fresh):
            h_carry[s] = jnp.zeros((SUBLANES, D_LRU), F32)
        for s in range(n_fresh, S):
            p = s - n_fresh
            h_carry[s] = jnp.broadcast_to(h0_ref[0, p:p + 1, :], (SUBLANES, D_LRU))
            for c in range(LRU_SLABS):
                for k in range(CONV_LRU - 1):
                    row = LRU_HDR - (CONV_LRU - 1) + k
                    lru_carry[s, c, row:row + 1, :] = conv0_ref[0, k, p:p + 1, _lanes(c)]
            for c in range(POOL_SLABS):
                for k in range(POOL_BUF):
                    row = POOL_KEEP - POOL_BUF + k
                    pool_carry[s, c, row:row + 1, :] = pool0_ref[0, k, p:p + 1, _lanes(c)]
            for c in range(FFN_SLABS):
                ffn_carry[s, c, FFN_HDR - (CONV_FFN - 1):, :] = ffn0_ref[0, p, :, _lanes(c)]

    x = jnp.concatenate([r[...].reshape(-1, D_MODEL) for r in x_refs], axis=0)

    u = _dot(_rmsnorm(x).astype(BF16), w_in_ref[...])
    u_lru = u[:, :D_LRU]
    g_lru = u[:, D_LRU:2 * D_LRU]
    u_pool = u[:, 2 * D_LRU:]

    lam = vec("lam")
    c_log_sig = LRU_C * (jnp.minimum(lam, 0.0) - jnp.log1p(jnp.exp(-jnp.abs(lam))))

    y_lru_parts = []
    m_parts = []
    for s in range(S):
        r0 = s * T
        c_slabs = []
        for c in range(LRU_SLABS):
            u_c = u_lru[r0:r0 + T, _lanes(c)]
            ext_lru[s, c, :LRU_HDR, :] = lru_carry[s, c]
            ext_lru[s, c, LRU_HDR:, :] = u_c
            lru_carry[s, c] = ext_lru[s, c, T:, :]
            acc_c = vec("conv_b", c) + u_c * vec(f"conv_w{CONV_LRU - 1}", c)
            for k in range(1, CONV_LRU):
                acc_c = acc_c + (ext_lru[s, c, LRU_HDR - k:LRU_HDR - k + T, :]
                                 * vec(f"conv_w{CONV_LRU - 1 - k}", c))
            c_slabs.append(acc_c)
        c = jnp.concatenate(c_slabs, axis=1)

        cb = c.astype(BF16)
        gates = [_dot(cb[:, q * MXU_DIM:(q + 1) * MXU_DIM], w_gate_ref[q])
                 for q in range(D_LRU // MXU_DIM)]
        r = jax.nn.sigmoid(jnp.concatenate([g[:, :MXU_DIM] for g in gates], axis=1) + vec("ba"))
        i = jax.nn.sigmoid(jnp.concatenate([g[:, MXU_DIM:] for g in gates], axis=1) + vec("bx"))
        log_a = r * c_log_sig
        a = jnp.exp(log_a)
        one_minus_a2 = jnp.tanh(log_a) * (-1.0 - a * a)
        mult = jnp.where(one_minus_a2 > 0.0, one_minus_a2 * lax.rsqrt(one_minus_a2), 0.0)
        b = mult * (i * c)

        for d in SCAN_SHIFTS:
            a_prev, b_prev = [], []
            for cc in range(LRU_SLABS):
                scan_a[cc, SCAN_HDR:, :] = a[:, _lanes(cc)]
                scan_b[cc, SCAN_HDR:, :] = b[:, _lanes(cc)]
                a_prev.append(scan_a[cc, SCAN_HDR - d:SCAN_HDR - d + T, :])
                b_prev.append(scan_b[cc, SCAN_HDR - d:SCAN_HDR - d + T, :])
            b = a * jnp.concatenate(b_prev, axis=1) + b
            a = a * jnp.concatenate(a_prev, axis=1)
        h = h_carry[s]
        for g in range(T // SUBLANES):
            lo = g * SUBLANES
            h = a[lo:lo + SUBLANES] * h + b[lo:lo + SUBLANES]
            scan_h[r0 + lo:r0 + lo + SUBLANES, :] = h
        h_carry[s] = jnp.broadcast_to(scan_h[r0 + T - 1:r0 + T, :], (SUBLANES, D_LRU))
        y_lru_parts.append(scan_h[r0:r0 + T, :] * _gelu(g_lru[r0:r0 + T]))

        ramp_up = start_pos[s] + 1 < max(POOL_WINDOWS)
        if ramp_up:
            pos1 = (lax.broadcasted_iota(jnp.int32, (T, POOL_GROUP), 0) + (start_pos[s] + 1)).astype(F32)
        means = []
        for gi, w in enumerate(POOL_WINDOWS):
            ext_pool[s, gi, POOL_PAD:POOL_HDR, :] = pool_carry[s, gi]
            ext_pool[s, gi, POOL_HDR:, :] = u_pool[r0:r0 + T, _lanes(gi)]
            pool_carry[s, gi] = ext_pool[s, gi, T + POOL_PAD:, :]
            cur = ext_pool.at[s, gi]
            n_steps_ = w.bit_length() - 1
            for l in range(n_steps_):
                d = 1 << l
                lo = POOL_HDR if l == n_steps_ - 1 else POOL_PAD
                partial = cur[lo:, :] + cur[lo - d:POOL_HDR + T - d, :]
                if l < n_steps_ - 1:
                    cur = pool_tmp.at[l % 2]
                    cur[POOL_PAD:, :] = partial
            means.append(partial / jnp.minimum(pos1, float(w)) if ramp_up else partial * (1.0 / w))
        m_parts.append(jnp.concatenate(means, axis=1) - u_pool[r0:r0 + T])

    y_lru = y_lru_parts[0] if S == 1 else jnp.concatenate(y_lru_parts, axis=0)
    m = (m_parts[0] if S == 1 else jnp.concatenate(m_parts, axis=0)).astype(BF16)

    y_pool = jnp.concatenate(
        [_dot(m[:, q * MXU_DIM:(q + 1) * MXU_DIM], w_pool_ref[q]) for q in range(D_POOL // MXU_DIM)],
        axis=1)
    y_pool = (y_pool + vec("pool_b")) * vec("pool_scale")

    mix_in = jnp.concatenate(
        [_rmsnorm(y_lru), _rmsnorm(y_pool)], axis=1)
    x1 = x + _dot(mix_in.astype(BF16), w_out_ref[...])

    xn = _rmsnorm(x1).astype(BF16)
    up_pair = lambda j: [_dot(xn, w_up_ref[half * FFN_CHUNKS + j]) for half in range(2)]
    acc = jnp.zeros((S * T, D_MODEL), F32)
    ups = up_pair(0)
    for j in range(FFN_CHUNKS):
        ups_next = up_pair(j + 1) if j + 1 < FFN_CHUNKS else None
        halves = []
        for half in range(2):
            up = ups[half]
            parts = []
            for s in range(S):
                r0 = s * T
                slabs = []
                for c in range(CHUNK_SLABS):
                    gs = (half * FFN_CHUNKS + j) * CHUNK_SLABS + c
                    ext = ext_up.at[j % 2, half, s, c]
                    up_c = up[r0:r0 + T, _lanes(c)]
                    ext[:FFN_HDR, :] = ffn_carry[s, gs]
                    ext[FFN_HDR:, :] = up_c
                    ffn_carry[s, gs] = ext[T:, :]
                    upc = vec("ffn_b", gs) + up_c * vec(f"ffn_w{CONV_FFN - 1}", gs)
                    for k in range(1, CONV_FFN):
                        upc = upc + ext[FFN_HDR - k:FFN_HDR - k + T, :] * vec(f"ffn_w{CONV_FFN - 1 - k}", gs)
                    slabs.append(upc)
                parts.append(jnp.concatenate(slabs, axis=1))
            halves.append(parts[0] if S == 1 else jnp.concatenate(parts, axis=0))
        act = (_gelu(halves[0]) * halves[1]).astype(BF16)
        acc = acc + _dot(act, w_down_ref[j])
        ups = ups_next

    y = _rmsnorm(x1 + acc, vec("final_g"))
    y_ref[...] = y[y_skip:].reshape(y_ref.shape)

    @pl.when(step == pl.num_programs(0) - 1)
    def _store_state():
        for (first, count), (h_out_ref, conv_out_ref, pool_out_ref, ffn_out_ref) in zip(
                out_groups, state_out_refs):
            for p in range(count):
                s = first + p
                h_out_ref[0, p:p + 1, :] = h_carry[s, :1, :]
                for c in range(LRU_SLABS):
                    for k in range(CONV_LRU - 1):
                        row = LRU_HDR - (CONV_LRU - 1) + k
                        conv_out_ref[0, k, p:p + 1, _lanes(c)] = lru_carry[s, c, row:row + 1, :]
                for c in range(POOL_SLABS):
                    for k in range(POOL_BUF):
                        row = POOL_KEEP - POOL_BUF + k
                        pool_out_ref[0, k, p:p + 1, _lanes(c)] = pool_carry[s, c, row:row + 1, :]
                for c in range(FFN_SLABS):
                    ffn_out_ref[0, p, :, _lanes(c)] = ffn_carry[s, c, FFN_HDR - (CONV_FFN - 1):, :]


def _scratch_shapes(n_streams, rows):
    S, T = n_streams, rows
    return [
        pltpu.VMEM((S, SUBLANES, D_LRU), F32),
        pltpu.VMEM((S, LRU_SLABS, LRU_HDR, LANES), F32),
        pltpu.VMEM((S, POOL_SLABS, POOL_KEEP, LANES), F32),
        pltpu.VMEM((S, FFN_SLABS, FFN_HDR, LANES), F32),
        pltpu.VMEM((S, LRU_SLABS, LRU_HDR + T, LANES), F32),
        pltpu.VMEM((S, POOL_SLABS, POOL_HDR + T, LANES), F32),
        pltpu.VMEM((2, 2, S, CHUNK_SLABS, FFN_HDR + T, LANES), F32),
        pltpu.VMEM((LRU_SLABS, SCAN_HDR + T, LANES), F32),
        pltpu.VMEM((LRU_SLABS, SCAN_HDR + T, LANES), F32),
        pltpu.VMEM((S * T, D_LRU), F32),
        pltpu.VMEM((2, POOL_HDR + T, LANES), F32),
    ]


def _vmem_limit(n_streams, rows, weights):
    m = n_streams * rows
    weight_bytes = sum(w.size * w.dtype.itemsize for w in weights)
    io_bytes = 2 * 2 * m * D_MODEL * 4
    scratch_bytes = 0
    for sc in _scratch_shapes(n_streams, rows):
        n = 4
        for d in sc.shape:
            n *= d
        scratch_bytes += n
    live_values = 10 * m * D_MODEL * 4
    return min(weight_bytes + io_bytes + scratch_bytes + live_values, VMEM_BYTES_V7X - (6 << 20))


def _state_shapes(count):
    return [
        jax.ShapeDtypeStruct((1, count, D_LRU), F32),
        jax.ShapeDtypeStruct((1, CONV_LRU - 1, count, D_LRU), F32),
        jax.ShapeDtypeStruct((1, POOL_BUF, count, D_POOL), F32),
        jax.ShapeDtypeStruct((1, count, CONV_FFN - 1, 2 * D_FF), F32),
    ]


def _rows_major(a):
    return a.transpose(0, 2, 1, 3)


def _run_streams(xs, states, weights, y_shape, *, n_streams, rows, start_pos, n_fresh, y_skip,
                 out_groups, tiled, name):
    S, T = n_streams, rows
    m = S * T
    n_steps = xs[0].shape[1] // m if tiled else 1
    const = lambda shape: pl.BlockSpec(shape, lambda i, _n=len(shape): (0,) * _n)
    whole = lambda a: pl.BlockSpec(a.shape, lambda i, _n=a.ndim: (0,) * _n,
                                   pipeline_mode=pl.Buffered(1))
    tile_spec = pl.BlockSpec((None, m, D_MODEL), lambda i: (0, i, 0))
    x_specs = [tile_spec] if tiled else [whole(a) for a in xs]
    out_shape = [jax.ShapeDtypeStruct(y_shape, F32)]
    for _, count in out_groups:
        out_shape += _state_shapes(count)
    kernel = functools.partial(
        _stream_kernel, n_streams=S, rows=T, start_pos=start_pos, n_x=len(xs), n_fresh=n_fresh,
        y_skip=y_skip, out_groups=out_groups)
    outs = pl.pallas_call(
        kernel,
        grid=(n_steps,),
        in_specs=x_specs + [whole(a) for a in states] + [whole(w) for w in weights],
        out_specs=[tile_spec if tiled else const(y_shape)] + [const(o.shape) for o in out_shape[1:]],
        out_shape=out_shape,
        scratch_shapes=_scratch_shapes(S, T),
        compiler_params=pltpu.CompilerParams(
            dimension_semantics=("arbitrary",),
            vmem_limit_bytes=_vmem_limit(S, T, weights)),
        name=name,
    )(*xs, *states, *weights)
    return outs[0], [outs[1 + 4 * g:5 + 4 * g] for g in range(len(out_groups))]


def _row_gain(gain_row):
    return jnp.broadcast_to(gain_row, (LANES, gain_row.shape[1])).T


def _scale_rows(w, gain_row):
    g = _row_gain(gain_row)
    return jnp.concatenate([w[:, _lanes(c)] * g for c in range(w.shape[1] // LANES)], axis=1)


def _cast_ffn_kernel(w_up_ref, w_down_ref, gain_ref, up_out_ref, down_out_ref):
    up_out_ref[0] = _scale_rows(w_up_ref[0], gain_ref[...]).astype(BF16)
    down_out_ref[...] = w_down_ref[0].astype(BF16)


def _prepare_mixer_kernel(*refs):
    (w_in_ref, w_out_ref, in_gain_ref, lru_gain_ref, pool_gain_ref, gate_a_ref, gate_x_ref,
     pool_w_ref), refs = refs[:8], refs[8:]
    n_vec = len(VEC_SOURCES)
    vec_refs, (w_in_out, w_out_out, w_gate_out, w_pool_out, vec_out) = refs[:n_vec], refs[n_vec:]
    w_in_out[...] = _scale_rows(w_in_ref[0], in_gain_ref[...]).astype(BF16)
    lru_rows = pl.program_id(0) < D_LRU // w_out_out.shape[0]
    out_gain = jnp.where(lru_rows, lru_gain_ref[...], pool_gain_ref[...])
    w_out_out[...] = _scale_rows(w_out_ref[0], out_gain).astype(BF16)

    @pl.when(pl.program_id(0) == 0)
    def _small_weights():
        heads = MXU_DIM // LRU_HEAD
        w_gate_out[...] = jnp.zeros_like(w_gate_out)
        for q in range(D_LRU // MXU_DIM):
            for p in range(heads):
                rows_ = slice(p * LRU_HEAD, (p + 1) * LRU_HEAD)
                for half, src_ref in enumerate((gate_a_ref, gate_x_ref)):
                    c0 = half * MXU_DIM + p * LRU_HEAD
                    w_gate_out[q, rows_, c0:c0 + LRU_HEAD] = src_ref[0, q * heads + p].astype(BF16)
        groups = MXU_DIM // POOL_GROUP
        w_pool_out[...] = jnp.zeros_like(w_pool_out)
        for q in range(D_POOL // MXU_DIM):
            for p in range(groups):
                blk = slice(p * POOL_GROUP, (p + 1) * POOL_GROUP)
                w_pool_out[q, blk, blk] = pool_w_ref[0, q * groups + p].astype(BF16)
        vec_out[...] = jnp.zeros_like(vec_out)
        for (name, pick), ref in zip(VEC_SOURCES, vec_refs):
            row, c0, width = VEC_LAYOUT[name]
            vec_out[row:row + 1, c0:c0 + width] = pick(ref)


VEC_SOURCES = (
    [(f"ffn_w{k}", lambda r, k=k: r[k]) for k in range(CONV_FFN)]
    + [(f"conv_w{k}", lambda r, k=k: r[0, k:k + 1, :]) for k in range(CONV_LRU)]
    + [(name, lambda r: r[...]) for name in (
        "ffn_b", "final_g", "conv_b", "ba", "bx", "lam", "pool_b", "pool_scale")])


def kernel(x_prompt, x_sample, state_lru_h, state_lru_conv, state_pool, state_ffn_conv, meta_tokens, norm_mix_g, w_in, conv_lru_w, conv_lru_b, gate_a_w, gate_a_b, gate_x_w, gate_x_b, lru_lambda, pool_w, pool_b, pool_scale, gn_lru, gn_pool, w_out, norm_ffn_g, w_up, ffn_conv_w, ffn_conv_b, w_down, final_norm_g):
    assert x_prompt.shape[0] == 1 and norm_mix_g.shape[0] == 1
    n_sample, sample_rows = x_sample.shape[:2]
    assert sample_rows == N_META

    n_blocks = 2 * FFN_CHUNKS
    down_rows = D_FF // n_blocks
    w_up_blocks, w_down_bf = pl.pallas_call(
        _cast_ffn_kernel,
        grid=(n_blocks,),
        in_specs=[pl.BlockSpec((1, D_MODEL, FFN_CHUNK), lambda j: (0, 0, j)),
                  pl.BlockSpec((1, down_rows, D_MODEL), lambda j: (0, j, 0)),
                  pl.BlockSpec((1, D_MODEL), lambda j: (0, 0))],
        out_specs=[pl.BlockSpec((1, D_MODEL, FFN_CHUNK), lambda j: (j, 0, 0)),
                   pl.BlockSpec((down_rows, D_MODEL), lambda j: (j, 0))],
        out_shape=[jax.ShapeDtypeStruct((n_blocks, D_MODEL, FFN_CHUNK), BF16),
                   jax.ShapeDtypeStruct((D_FF, D_MODEL), BF16)],
        compiler_params=pltpu.CompilerParams(dimension_semantics=("arbitrary",)),
        name="cast_ffn_weights",
    )(w_up, w_down, norm_ffn_g)
    w_down_blocks = w_down_bf.reshape(FFN_CHUNKS, FFN_CHUNK, D_MODEL)

    vec_params = {
        **{f"ffn_w{k}": ffn_conv_w.transpose(1, 0, 2) for k in range(CONV_FFN)},
        **{f"conv_w{k}": conv_lru_w for k in range(CONV_LRU)},
        "ffn_b": ffn_conv_b, "final_g": final_norm_g.reshape(1, D_MODEL), "conv_b": conv_lru_b,
        "ba": gate_a_b, "bx": gate_x_b, "lam": lru_lambda, "pool_b": pool_b, "pool_scale": pool_scale}
    vec_args = [vec_params[name] for name, _ in VEC_SOURCES]
    mix_steps = 4
    mix_rows = D_MODEL // mix_steps
    group_blocks = D_LRU // mix_rows
    assert D_LRU == D_POOL and group_blocks * mix_rows == D_LRU
    d_in = w_in.shape[2]
    const = lambda shape: pl.BlockSpec(shape, lambda j, _n=len(shape): (0,) * _n)
    w_in_bf, w_out_bf, w_gate, w_pool, vectors = pl.pallas_call(
        _prepare_mixer_kernel,
        grid=(mix_steps,),
        in_specs=[pl.BlockSpec((1, mix_rows, d_in), lambda j: (0, j, 0)),
                  pl.BlockSpec((1, mix_rows, D_MODEL), lambda j: (0, j, 0)),
                  pl.BlockSpec((1, mix_rows), lambda j: (0, j)),
                  pl.BlockSpec((1, mix_rows), lambda j: (0, j % group_blocks)),
                  pl.BlockSpec((1, mix_rows), lambda j: (0, j % group_blocks)),
                  const(gate_a_w.shape), const(gate_x_w.shape), const(pool_w.shape)]
        + [const(a.shape) for a in vec_args],
        out_specs=[pl.BlockSpec((mix_rows, d_in), lambda j: (j, 0)),
                   pl.BlockSpec((mix_rows, D_MODEL), lambda j: (j, 0)),
                   const((D_LRU // MXU_DIM, MXU_DIM, 2 * MXU_DIM)),
                   const((D_POOL // MXU_DIM, MXU_DIM, MXU_DIM)),
                   const((VEC_ROWS, 2 * D_FF))],
        out_shape=[jax.ShapeDtypeStruct((D_MODEL, d_in), BF16),
                   jax.ShapeDtypeStruct((D_MODEL, D_MODEL), BF16),
                   jax.ShapeDtypeStruct((D_LRU // MXU_DIM, MXU_DIM, 2 * MXU_DIM), BF16),
                   jax.ShapeDtypeStruct((D_POOL // MXU_DIM, MXU_DIM, MXU_DIM), BF16),
                   jax.ShapeDtypeStruct((VEC_ROWS, 2 * D_FF), F32)],
        compiler_params=pltpu.CompilerParams(dimension_semantics=("arbitrary",)),
        name="prepare_mixer_weights",
    )(w_in, w_out, norm_mix_g, gn_lru, gn_pool, gate_a_w, gate_x_w, pool_w, *vec_args)
    weights = (vectors, w_in_bf, w_gate, w_pool, w_out_bf, w_up_blocks, w_down_blocks)

    y_s, (meta_state, sample_state) = _run_streams(
        (meta_tokens, x_sample),
        (state_lru_h, _rows_major(state_lru_conv), _rows_major(state_pool), state_ffn_conv), weights,
        x_sample.shape, n_streams=1 + n_sample, rows=N_META,
        start_pos=(0,) + (PAST_LEN,) * n_sample, n_fresh=1, y_skip=N_META,
        out_groups=((0, 1), (1, n_sample)), tiled=False, name="short_streams")

    y_p, (prompt_state,) = _run_streams(
        (x_prompt,), meta_state, weights, x_prompt.shape, n_streams=1, rows=PROMPT_TILE,
        start_pos=(N_META,), n_fresh=0, y_skip=0, out_groups=((0, 1),), tiled=True,
        name="prompt_stream")

    restore = lambda st: (st[0], _rows_major(st[1]), _rows_major(st[2]), st[3])
    return (y_p, y_s, *restore(prompt_state), *restore(sample_state))
```

```python
import functools

import jax
import jax.numpy as jnp
from jax import lax
from jax.experimental import pallas as pl
from jax.experimental.pallas import tpu as pltpu

D_MODEL = 1024
N_META = 16
PAST_LEN = 1024
D_LRU = 512
N_LRU_HEADS = 8
LRU_HEAD = D_LRU // N_LRU_HEADS
LRU_C = 8.0
CONV_LRU = 4
D_POOL = 512
POOL_WINDOWS = (2, 4, 8, 16)
POOL_GROUP = D_POOL // len(POOL_WINDOWS)
POOL_BUF = max(POOL_WINDOWS) - 1
D_FF = 3 * D_MODEL
CONV_FFN = 3
EPS = 1e-6

SUBLANES = 8
LANES = 128
MXU_DIM = 256
VMEM_BYTES_V7X = 64 * 1024 * 1024
VMEM_RESERVE = 6 * 1024 * 1024

LRU_HDR = SUBLANES
POOL_PAD = SUBLANES
POOL_KEEP = 2 * SUBLANES
POOL_HDR = POOL_PAD + POOL_KEEP
FFN_HDR = SUBLANES
SCAN_HDR = SUBLANES
SCAN_SHIFTS = (1, 2, 4)
FFN_CHUNK = 512
FFN_CHUNKS = D_FF // FFN_CHUNK
LRU_SLABS = D_LRU // LANES
POOL_SLABS = D_POOL // LANES
CHUNK_SLABS = FFN_CHUNK // LANES
FFN_SLABS = 2 * D_FF // LANES
PROMPT_TILE = 512

VEC_ROWS = SUBLANES
VEC_LAYOUT = {}


def _vec_layout():
    for k in range(CONV_FFN):
        VEC_LAYOUT[f"ffn_w{k}"] = (k, 0, 2 * D_FF)
    VEC_LAYOUT["ffn_b"] = (CONV_FFN, 0, 2 * D_FF)
    col = 0
    for name, width in (("final_g", D_MODEL), ("conv_b", D_LRU), ("ba", D_LRU), ("bx", D_LRU),
                        ("lam", D_LRU), ("pool_b", D_POOL), ("pool_scale", D_POOL)):
        VEC_LAYOUT[name] = (CONV_FFN + 1, col, width)
        col += width
    assert col <= 2 * D_FF
    col = 0
    for name, width in [(f"conv_w{k}", D_LRU) for k in range(CONV_LRU)]:
        VEC_LAYOUT[name] = (CONV_FFN + 2, col, width)
        col += width
    assert col <= 2 * D_FF and CONV_FFN + 2 < VEC_ROWS


_vec_layout()


F32 = jnp.float32
BF16 = jnp.bfloat16


def _rmsnorm(x, g=None):
    y = x * lax.rsqrt(jnp.mean(x * x, axis=-1, keepdims=True) + EPS)
    return y if g is None else y * g


def _dot(a, b):
    return jnp.dot(a, b, preferred_element_type=F32)


GELU_C0 = (2.0 / jnp.pi) ** 0.5
GELU_C1 = 0.044715 * GELU_C0


def _gelu(x):
    half = 0.5 * x
    return half + half * jnp.tanh(x * (GELU_C0 + GELU_C1 * (x * x)))


def _lanes(c):
    return slice(c * LANES, (c + 1) * LANES)


def _stream_kernel(*refs, n_streams, rows, start_pos, n_x, n_fresh, y_skip, out_groups):
    S, T = n_streams, rows
    refs = list(refs)
    take = lambda n: [refs.pop(0) for _ in range(n)]
    x_refs = take(n_x)
    h0_ref, conv0_ref, pool0_ref, ffn0_ref = take(4)
    vec_ref, w_in_ref, w_gate_ref, w_pool_ref, w_out_ref, w_up_ref, w_down_ref = take(7)
    y_ref, = take(1)
    state_out_refs = [take(4) for _ in out_groups]
    (h_carry, lru_carry, pool_carry, ffn_carry, ext_lru, ext_pool, ext_up,
     scan_a, scan_b, scan_h, pool_tmp) = refs
    step = pl.program_id(0)

    def vec(name, slab=None):
        row, c0, width = VEC_LAYOUT[name]
        if slab is None:
            return vec_ref[row:row + 1, c0:c0 + width]
        return vec_ref[row:row + 1, c0 + slab * LANES:c0 + (slab + 1) * LANES]

    @pl.when(step == 0)
    def _load_state():
        lru_carry[...] = jnp.zeros_like(lru_carry)
        pool_carry[...] = jnp.zeros_like(pool_carry)
        ext_pool[:, :, :POOL_PAD, :] = jnp.zeros((S, POOL_SLABS, POOL_PAD, LANES), F32)
        pool_tmp[:, :POOL_PAD, :] = jnp.zeros((2, POOL_PAD, LANES), F32)
        ffn_carry[...] = jnp.zeros_like(ffn_carry)
        scan_a[:, :SCAN_HDR, :] = jnp.ones((LRU_SLABS, SCAN_HDR, LANES), F32)
        scan_b[:, :SCAN_HDR, :] = jnp.zeros((LRU_SLABS, SCAN_HDR, LANES), F32)
        for s in range(n_fresh):
            h_carry[s] = jnp.zeros((SUBLANES, D_LRU), F32)
        for s in range(n_fresh, S):
            p = s - n_fresh
            h_carry[s] = jnp.broadcast_to(h0_ref[0, p:p + 1, :], (SUBLANES, D_LRU))
            for c in range(LRU_SLABS):
                for k in range(CONV_LRU - 1):
                    row = LRU_HDR - (CONV_LRU - 1) + k
                    lru_carry[s, c, row:row + 1, :] = conv0_ref[0, k, p:p + 1, _lanes(c)]
            for c in range(POOL_SLABS):
                for k in range(POOL_BUF):
                    row = POOL_KEEP - POOL_BUF + k
                    pool_carry[s, c, row:row + 1, :] = pool0_ref[0, k, p:p + 1, _lanes(c)]
            for c in range(FFN_SLABS):
                ffn_carry[s, c, FFN_HDR - (CONV_FFN - 1):, :] = ffn0_ref[0, p, :, _lanes(c)]

    x = jnp.concatenate([r[...].reshape(-1, D_MODEL) for r in x_refs], axis=0)

    u = _dot(_rmsnorm(x).astype(BF16), w_in_ref[...])
    u_lru = u[:, :D_LRU]
    g_lru = u[:, D_LRU:2 * D_LRU]
    u_pool = u[:, 2 * D_LRU:]

    lam = vec("lam")
    c_log_sig = LRU_C * (jnp.minimum(lam, 0.0) - jnp.log1p(jnp.exp(-jnp.abs(lam))))

    y_lru_parts = []
    m_parts = []
    for s in range(S):
        r0 = s * T
        c_slabs = []
        for c in range(LRU_SLABS):
            u_c = u_lru[r0:r0 + T, _lanes(c)]
            ext_lru[s, c, :LRU_HDR, :] = lru_carry[s, c]
            ext_lru[s, c, LRU_HDR:, :] = u_c
            lru_carry[s, c] = ext_lru[s, c, T:, :]
            acc_c = vec("conv_b", c) + u_c * vec(f"conv_w{CONV_LRU - 1}", c)
            for k in range(1, CONV_LRU):
                acc_c = acc_c + (ext_lru[s, c, LRU_HDR - k:LRU_HDR - k + T, :]
                                 * vec(f"conv_w{CONV_LRU - 1 - k}", c))
            c_slabs.append(acc_c)
        c = jnp.concatenate(c_slabs, axis=1)

        cb = c.astype(BF16)
        gates = [_dot(cb[:, q * MXU_DIM:(q + 1) * MXU_DIM], w_gate_ref[q])
                 for q in range(D_LRU // MXU_DIM)]
        r = jax.nn.sigmoid(jnp.concatenate([g[:, :MXU_DIM] for g in gates], axis=1) + vec("ba"))
        i = jax.nn.sigmoid(jnp.concatenate([g[:, MXU_DIM:] for g in gates], axis=1) + vec("bx"))
        log_a = r * c_log_sig
        a = jnp.exp(log_a)
        one_minus_a2 = jnp.tanh(log_a) * (-1.0 - a * a)
        mult = jnp.where(one_minus_a2 > 0.0, one_minus_a2 * lax.rsqrt(one_minus_a2), 0.0)
        b = mult * (i * c)

        for d in SCAN_SHIFTS:
            a_prev, b_prev = [], []
            for cc in range(LRU_SLABS):
                scan_a[cc, SCAN_HDR:, :] = a[:, _lanes(cc)]
                scan_b[cc, SCAN_HDR:, :] = b[:, _lanes(cc)]
                a_prev.append(scan_a[cc, SCAN_HDR - d:SCAN_HDR - d + T, :])
                b_prev.append(scan_b[cc, SCAN_HDR - d:SCAN_HDR - d + T, :])
            b = a * jnp.concatenate(b_prev, axis=1) + b
            a = a * jnp.concatenate(a_prev, axis=1)
        h = h_carry[s]
        for g in range(T // SUBLANES):
            lo = g * SUBLANES
            h = a[lo:lo + SUBLANES] * h + b[lo:lo + SUBLANES]
            scan_h[r0 + lo:r0 + lo + SUBLANES, :] = h
        h_carry[s] = jnp.broadcast_to(scan_h[r0 + T - 1:r0 + T, :], (SUBLANES, D_LRU))
        y_lru_parts.append(scan_h[r0:r0 + T, :] * _gelu(g_lru[r0:r0 + T]))

        ramp_up = start_pos[s] + 1 < max(POOL_WINDOWS)
        if ramp_up:
            pos1 = (lax.broadcasted_iota(jnp.int32, (T, POOL_GROUP), 0) + (start_pos[s] + 1)).astype(F32)
        means = []
        for gi, w in enumerate(POOL_WINDOWS):
            ext_pool[s, gi, POOL_PAD:POOL_HDR, :] = pool_carry[s, gi]
            ext_pool[s, gi, POOL_HDR:, :] = u_pool[r0:r0 + T, _lanes(gi)]
            pool_carry[s, gi] = ext_pool[s, gi, T + POOL_PAD:, :]
            cur = ext_pool.at[s, gi]
            n_steps_ = w.bit_length() - 1
            for l in range(n_steps_):
                d = 1 << l
                lo = POOL_HDR if l == n_steps_ - 1 else POOL_PAD
                partial = cur[lo:, :] + cur[lo - d:POOL_HDR + T - d, :]
                if l < n_steps_ - 1:
                    cur = pool_tmp.at[l % 2]
                    cur[POOL_PAD:, :] = partial
            means.append(partial / jnp.minimum(pos1, float(w)) if ramp_up else partial * (1.0 / w))
        m_parts.append(jnp.concatenate(means, axis=1) - u_pool[r0:r0 + T])

    y_lru = y_lru_parts[0] if S == 1 else jnp.concatenate(y_lru_parts, axis=0)
    m = (m_parts[0] if S == 1 else jnp.concatenate(m_parts, axis=0)).astype(BF16)

    y_pool = jnp.concatenate(
        [_dot(m[:, q * MXU_DIM:(q + 1) * MXU_DIM], w_pool_ref[q]) for q in range(D_POOL // MXU_DIM)],
        axis=1)
    y_pool = (y_pool + vec("pool_b")) * vec("pool_scale")

    mix_in = jnp.concatenate(
        [_rmsnorm(y_lru), _rmsnorm(y_pool)], axis=1)
    x1 = x + _dot(mix_in.astype(BF16), w_out_ref[...])

    xn = _rmsnorm(x1).astype(BF16)
    up_pair = lambda j: [_dot(xn, w_up_ref[half * FFN_CHUNKS + j]) for half in range(2)]
    acc = jnp.zeros((S * T, D_MODEL), F32)
    ups = up_pair(0)
    for j in range(FFN_CHUNKS):
        ups_next = up_pair(j + 1) if j + 1 < FFN_CHUNKS else None
        halves = []
        for half in range(2):
            up = ups[half]
            parts = []
            for s in range(S):
                r0 = s * T
                slabs = []
                for c in range(CHUNK_SLABS):
                    gs = (half * FFN_CHUNKS + j) * CHUNK_SLABS + c
                    ext = ext_up.at[j % 2, half, s, c]
                    up_c = up[r0:r0 + T, _lanes(c)]
                    ext[:FFN_HDR, :] = ffn_carry[s, gs]
                    ext[FFN_HDR:, :] = up_c
                    ffn_carry[s, gs] = ext[T:, :]
                    upc = vec("ffn_b", gs) + up_c * vec(f"ffn_w{CONV_FFN - 1}", gs)
                    for k in range(1, CONV_FFN):
                        upc = upc + ext[FFN_HDR - k:FFN_HDR - k + T, :] * vec(f"ffn_w{CONV_FFN - 1 - k}", gs)
                    slabs.append(upc)
                parts.append(jnp.concatenate(slabs, axis=1))
            halves.append(parts[0] if S == 1 else jnp.concatenate(parts, axis=0))
        act = (_gelu(halves[0]) * halves[1]).astype(BF16)
        acc = acc + _dot(act, w_down_ref[j])
        ups = ups_next

    y = _rmsnorm(x1 + acc, vec("final_g"))
    y_ref[...] = y[y_skip:].reshape(y_ref.shape)

    @pl.when(step == pl.num_programs(0) - 1)
    def _store_state():
        for (first, count), (h_out_ref, conv_out_ref, pool_out_ref, ffn_out_ref) in zip(
                out_groups, state_out_refs):
            for p in range(count):
                s = first + p
                h_out_ref[0, p:p + 1, :] = h_carry[s, :1, :]
                for c in range(LRU_SLABS):
                    for k in range(CONV_LRU - 1):
                        row = LRU_HDR - (CONV_LRU - 1) + k
                        conv_out_ref[0, k, p:p + 1, _lanes(c)] = lru_carry[s, c, row:row + 1, :]
                for c in range(POOL_SLABS):
                    for k in range(POOL_BUF):
                        row = POOL_KEEP - POOL_BUF + k
                        pool_out_ref[0, k, p:p + 1, _lanes(c)] = pool_carry[s, c, row:row + 1, :]
                for c in range(FFN_SLABS):
                    ffn_out_ref[0, p, :, _lanes(c)] = ffn_carry[s, c, FFN_HDR - (CONV_FFN - 1):, :]


def _scratch_shapes(n_streams, rows):
    S, T = n_streams, rows
    return [
        pltpu.VMEM((S, SUBLANES, D_LRU), F32),
        pltpu.VMEM((S, LRU_SLABS, LRU_HDR, LANES), F32),
        pltpu.VMEM((S, POOL_SLABS, POOL_KEEP, LANES), F32),
        pltpu.VMEM((S, FFN_SLABS, FFN_HDR, LANES), F32),
        pltpu.VMEM((S, LRU_SLABS, LRU_HDR + T, LANES), F32),
        pltpu.VMEM((S, POOL_SLABS, POOL_HDR + T, LANES), F32),
        pltpu.VMEM((2, 2, S, CHUNK_SLABS, FFN_HDR + T, LANES), F32),
        pltpu.VMEM((LRU_SLABS, SCAN_HDR + T, LANES), F32),
        pltpu.VMEM((LRU_SLABS, SCAN_HDR + T, LANES), F32),
        pltpu.VMEM((S * T, D_LRU), F32),
        pltpu.VMEM((2, POOL_HDR + T, LANES), F32),
    ]


def _vmem_limit(n_streams, rows, weights):
    m = n_streams * rows
    weight_bytes = sum(w.size * w.dtype.itemsize for w in weights)
    io_bytes = 2 * 2 * m * D_MODEL * 4
    scratch_bytes = 0
    for sc in _scratch_shapes(n_streams, rows):
        n = jnp.dtype(sc.dtype).itemsize
        for d in sc.shape:
            n *= d
        scratch_bytes += n
    budget = VMEM_BYTES_V7X - VMEM_RESERVE
    assert weight_bytes + io_bytes + scratch_bytes < budget
    return budget


def _state_shapes(count):
    return [
        jax.ShapeDtypeStruct((1, count, D_LRU), F32),
        jax.ShapeDtypeStruct((1, CONV_LRU - 1, count, D_LRU), F32),
        jax.ShapeDtypeStruct((1, POOL_BUF, count, D_POOL), F32),
        jax.ShapeDtypeStruct((1, count, CONV_FFN - 1, 2 * D_FF), F32),
    ]


def _rows_major(a):
    return a.transpose(0, 2, 1, 3)


def _run_streams(xs, states, weights, y_shape, *, n_streams, rows, start_pos, n_fresh, y_skip,
                 out_groups, tiled, name):
    S, T = n_streams, rows
    m = S * T
    n_steps = xs[0].shape[1] // m if tiled else 1
    const = lambda shape: pl.BlockSpec(shape, lambda i, _n=len(shape): (0,) * _n)
    whole = lambda a: pl.BlockSpec(a.shape, lambda i, _n=a.ndim: (0,) * _n,
                                   pipeline_mode=pl.Buffered(1))
    tile_spec = pl.BlockSpec((None, m, D_MODEL), lambda i: (0, i, 0))
    x_specs = [tile_spec] if tiled else [whole(a) for a in xs]
    out_shape = [jax.ShapeDtypeStruct(y_shape, F32)]
    for _, count in out_groups:
        out_shape += _state_shapes(count)
    kernel = functools.partial(
        _stream_kernel, n_streams=S, rows=T, start_pos=start_pos, n_x=len(xs), n_fresh=n_fresh,
        y_skip=y_skip, out_groups=out_groups)
    outs = pl.pallas_call(
        kernel,
        grid=(n_steps,),
        in_specs=x_specs + [whole(a) for a in states] + [whole(w) for w in weights],
        out_specs=[tile_spec if tiled else const(y_shape)] + [const(o.shape) for o in out_shape[1:]],
        out_shape=out_shape,
        scratch_shapes=_scratch_shapes(S, T),
        compiler_params=pltpu.CompilerParams(
            dimension_semantics=("arbitrary",),
            vmem_limit_bytes=_vmem_limit(S, T, weights)),
        name=name,
    )(*xs, *states, *weights)
    return outs[0], [outs[1 + 4 * g:5 + 4 * g] for g in range(len(out_groups))]


def _row_gain(gain_row):
    return jnp.broadcast_to(gain_row, (LANES, gain_row.shape[1])).T


def _scale_rows(w, gain_row):
    g = _row_gain(gain_row)
    return jnp.concatenate([w[:, _lanes(c)] * g for c in range(w.shape[1] // LANES)], axis=1)


def _cast_ffn_kernel(w_up_ref, w_down_ref, gain_ref, up_out_ref, down_out_ref):
    up_out_ref[0] = _scale_rows(w_up_ref[0], gain_ref[...]).astype(BF16)
    down_out_ref[...] = w_down_ref[0].astype(BF16)


def _prepare_mixer_kernel(*refs):
    (w_in_ref, w_out_ref, in_gain_ref, lru_gain_ref, pool_gain_ref, gate_a_ref, gate_x_ref,
     pool_w_ref), refs = refs[:8], refs[8:]
    n_vec = len(VEC_SOURCES)
    vec_refs, (w_in_out, w_out_out, w_gate_out, w_pool_out, vec_out) = refs[:n_vec], refs[n_vec:]
    w_in_out[...] = _scale_rows(w_in_ref[0], in_gain_ref[...]).astype(BF16)
    lru_rows = pl.program_id(0) < D_LRU // w_out_out.shape[0]
    out_gain = jnp.where(lru_rows, lru_gain_ref[...], pool_gain_ref[...])
    w_out_out[...] = _scale_rows(w_out_ref[0], out_gain).astype(BF16)

    @pl.when(pl.program_id(0) == 0)
    def _small_weights():
        heads = MXU_DIM // LRU_HEAD
        w_gate_out[...] = jnp.zeros_like(w_gate_out)
        for q in range(D_LRU // MXU_DIM):
            for p in range(heads):
                rows_ = slice(p * LRU_HEAD, (p + 1) * LRU_HEAD)
                for half, src_ref in enumerate((gate_a_ref, gate_x_ref)):
                    c0 = half * MXU_DIM + p * LRU_HEAD
                    w_gate_out[q, rows_, c0:c0 + LRU_HEAD] = src_ref[0, q * heads + p].astype(BF16)
        groups = MXU_DIM // POOL_GROUP
        w_pool_out[...] = jnp.zeros_like(w_pool_out)
        for q in range(D_POOL // MXU_DIM):
            for p in range(groups):
                blk = slice(p * POOL_GROUP, (p + 1) * POOL_GROUP)
                w_pool_out[q, blk, blk] = pool_w_ref[0, q * groups + p].astype(BF16)
        vec_out[...] = jnp.zeros_like(vec_out)
        for (name, pick), ref in zip(VEC_SOURCES, vec_refs):
            row, c0, width = VEC_LAYOUT[name]
            vec_out[row:row + 1, c0:c0 + width] = pick(ref)


VEC_SOURCES = (
    [(f"ffn_w{k}", lambda r, k=k: r[k]) for k in range(CONV_FFN)]
    + [(f"conv_w{k}", lambda r, k=k: r[0, k:k + 1, :]) for k in range(CONV_LRU)]
    + [(name, lambda r: r[...]) for name in (
        "ffn_b", "final_g", "conv_b", "ba", "bx", "lam", "pool_b", "pool_scale")])


def kernel(x_prompt, x_sample, state_lru_h, state_lru_conv, state_pool, state_ffn_conv, meta_tokens, norm_mix_g, w_in, conv_lru_w, conv_lru_b, gate_a_w, gate_a_b, gate_x_w, gate_x_b, lru_lambda, pool_w, pool_b, pool_scale, gn_lru, gn_pool, w_out, norm_ffn_g, w_up, ffn_conv_w, ffn_conv_b, w_down, final_norm_g):
    assert x_prompt.shape[0] == 1 and norm_mix_g.shape[0] == 1
    n_sample, sample_rows = x_sample.shape[:2]
    assert sample_rows == N_META

    n_blocks = 2 * FFN_CHUNKS
    down_rows = D_FF // n_blocks
    w_up_blocks, w_down_bf = pl.pallas_call(
        _cast_ffn_kernel,
        grid=(n_blocks,),
        in_specs=[pl.BlockSpec((1, D_MODEL, FFN_CHUNK), lambda j: (0, 0, j)),
                  pl.BlockSpec((1, down_rows, D_MODEL), lambda j: (0, j, 0)),
                  pl.BlockSpec((1, D_MODEL), lambda j: (0, 0))],
        out_specs=[pl.BlockSpec((1, D_MODEL, FFN_CHUNK), lambda j: (j, 0, 0)),
                   pl.BlockSpec((down_rows, D_MODEL), lambda j: (j, 0))],
        out_shape=[jax.ShapeDtypeStruct((n_blocks, D_MODEL, FFN_CHUNK), BF16),
                   jax.ShapeDtypeStruct((D_FF, D_MODEL), BF16)],
        compiler_params=pltpu.CompilerParams(dimension_semantics=("arbitrary",)),
        name="cast_ffn_weights",
    )(w_up, w_down, norm_ffn_g)
    w_down_blocks = w_down_bf.reshape(FFN_CHUNKS, FFN_CHUNK, D_MODEL)

    vec_params = {
        **{f"ffn_w{k}": ffn_conv_w.transpose(1, 0, 2) for k in range(CONV_FFN)},
        **{f"conv_w{k}": conv_lru_w for k in range(CONV_LRU)},
        "ffn_b": ffn_conv_b, "final_g": final_norm_g.reshape(1, D_MODEL), "conv_b": conv_lru_b,
        "ba": gate_a_b, "bx": gate_x_b, "lam": lru_lambda, "pool_b": pool_b, "pool_scale": pool_scale}
    vec_args = [vec_params[name] for name, _ in VEC_SOURCES]
    mix_steps = 4
    mix_rows = D_MODEL // mix_steps
    group_blocks = D_LRU // mix_rows
    assert D_LRU == D_POOL and group_blocks * mix_rows == D_LRU
    d_in = w_in.shape[2]
    const = lambda shape: pl.BlockSpec(shape, lambda j, _n=len(shape): (0,) * _n)
    w_in_bf, w_out_bf, w_gate, w_pool, vectors = pl.pallas_call(
        _prepare_mixer_kernel,
        grid=(mix_steps,),
        in_specs=[pl.BlockSpec((1, mix_rows, d_in), lambda j: (0, j, 0)),
                  pl.BlockSpec((1, mix_rows, D_MODEL), lambda j: (0, j, 0)),
                  pl.BlockSpec((1, mix_rows), lambda j: (0, j)),
                  pl.BlockSpec((1, mix_rows), lambda j: (0, j % group_blocks)),
                  pl.BlockSpec((1, mix_rows), lambda j: (0, j % group_blocks)),
                  const(gate_a_w.shape), const(gate_x_w.shape), const(pool_w.shape)]
        + [const(a.shape) for a in vec_args],
        out_specs=[pl.BlockSpec((mix_rows, d_in), lambda j: (j, 0)),
                   pl.BlockSpec((mix_rows, D_MODEL), lambda j: (j, 0)),
                   const((D_LRU // MXU_DIM, MXU_DIM, 2 * MXU_DIM)),
                   const((D_POOL // MXU_DIM, MXU_DIM, MXU_DIM)),
                   const((VEC_ROWS, 2 * D_FF))],
        out_shape=[jax.ShapeDtypeStruct((D_MODEL, d_in), BF16),
                   jax.ShapeDtypeStruct((D_MODEL, D_MODEL), BF16),
                   jax.ShapeDtypeStruct((D_LRU // MXU_DIM, MXU_DIM, 2 * MXU_DIM), BF16),
                   jax.ShapeDtypeStruct((D_POOL // MXU_DIM, MXU_DIM, MXU_DIM), BF16),
                   jax.ShapeDtypeStruct((VEC_ROWS, 2 * D_FF), F32)],
        compiler_params=pltpu.CompilerParams(dimension_semantics=("arbitrary",)),
        name="prepare_mixer_weights",
    )(w_in, w_out, norm_mix_g, gn_lru, gn_pool, gate_a_w, gate_x_w, pool_w, *vec_args)
    weights = (vectors, w_in_bf, w_gate, w_pool, w_out_bf, w_up_blocks, w_down_blocks)

    y_s, (meta_state, sample_state) = _run_streams(
        (meta_tokens, x_sample),
        (state_lru_h, _rows_major(state_lru_conv), _rows_major(state_pool), state_ffn_conv), weights,
        x_sample.shape, n_streams=1 + n_sample, rows=N_META,
        start_pos=(0,) + (PAST_LEN,) * n_sample, n_fresh=1, y_skip=N_META,
        out_groups=((0, 1), (1, n_sample)), tiled=False, name="short_streams")

    y_p, (prompt_state,) = _run_streams(
        (x_prompt,), meta_state, weights, x_prompt.shape, n_streams=1, rows=PROMPT_TILE,
        start_pos=(N_META,), n_fresh=0, y_skip=0, out_groups=((0, 1),), tiled=True,
        name="prompt_stream")

    restore = lambda st: (st[0], _rows_major(st[1]), _rows_major(st[2]), st[3])
    return (y_p, y_s, *restore(prompt_state), *restore(sample_state))
```

```python
import functools

import jax
import jax.numpy as jnp
from jax import lax
from jax.experimental import pallas as pl
from jax.experimental.pallas import tpu as pltpu

D_MODEL = 1024
N_META = 16
PAST_LEN = 1024
D_LRU = 512
N_LRU_HEADS = 8
LRU_HEAD = D_LRU // N_LRU_HEADS
LRU_C = 8.0
CONV_LRU = 4
D_POOL = 512
POOL_WINDOWS = (2, 4, 8, 16)
POOL_GROUP = D_POOL // len(POOL_WINDOWS)
POOL_BUF = max(POOL_WINDOWS) - 1
D_FF = 3 * D_MODEL
CONV_FFN = 3
EPS = 1e-6

SUBLANES = 8
LANES = 128
MXU_DIM = 256
VMEM_BYTES_V7X = 64 * 1024 * 1024
VMEM_RESERVE = 6 * 1024 * 1024

LRU_HDR = SUBLANES
POOL_PAD = SUBLANES
POOL_KEEP = 2 * SUBLANES
POOL_HDR = POOL_PAD + POOL_KEEP
FFN_HDR = SUBLANES
SCAN_HDR = SUBLANES
SCAN_SHIFTS = (1, 2, 4)
FFN_CHUNK = 512
FFN_CHUNKS = D_FF // FFN_CHUNK
LRU_SLABS = D_LRU // LANES
POOL_SLABS = D_POOL // LANES
CHUNK_SLABS = FFN_CHUNK // LANES
FFN_SLABS = 2 * D_FF // LANES
PROMPT_TILE = 512

VEC_ROWS = SUBLANES
VEC_LAYOUT = {}


def _vec_layout():
    for k in range(CONV_FFN):
        VEC_LAYOUT[f"ffn_w{k}"] = (k, 0, 2 * D_FF)
    VEC_LAYOUT["ffn_b"] = (CONV_FFN, 0, 2 * D_FF)
    col = 0
    for name, width in (("final_g", D_MODEL), ("conv_b", D_LRU), ("ba", D_LRU), ("bx", D_LRU),
                        ("lam", D_LRU), ("pool_b", D_POOL), ("pool_scale", D_POOL)):
        VEC_LAYOUT[name] = (CONV_FFN + 1, col, width)
        col += width
    assert col <= 2 * D_FF
    col = 0
    for name, width in [(f"conv_w{k}", D_LRU) for k in range(CONV_LRU)]:
        VEC_LAYOUT[name] = (CONV_FFN + 2, col, width)
        col += width
    assert col <= 2 * D_FF and CONV_FFN + 2 < VEC_ROWS


_vec_layout()

F32 = jnp.float32
BF16 = jnp.bfloat16


def _rmsnorm(x, g=None):
    y = x * lax.rsqrt(jnp.mean(x * x, axis=-1, keepdims=True) + EPS)
    return y if g is None else y * g


def _dot(a, b):
    return jnp.dot(a, b, preferred_element_type=F32)


GELU_C0 = (2.0 / jnp.pi) ** 0.5
GELU_C1 = 0.044715 * GELU_C0


def _gelu(x):
    half = 0.5 * x
    return half + half * jnp.tanh(x * (GELU_C0 + GELU_C1 * (x * x)))


def _lanes(c):
    return slice(c * LANES, (c + 1) * LANES)


def _stream_kernel(*refs, n_streams, rows, start_pos, n_x, n_fresh, y_skip, out_groups, cast_ffn):
    S, T = n_streams, rows
    refs = list(refs)
    take = lambda n: [refs.pop(0) for _ in range(n)]
    x_refs = take(n_x)
    h0_ref, conv0_ref, pool0_ref, ffn0_ref = take(4)
    vec_ref, w_in_ref, w_gate_ref, w_pool_ref, w_out_ref = take(5)
    if cast_ffn:
        w_up_a_f32, w_up_b_f32, w_down_f32, up_gain_ref = take(4)
    else:
        w_up_ref, w_down_ref = take(2)
    y_ref, = take(1)
    state_out_refs = [take(4) for _ in out_groups]
    if cast_ffn:
        w_up_out, w_down_out = take(2)
    (h_carry, lru_carry, pool_carry, ffn_carry, ext_lru, ext_pool, ext_up,
     scan_a, scan_b, scan_h, pool_tmp) = refs[:11]
    if cast_ffn:
        xn_buf, acc_buf = refs[11:]
    step = pl.program_id(0)
    last = pl.num_programs(0) - 1

    def vec(name, slab=None):
        row, c0, width = VEC_LAYOUT[name]
        if slab is None:
            return vec_ref[row:row + 1, c0:c0 + width]
        return vec_ref[row:row + 1, c0 + slab * LANES:c0 + (slab + 1) * LANES]

    @pl.when(step == 0)
    def _load_state():
        lru_carry[...] = jnp.zeros_like(lru_carry)
        pool_carry[...] = jnp.zeros_like(pool_carry)
        ext_pool[:, :, :POOL_PAD, :] = jnp.zeros((S, POOL_SLABS, POOL_PAD, LANES), F32)
        pool_tmp[:, :POOL_PAD, :] = jnp.zeros((2, POOL_PAD, LANES), F32)
        ffn_carry[...] = jnp.zeros_like(ffn_carry)
        scan_a[:, :SCAN_HDR, :] = jnp.ones((LRU_SLABS, SCAN_HDR, LANES), F32)
        scan_b[:, :SCAN_HDR, :] = jnp.zeros((LRU_SLABS, SCAN_HDR, LANES), F32)
        for s in range(n_fresh):
            h_carry[s] = jnp.zeros((SUBLANES, D_LRU), F32)
        for s in range(n_fresh, S):
            p = s - n_fresh
            h_carry[s] = jnp.broadcast_to(h0_ref[0, p:p + 1, :], (SUBLANES, D_LRU))
            for c in range(LRU_SLABS):
                for k in range(CONV_LRU - 1):
                    row = LRU_HDR - (CONV_LRU - 1) + k
                    lru_carry[s, c, row:row + 1, :] = conv0_ref[0, k, p:p + 1, _lanes(c)]
            for c in range(POOL_SLABS):
                for k in range(POOL_BUF):
                    row = POOL_KEEP - POOL_BUF + k
                    pool_carry[s, c, row:row + 1, :] = pool0_ref[0, k, p:p + 1, _lanes(c)]
            for c in range(FFN_SLABS):
                ffn_carry[s, c, FFN_HDR - (CONV_FFN - 1):, :] = ffn0_ref[0, p, :, _lanes(c)]

    def mixer():
        x = jnp.concatenate([r[...].reshape(-1, D_MODEL) for r in x_refs], axis=0)

        u = _dot(_rmsnorm(x).astype(BF16), w_in_ref[...])
        u_lru = u[:, :D_LRU]
        g_lru = u[:, D_LRU:2 * D_LRU]
        u_pool = u[:, 2 * D_LRU:]

        lam = vec("lam")
        c_log_sig = LRU_C * (jnp.minimum(lam, 0.0) - jnp.log1p(jnp.exp(-jnp.abs(lam))))

        y_lru_parts = []
        m_parts = []
        for s in range(S):
            r0 = s * T
            c_slabs = []
            for c in range(LRU_SLABS):
                u_c = u_lru[r0:r0 + T, _lanes(c)]
                ext_lru[s, c, :LRU_HDR, :] = lru_carry[s, c]
                ext_lru[s, c, LRU_HDR:, :] = u_c
                lru_carry[s, c] = ext_lru[s, c, T:, :]
                acc_c = vec("conv_b", c) + u_c * vec(f"conv_w{CONV_LRU - 1}", c)
                for k in range(1, CONV_LRU):
                    acc_c = acc_c + (ext_lru[s, c, LRU_HDR - k:LRU_HDR - k + T, :]
                                     * vec(f"conv_w{CONV_LRU - 1 - k}", c))
                c_slabs.append(acc_c)
            c = jnp.concatenate(c_slabs, axis=1)

            cb = c.astype(BF16)
            gates = [_dot(cb[:, q * MXU_DIM:(q + 1) * MXU_DIM], w_gate_ref[q])
                     for q in range(D_LRU // MXU_DIM)]
            r = jax.nn.sigmoid(jnp.concatenate([g[:, :MXU_DIM] for g in gates], axis=1) + vec("ba"))
            i = jax.nn.sigmoid(jnp.concatenate([g[:, MXU_DIM:] for g in gates], axis=1) + vec("bx"))
            log_a = r * c_log_sig
            a = jnp.exp(log_a)
            one_minus_a2 = jnp.tanh(log_a) * (-1.0 - a * a)
            mult = jnp.where(one_minus_a2 > 0.0, one_minus_a2 * lax.rsqrt(one_minus_a2), 0.0)
            b = mult * (i * c)

            for d in SCAN_SHIFTS:
                a_prev, b_prev = [], []
                for cc in range(LRU_SLABS):
                    scan_a[cc, SCAN_HDR:, :] = a[:, _lanes(cc)]
                    scan_b[cc, SCAN_HDR:, :] = b[:, _lanes(cc)]
                    a_prev.append(scan_a[cc, SCAN_HDR - d:SCAN_HDR - d + T, :])
                    b_prev.append(scan_b[cc, SCAN_HDR - d:SCAN_HDR - d + T, :])
                b = a * jnp.concatenate(b_prev, axis=1) + b
                a = a * jnp.concatenate(a_prev, axis=1)
            h = h_carry[s]
            for g in range(T // SUBLANES):
                lo = g * SUBLANES
                h = a[lo:lo + SUBLANES] * h + b[lo:lo + SUBLANES]
                scan_h[r0 + lo:r0 + lo + SUBLANES, :] = h
            h_carry[s] = jnp.broadcast_to(scan_h[r0 + T - 1:r0 + T, :], (SUBLANES, D_LRU))
            y_lru_parts.append(scan_h[r0:r0 + T, :] * _gelu(g_lru[r0:r0 + T]))

            ramp_up = start_pos[s] + 1 < max(POOL_WINDOWS)
            if ramp_up:
                pos1 = (lax.broadcasted_iota(jnp.int32, (T, POOL_GROUP), 0) + (start_pos[s] + 1)).astype(F32)
            means = []
            for gi, w in enumerate(POOL_WINDOWS):
                ext_pool[s, gi, POOL_PAD:POOL_HDR, :] = pool_carry[s, gi]
                ext_pool[s, gi, POOL_HDR:, :] = u_pool[r0:r0 + T, _lanes(gi)]
                pool_carry[s, gi] = ext_pool[s, gi, T + POOL_PAD:, :]
                cur = ext_pool.at[s, gi]
                n_steps_ = w.bit_length() - 1
                for l in range(n_steps_):
                    d = 1 << l
                    lo = POOL_HDR if l == n_steps_ - 1 else POOL_PAD
                    partial = cur[lo:, :] + cur[lo - d:POOL_HDR + T - d, :]
                    if l < n_steps_ - 1:
                        cur = pool_tmp.at[l % 2]
                        cur[POOL_PAD:, :] = partial
                means.append(partial / jnp.minimum(pos1, float(w)) if ramp_up else partial * (1.0 / w))
            m_parts.append(jnp.concatenate(means, axis=1) - u_pool[r0:r0 + T])

        y_lru = y_lru_parts[0] if S == 1 else jnp.concatenate(y_lru_parts, axis=0)
        m = (m_parts[0] if S == 1 else jnp.concatenate(m_parts, axis=0)).astype(BF16)

        y_pool = jnp.concatenate(
            [_dot(m[:, q * MXU_DIM:(q + 1) * MXU_DIM], w_pool_ref[q]) for q in range(D_POOL // MXU_DIM)],
            axis=1)
        y_pool = (y_pool + vec("pool_b")) * vec("pool_scale")

        mix_in = jnp.concatenate(
            [_rmsnorm(y_lru), _rmsnorm(y_pool)], axis=1)
        x1 = x + _dot(mix_in.astype(BF16), w_out_ref[...])
        return x1

    def up_halves(j, xn, w_pair):
        return [_dot(xn, w_pair[half]) for half in range(2)]

    def gate_block(j, ups):
        halves = []
        for half in range(2):
            up = ups[half]
            parts = []
            for s in range(S):
                r0 = s * T
                slabs = []
                for c in range(CHUNK_SLABS):
                    gs = (half * FFN_CHUNKS + j) * CHUNK_SLABS + c
                    ext = ext_up.at[j % 2, half, s, c]
                    up_c = up[r0:r0 + T, _lanes(c)]
                    ext[:FFN_HDR, :] = ffn_carry[s, gs]
                    ext[FFN_HDR:, :] = up_c
                    ffn_carry[s, gs] = ext[T:, :]
                    upc = vec("ffn_b", gs) + up_c * vec(f"ffn_w{CONV_FFN - 1}", gs)
                    for k in range(1, CONV_FFN):
                        upc = upc + ext[FFN_HDR - k:FFN_HDR - k + T, :] * vec(f"ffn_w{CONV_FFN - 1 - k}", gs)
                    slabs.append(upc)
                parts.append(jnp.concatenate(slabs, axis=1))
            halves.append(parts[0] if S == 1 else jnp.concatenate(parts, axis=0))
        return (_gelu(halves[0]) * halves[1]).astype(BF16)

    def emit(rows_out):
        y = _rmsnorm(rows_out, vec("final_g"))
        y_ref[...] = y[y_skip:].reshape(y_ref.shape)

    if cast_ffn:
        @pl.when(step == 0)
        def _mix():
            x1 = mixer()
            acc_buf[...] = x1
            xn_buf[...] = _rmsnorm(x1).astype(BF16)

        for j in range(FFN_CHUNKS):
            @pl.when(step == j)
            def _block(j=j):
                gain = up_gain_ref[...]
                w_pair = [_scale_rows(w_ref[0], gain).astype(BF16) for w_ref in (w_up_a_f32, w_up_b_f32)]
                w_down_j = w_down_f32[0].astype(BF16)
                for half in range(2):
                    w_up_out[half] = w_pair[half]
                w_down_out[0] = w_down_j
                act = gate_block(j, up_halves(j, xn_buf[...], w_pair))
                acc_buf[...] += _dot(act, w_down_j)

        @pl.when(step == last)
        def _emit():
            emit(acc_buf[...])
    else:
        x1 = mixer()
        xn = _rmsnorm(x1).astype(BF16)
        w_pair = lambda j: [w_up_ref[2 * j + half] for half in range(2)]
        acc = x1
        ups = up_halves(0, xn, w_pair(0))
        for j in range(FFN_CHUNKS):
            ups_next = up_halves(j + 1, xn, w_pair(j + 1)) if j + 1 < FFN_CHUNKS else None
            acc = acc + _dot(gate_block(j, ups), w_down_ref[j])
            ups = ups_next
        emit(acc)

    @pl.when(step == last)
    def _store_state():
        for (first, count), (h_out_ref, conv_out_ref, pool_out_ref, ffn_out_ref) in zip(
                out_groups, state_out_refs):
            for p in range(count):
                s = first + p
                h_out_ref[0, p:p + 1, :] = h_carry[s, :1, :]
                for c in range(LRU_SLABS):
                    for k in range(CONV_LRU - 1):
                        row = LRU_HDR - (CONV_LRU - 1) + k
                        conv_out_ref[0, k, p:p + 1, _lanes(c)] = lru_carry[s, c, row:row + 1, :]
                for c in range(POOL_SLABS):
                    for k in range(POOL_BUF):
                        row = POOL_KEEP - POOL_BUF + k
                        pool_out_ref[0, k, p:p + 1, _lanes(c)] = pool_carry[s, c, row:row + 1, :]
                for c in range(FFN_SLABS):
                    ffn_out_ref[0, p, :, _lanes(c)] = ffn_carry[s, c, FFN_HDR - (CONV_FFN - 1):, :]


def _scratch_shapes(n_streams, rows, cast_ffn):
    S, T = n_streams, rows
    across_steps = [
        pltpu.VMEM((S * T, D_MODEL), BF16),
        pltpu.VMEM((S * T, D_MODEL), F32),
    ] if cast_ffn else []
    return [
        pltpu.VMEM((S, SUBLANES, D_LRU), F32),
        pltpu.VMEM((S, LRU_SLABS, LRU_HDR, LANES), F32),
        pltpu.VMEM((S, POOL_SLABS, POOL_KEEP, LANES), F32),
        pltpu.VMEM((S, FFN_SLABS, FFN_HDR, LANES), F32),
        pltpu.VMEM((S, LRU_SLABS, LRU_HDR + T, LANES), F32),
        pltpu.VMEM((S, POOL_SLABS, POOL_HDR + T, LANES), F32),
        pltpu.VMEM((2, 2, S, CHUNK_SLABS, FFN_HDR + T, LANES), F32),
        pltpu.VMEM((LRU_SLABS, SCAN_HDR + T, LANES), F32),
        pltpu.VMEM((LRU_SLABS, SCAN_HDR + T, LANES), F32),
        pltpu.VMEM((S * T, D_LRU), F32),
        pltpu.VMEM((2, POOL_HDR + T, LANES), F32),
    ] + across_steps


def _vmem_limit(n_streams, rows, weights, cast_ffn):
    m = n_streams * rows
    weight_bytes = sum(w.size * w.dtype.itemsize for w in weights)
    io_bytes = 2 * 2 * m * D_MODEL * 4
    scratch_bytes = 0
    for sc in _scratch_shapes(n_streams, rows, cast_ffn):
        n = jnp.dtype(sc.dtype).itemsize
        for d in sc.shape:
            n *= d
        scratch_bytes += n
    budget = VMEM_BYTES_V7X - VMEM_RESERVE
    assert weight_bytes + io_bytes + scratch_bytes < budget
    return budget


def _state_shapes(count):
    return [
        jax.ShapeDtypeStruct((1, count, D_LRU), F32),
        jax.ShapeDtypeStruct((1, CONV_LRU - 1, count, D_LRU), F32),
        jax.ShapeDtypeStruct((1, POOL_BUF, count, D_POOL), F32),
        jax.ShapeDtypeStruct((1, count, CONV_FFN - 1, 2 * D_FF), F32),
    ]


def _rows_major(a):
    return a.transpose(0, 2, 1, 3)


def _run_streams(xs, states, weights, y_shape, *, n_streams, rows, start_pos, n_fresh, y_skip,
                 out_groups, tiled, name, ffn_f32=None):
    S, T = n_streams, rows
    m = S * T
    cast_ffn = ffn_f32 is not None
    assert not (cast_ffn and tiled)
    n_steps = xs[0].shape[1] // m if tiled else (FFN_CHUNKS if cast_ffn else 1)
    const = lambda shape: pl.BlockSpec(shape, lambda i, _n=len(shape): (0,) * _n)
    whole = lambda a: pl.BlockSpec(a.shape, lambda i, _n=a.ndim: (0,) * _n,
                                   pipeline_mode=pl.Buffered(1))
    tile_spec = pl.BlockSpec((None, m, D_MODEL), lambda i: (0, i, 0))
    x_specs = [tile_spec] if tiled else [whole(a) for a in xs]
    out_shape = [jax.ShapeDtypeStruct(y_shape, F32)]
    for _, count in out_groups:
        out_shape += _state_shapes(count)
    out_specs = [tile_spec if tiled else const(y_shape)] + [const(o.shape) for o in out_shape[1:]]
    weight_specs = [whole(w) for w in weights]
    operands = [*xs, *states, *weights]
    if cast_ffn:
        w_up, w_down, up_gain = ffn_f32
        operands += [w_up, w_up, w_down, up_gain]
        weight_specs += [
            pl.BlockSpec((1, D_MODEL, FFN_CHUNK), lambda j: (0, 0, j)),
            pl.BlockSpec((1, D_MODEL, FFN_CHUNK), lambda j: (0, 0, FFN_CHUNKS + j)),
            pl.BlockSpec((1, FFN_CHUNK, D_MODEL), lambda j: (0, j, 0)),
            const(up_gain.shape)]
        out_shape += [jax.ShapeDtypeStruct((2 * FFN_CHUNKS, D_MODEL, FFN_CHUNK), BF16),
                      jax.ShapeDtypeStruct((FFN_CHUNKS, FFN_CHUNK, D_MODEL), BF16)]
        out_specs += [pl.BlockSpec((2, D_MODEL, FFN_CHUNK), lambda j: (j, 0, 0)),
                      pl.BlockSpec((1, FFN_CHUNK, D_MODEL), lambda j: (j, 0, 0))]
    kernel = functools.partial(
        _stream_kernel, n_streams=S, rows=T, start_pos=start_pos, n_x=len(xs), n_fresh=n_fresh,
        y_skip=y_skip, out_groups=out_groups, cast_ffn=cast_ffn)
    outs = pl.pallas_call(
        kernel,
        grid=(n_steps,),
        in_specs=x_specs + [whole(a) for a in states] + weight_specs,
        out_specs=out_specs,
        out_shape=out_shape,
        scratch_shapes=_scratch_shapes(S, T, cast_ffn),
        compiler_params=pltpu.CompilerParams(
            dimension_semantics=("arbitrary",),
            vmem_limit_bytes=_vmem_limit(S, T, weights, cast_ffn)),
        name=name,
    )(*operands)
    n_state = 4 * len(out_groups)
    return outs[0], [outs[1 + 4 * g:5 + 4 * g] for g in range(len(out_groups))], outs[1 + n_state:]


def _row_gain(gain_row):
    return jnp.broadcast_to(gain_row, (LANES, gain_row.shape[1])).T


def _scale_rows(w, gain_row):
    g = _row_gain(gain_row)
    return jnp.concatenate([w[:, _lanes(c)] * g for c in range(w.shape[1] // LANES)], axis=1)


def _prepare_mixer_kernel(*refs):
    (w_in_ref, w_out_ref, in_gain_ref, lru_gain_ref, pool_gain_ref, gate_a_ref, gate_x_ref,
     pool_w_ref), refs = refs[:8], refs[8:]
    n_vec = len(VEC_SOURCES)
    vec_refs, (w_in_out, w_out_out, w_gate_out, w_pool_out, vec_out) = refs[:n_vec], refs[n_vec:]
    w_in_out[...] = _scale_rows(w_in_ref[0], in_gain_ref[...]).astype(BF16)
    lru_rows = pl.program_id(0) < D_LRU // w_out_out.shape[0]
    out_gain = jnp.where(lru_rows, lru_gain_ref[...], pool_gain_ref[...])
    w_out_out[...] = _scale_rows(w_out_ref[0], out_gain).astype(BF16)

    @pl.when(pl.program_id(0) == 0)
    def _small_weights():
        heads = MXU_DIM // LRU_HEAD
        w_gate_out[...] = jnp.zeros_like(w_gate_out)
        for q in range(D_LRU // MXU_DIM):
            for p in range(heads):
                rows_ = slice(p * LRU_HEAD, (p + 1) * LRU_HEAD)
                for half, src_ref in enumerate((gate_a_ref, gate_x_ref)):
                    c0 = half * MXU_DIM + p * LRU_HEAD
                    w_gate_out[q, rows_, c0:c0 + LRU_HEAD] = src_ref[0, q * heads + p].astype(BF16)
        groups = MXU_DIM // POOL_GROUP
        w_pool_out[...] = jnp.zeros_like(w_pool_out)
        for q in range(D_POOL // MXU_DIM):
            for p in range(groups):
                blk = slice(p * POOL_GROUP, (p + 1) * POOL_GROUP)
                w_pool_out[q, blk, blk] = pool_w_ref[0, q * groups + p].astype(BF16)
        vec_out[...] = jnp.zeros_like(vec_out)
        for (name, pick), ref in zip(VEC_SOURCES, vec_refs):
            row, c0, width = VEC_LAYOUT[name]
            vec_out[row:row + 1, c0:c0 + width] = pick(ref)


VEC_SOURCES = (
    [(f"ffn_w{k}", lambda r, k=k: r[k]) for k in range(CONV_FFN)]
    + [(f"conv_w{k}", lambda r, k=k: r[0, k:k + 1, :]) for k in range(CONV_LRU)]
    + [(name, lambda r: r[...]) for name in (
        "ffn_b", "final_g", "conv_b", "ba", "bx", "lam", "pool_b", "pool_scale")])


def kernel(x_prompt, x_sample, state_lru_h, state_lru_conv, state_pool, state_ffn_conv, meta_tokens, norm_mix_g, w_in, conv_lru_w, conv_lru_b, gate_a_w, gate_a_b, gate_x_w, gate_x_b, lru_lambda, pool_w, pool_b, pool_scale, gn_lru, gn_pool, w_out, norm_ffn_g, w_up, ffn_conv_w, ffn_conv_b, w_down, final_norm_g):
    assert x_prompt.shape[0] == 1 and norm_mix_g.shape[0] == 1
    n_sample, sample_rows = x_sample.shape[:2]
    assert sample_rows == N_META

    vec_params = {
        **{f"ffn_w{k}": ffn_conv_w.transpose(1, 0, 2) for k in range(CONV_FFN)},
        **{f"conv_w{k}": conv_lru_w for k in range(CONV_LRU)},
        "ffn_b": ffn_conv_b, "final_g": final_norm_g.reshape(1, D_MODEL), "conv_b": conv_lru_b,
        "ba": gate_a_b, "bx": gate_x_b, "lam": lru_lambda, "pool_b": pool_b, "pool_scale": pool_scale}
    vec_args = [vec_params[name] for name, _ in VEC_SOURCES]
    mix_steps = 4
    mix_rows = D_MODEL // mix_steps
    group_blocks = D_LRU // mix_rows
    assert D_LRU == D_POOL and group_blocks * mix_rows == D_LRU
    d_in = w_in.shape[2]
    const = lambda shape: pl.BlockSpec(shape, lambda j, _n=len(shape): (0,) * _n)
    w_in_bf, w_out_bf, w_gate, w_pool, vectors = pl.pallas_call(
        _prepare_mixer_kernel,
        grid=(mix_steps,),
        in_specs=[pl.BlockSpec((1, mix_rows, d_in), lambda j: (0, j, 0)),
                  pl.BlockSpec((1, mix_rows, D_MODEL), lambda j: (0, j, 0)),
                  pl.BlockSpec((1, mix_rows), lambda j: (0, j)),
                  pl.BlockSpec((1, mix_rows), lambda j: (0, j % group_blocks)),
                  pl.BlockSpec((1, mix_rows), lambda j: (0, j % group_blocks)),
                  const(gate_a_w.shape), const(gate_x_w.shape), const(pool_w.shape)]
        + [const(a.shape) for a in vec_args],
        out_specs=[pl.BlockSpec((mix_rows, d_in), lambda j: (j, 0)),
                   pl.BlockSpec((mix_rows, D_MODEL), lambda j: (j, 0)),
                   const((D_LRU // MXU_DIM, MXU_DIM, 2 * MXU_DIM)),
                   const((D_POOL // MXU_DIM, MXU_DIM, MXU_DIM)),
                   const((VEC_ROWS, 2 * D_FF))],
        out_shape=[jax.ShapeDtypeStruct((D_MODEL, d_in), BF16),
                   jax.ShapeDtypeStruct((D_MODEL, D_MODEL), BF16),
                   jax.ShapeDtypeStruct((D_LRU // MXU_DIM, MXU_DIM, 2 * MXU_DIM), BF16),
                   jax.ShapeDtypeStruct((D_POOL // MXU_DIM, MXU_DIM, MXU_DIM), BF16),
                   jax.ShapeDtypeStruct((VEC_ROWS, 2 * D_FF), F32)],
        compiler_params=pltpu.CompilerParams(dimension_semantics=("arbitrary",)),
        name="prepare_mixer_weights",
    )(w_in, w_out, norm_mix_g, gn_lru, gn_pool, gate_a_w, gate_x_w, pool_w, *vec_args)
    mixer_weights = (vectors, w_in_bf, w_gate, w_pool, w_out_bf)

    y_s, (meta_state, sample_state), ffn_weights = _run_streams(
        (meta_tokens, x_sample),
        (state_lru_h, _rows_major(state_lru_conv), _rows_major(state_pool), state_ffn_conv),
        mixer_weights, x_sample.shape, n_streams=1 + n_sample, rows=N_META,
        start_pos=(0,) + (PAST_LEN,) * n_sample, n_fresh=1, y_skip=N_META,
        out_groups=((0, 1), (1, n_sample)), tiled=False, name="short_streams",
        ffn_f32=(w_up, w_down, norm_ffn_g))

    y_p, (prompt_state,), _ = _run_streams(
        (x_prompt,), meta_state, (*mixer_weights, *ffn_weights), x_prompt.shape, n_streams=1,
        rows=PROMPT_TILE, start_pos=(N_META,), n_fresh=0, y_skip=0, out_groups=((0, 1),), tiled=True,
        name="prompt_stream")

    restore = lambda st: (st[0], _rows_major(st[1]), _rows_major(st[2]), st[3])
    return (y_p, y_s, *restore(prompt_state), *restore(sample_state))
```

```python
import functools

import jax
import jax.numpy as jnp
from jax import lax
from jax.experimental import pallas as pl
from jax.experimental.pallas import tpu as pltpu

D_MODEL = 1024
N_META = 16
PAST_LEN = 1024
D_LRU = 512
N_LRU_HEADS = 8
LRU_HEAD = D_LRU // N_LRU_HEADS
LRU_C = 8.0
CONV_LRU = 4
D_POOL = 512
POOL_WINDOWS = (2, 4, 8, 16)
POOL_GROUP = D_POOL // len(POOL_WINDOWS)
POOL_BUF = max(POOL_WINDOWS) - 1
D_FF = 3 * D_MODEL
CONV_FFN = 3
EPS = 1e-6

SUBLANES = 8
LANES = 128
MXU_DIM = 256
VMEM_BYTES_V7X = 64 * 1024 * 1024
VMEM_RESERVE = 6 * 1024 * 1024

LRU_HDR = SUBLANES
POOL_PAD = SUBLANES
POOL_KEEP = 2 * SUBLANES
POOL_HDR = POOL_PAD + POOL_KEEP
FFN_HDR = SUBLANES
SCAN_HDR = SUBLANES
SCAN_SHIFTS = (1, 2, 4)
FFN_CHUNK = 512
FFN_CHUNKS = D_FF // FFN_CHUNK
LRU_SLABS = D_LRU // LANES
POOL_SLABS = D_POOL // LANES
CHUNK_SLABS = FFN_CHUNK // LANES
FFN_SLABS = 2 * D_FF // LANES
PROMPT_TILE = 512

VEC_ROWS = SUBLANES
VEC_LAYOUT = {}


def _vec_layout():
    for k in range(CONV_FFN):
        VEC_LAYOUT[f"ffn_w{k}"] = (k, 0, 2 * D_FF)
    VEC_LAYOUT["ffn_b"] = (CONV_FFN, 0, 2 * D_FF)
    col = 0
    for name, width in (("final_g", D_MODEL), ("conv_b", D_LRU), ("ba", D_LRU), ("bx", D_LRU),
                        ("lam", D_LRU), ("pool_b", D_POOL), ("pool_scale", D_POOL)):
        VEC_LAYOUT[name] = (CONV_FFN + 1, col, width)
        col += width
    assert col <= 2 * D_FF
    col = 0
    for name, width in [(f"conv_w{k}", D_LRU) for k in range(CONV_LRU)]:
        VEC_LAYOUT[name] = (CONV_FFN + 2, col, width)
        col += width
    assert col <= 2 * D_FF and CONV_FFN + 2 < VEC_ROWS


_vec_layout()

F32 = jnp.float32
BF16 = jnp.bfloat16


def _rmsnorm(x, g=None):
    y = x * lax.rsqrt(jnp.mean(x * x, axis=-1, keepdims=True) + EPS)
    return y if g is None else y * g


def _dot(a, b):
    return jnp.dot(a, b, preferred_element_type=F32)


GELU_C0 = (2.0 / jnp.pi) ** 0.5
GELU_C1 = 0.044715 * GELU_C0


def _gelu(x):
    half = 0.5 * x
    return half + half * jnp.tanh(x * (GELU_C0 + GELU_C1 * (x * x)))


def _lanes(c):
    return slice(c * LANES, (c + 1) * LANES)


def _stream_kernel(*refs, n_streams, rows, start_pos, n_x, n_fresh, y_skip, out_groups, cast_ffn):
    S, T = n_streams, rows
    refs = list(refs)
    take = lambda n: [refs.pop(0) for _ in range(n)]
    x_refs = take(n_x)
    h0_ref, conv0_ref, pool0_ref, ffn0_ref = take(4)
    vec_ref, w_in_ref, w_gate_ref, w_pool_ref, w_out_ref = take(5)
    if cast_ffn:
        w_up_a_f32, w_up_b_f32, w_down_f32, up_gain_ref = take(4)
    else:
        w_up_ref, w_down_ref = take(2)
    y_ref, = take(1)
    state_out_refs = [take(4) for _ in out_groups]
    if cast_ffn:
        w_up_out, w_down_out = take(2)
    (h_carry, lru_carry, pool_carry, ffn_carry, ext_lru, ext_pool, ext_up,
     scan_a, scan_b, scan_h, pool_tmp) = refs[:11]
    if cast_ffn:
        xn_buf, acc_buf = refs[11:]
    step = pl.program_id(0)
    last = pl.num_programs(0) - 1

    def vec(name, slab=None):
        row, c0, width = VEC_LAYOUT[name]
        if slab is None:
            return vec_ref[row:row + 1, c0:c0 + width]
        return vec_ref[row:row + 1, c0 + slab * LANES:c0 + (slab + 1) * LANES]

    @pl.when(step == 0)
    def _load_state():
        lru_carry[...] = jnp.zeros_like(lru_carry)
        pool_carry[...] = jnp.zeros_like(pool_carry)
        ext_pool[:, :, :POOL_PAD, :] = jnp.zeros((S, POOL_SLABS, POOL_PAD, LANES), F32)
        pool_tmp[:, :POOL_PAD, :] = jnp.zeros((2, POOL_PAD, LANES), F32)
        ffn_carry[...] = jnp.zeros_like(ffn_carry)
        scan_a[:, :SCAN_HDR, :] = jnp.ones((LRU_SLABS, SCAN_HDR, LANES), F32)
        scan_b[:, :SCAN_HDR, :] = jnp.zeros((LRU_SLABS, SCAN_HDR, LANES), F32)
        for s in range(n_fresh):
            h_carry[s] = jnp.zeros((SUBLANES, D_LRU), F32)
        for s in range(n_fresh, S):
            p = s - n_fresh
            h_carry[s] = jnp.broadcast_to(h0_ref[0, p:p + 1, :], (SUBLANES, D_LRU))
            for c in range(LRU_SLABS):
                for k in range(CONV_LRU - 1):
                    row = LRU_HDR - (CONV_LRU - 1) + k
                    lru_carry[s, c, row:row + 1, :] = conv0_ref[0, k, p:p + 1, _lanes(c)]
            for c in range(POOL_SLABS):
                for k in range(POOL_BUF):
                    row = POOL_KEEP - POOL_BUF + k
                    pool_carry[s, c, row:row + 1, :] = pool0_ref[0, k, p:p + 1, _lanes(c)]
            for c in range(FFN_SLABS):
                ffn_carry[s, c, FFN_HDR - (CONV_FFN - 1):, :] = ffn0_ref[0, p, :, _lanes(c)]

    def mixer():
        x = jnp.concatenate([r[...].reshape(-1, D_MODEL) for r in x_refs], axis=0)

        u = _dot(_rmsnorm(x).astype(BF16), w_in_ref[...])
        u_lru = u[:, :D_LRU]
        g_lru = u[:, D_LRU:2 * D_LRU]
        u_pool = u[:, 2 * D_LRU:]

        lam = vec("lam")
        c_log_sig = LRU_C * (jnp.minimum(lam, 0.0) - jnp.log1p(jnp.exp(-jnp.abs(lam))))

        y_lru_parts = []
        m_parts = []
        for s in range(S):
            r0 = s * T
            c_slabs = []
            for c in range(LRU_SLABS):
                u_c = u_lru[r0:r0 + T, _lanes(c)]
                ext_lru[s, c, :LRU_HDR, :] = lru_carry[s, c]
                ext_lru[s, c, LRU_HDR:, :] = u_c
                lru_carry[s, c] = ext_lru[s, c, T:, :]
                acc_c = vec("conv_b", c) + u_c * vec(f"conv_w{CONV_LRU - 1}", c)
                for k in range(1, CONV_LRU):
                    acc_c = acc_c + (ext_lru[s, c, LRU_HDR - k:LRU_HDR - k + T, :]
                                     * vec(f"conv_w{CONV_LRU - 1 - k}", c))
                c_slabs.append(acc_c)
            c = jnp.concatenate(c_slabs, axis=1)

            cb = c.astype(BF16)
            gates = [_dot(cb[:, q * MXU_DIM:(q + 1) * MXU_DIM], w_gate_ref[q])
                     for q in range(D_LRU // MXU_DIM)]
            r = jax.nn.sigmoid(jnp.concatenate([g[:, :MXU_DIM] for g in gates], axis=1) + vec("ba"))
            i = jax.nn.sigmoid(jnp.concatenate([g[:, MXU_DIM:] for g in gates], axis=1) + vec("bx"))
            log_a = r * c_log_sig
            a = jnp.exp(log_a)
            one_minus_a2 = jnp.tanh(log_a) * (-1.0 - a * a)
            mult = jnp.where(one_minus_a2 > 0.0, one_minus_a2 * lax.rsqrt(one_minus_a2), 0.0)
            b = mult * (i * c)

            for d in SCAN_SHIFTS:
                a_prev, b_prev = [], []
                for cc in range(LRU_SLABS):
                    scan_a[cc, SCAN_HDR:, :] = a[:, _lanes(cc)]
                    scan_b[cc, SCAN_HDR:, :] = b[:, _lanes(cc)]
                    a_prev.append(scan_a[cc, SCAN_HDR - d:SCAN_HDR - d + T, :])
                    b_prev.append(scan_b[cc, SCAN_HDR - d:SCAN_HDR - d + T, :])
                b = a * jnp.concatenate(b_prev, axis=1) + b
                a = a * jnp.concatenate(a_prev, axis=1)
            h = h_carry[s]
            for g in range(T // SUBLANES):
                lo = g * SUBLANES
                h = a[lo:lo + SUBLANES] * h + b[lo:lo + SUBLANES]
                scan_h[r0 + lo:r0 + lo + SUBLANES, :] = h
            h_carry[s] = jnp.broadcast_to(scan_h[r0 + T - 1:r0 + T, :], (SUBLANES, D_LRU))
            y_lru_parts.append(scan_h[r0:r0 + T, :] * _gelu(g_lru[r0:r0 + T]))

            ramp_up = start_pos[s] + 1 < max(POOL_WINDOWS)
            if ramp_up:
                pos1 = (lax.broadcasted_iota(jnp.int32, (T, POOL_GROUP), 0) + (start_pos[s] + 1)).astype(F32)
            means = []
            for gi, w in enumerate(POOL_WINDOWS):
                ext_pool[s, gi, POOL_PAD:POOL_HDR, :] = pool_carry[s, gi]
                ext_pool[s, gi, POOL_HDR:, :] = u_pool[r0:r0 + T, _lanes(gi)]
                pool_carry[s, gi] = ext_pool[s, gi, T + POOL_PAD:, :]
                cur = ext_pool.at[s, gi]
                n_steps_ = w.bit_length() - 1
                for l in range(n_steps_):
                    d = 1 << l
                    lo = POOL_HDR if l == n_steps_ - 1 else POOL_PAD
                    partial = cur[lo:, :] + cur[lo - d:POOL_HDR + T - d, :]
                    if l < n_steps_ - 1:
                        cur = pool_tmp.at[l % 2]
                        cur[POOL_PAD:, :] = partial
                means.append(partial / jnp.minimum(pos1, float(w)) if ramp_up else partial * (1.0 / w))
            m_parts.append(jnp.concatenate(means, axis=1) - u_pool[r0:r0 + T])

        y_lru = y_lru_parts[0] if S == 1 else jnp.concatenate(y_lru_parts, axis=0)
        m = (m_parts[0] if S == 1 else jnp.concatenate(m_parts, axis=0)).astype(BF16)

        y_pool = jnp.concatenate(
            [_dot(m[:, q * MXU_DIM:(q + 1) * MXU_DIM], w_pool_ref[q]) for q in range(D_POOL // MXU_DIM)],
            axis=1)
        y_pool = (y_pool + vec("pool_b")) * vec("pool_scale")

        mix_in = jnp.concatenate(
            [_rmsnorm(y_lru), _rmsnorm(y_pool)], axis=1)
        x1 = x + _dot(mix_in.astype(BF16), w_out_ref[...])
        return x1

    def up_halves(j, xn, w_pair):
        return [_dot(xn, w_pair[half]) for half in range(2)]

    def gate_block(j, ups):
        halves = []
        for half in range(2):
            up = ups[half]
            parts = []
            for s in range(S):
                r0 = s * T
                slabs = []
                for c in range(CHUNK_SLABS):
                    gs = (half * FFN_CHUNKS + j) * CHUNK_SLABS + c
                    ext = ext_up.at[j % 2, half, s, c]
                    up_c = up[r0:r0 + T, _lanes(c)]
                    ext[:FFN_HDR, :] = ffn_carry[s, gs]
                    ext[FFN_HDR:, :] = up_c
                    ffn_carry[s, gs] = ext[T:, :]
                    upc = vec("ffn_b", gs) + up_c * vec(f"ffn_w{CONV_FFN - 1}", gs)
                    for k in range(1, CONV_FFN):
                        upc = upc + ext[FFN_HDR - k:FFN_HDR - k + T, :] * vec(f"ffn_w{CONV_FFN - 1 - k}", gs)
                    slabs.append(upc)
                parts.append(jnp.concatenate(slabs, axis=1))
            halves.append(parts[0] if S == 1 else jnp.concatenate(parts, axis=0))
        return (_gelu(halves[0]) * halves[1]).astype(BF16)

    def emit(rows_out):
        y = _rmsnorm(rows_out, vec("final_g"))
        y_ref[...] = y[y_skip:].reshape(y_ref.shape)

    if cast_ffn:
        @pl.when(step == 0)
        def _mix():
            x1 = mixer()
            acc_buf[...] = x1
            xn_buf[...] = _rmsnorm(x1).astype(BF16)

        for j in range(FFN_CHUNKS):
            @pl.when(step == j)
            def _block(j=j):
                gain = up_gain_ref[...]
                w_pair = [_scale_rows(w_ref[0], gain).astype(BF16) for w_ref in (w_up_a_f32, w_up_b_f32)]
                w_down_j = w_down_f32[0].astype(BF16)
                for half in range(2):
                    w_up_out[half] = w_pair[half]
                w_down_out[0] = w_down_j
                act = gate_block(j, up_halves(j, xn_buf[...], w_pair))
                acc_buf[...] += _dot(act, w_down_j)

        @pl.when(step == last)
        def _emit():
            emit(acc_buf[...])
    else:
        x1 = mixer()
        xn = _rmsnorm(x1).astype(BF16)
        w_pair = lambda j: [w_up_ref[2 * j + half] for half in range(2)]
        acc = jnp.zeros((S * T, D_MODEL), F32)
        ups = up_halves(0, xn, w_pair(0))
        for j in range(FFN_CHUNKS):
            ups_next = up_halves(j + 1, xn, w_pair(j + 1)) if j + 1 < FFN_CHUNKS else None
            acc = acc + _dot(gate_block(j, ups), w_down_ref[j])
            ups = ups_next
        emit(x1 + acc)

    @pl.when(step == last)
    def _store_state():
        for (first, count), (h_out_ref, conv_out_ref, pool_out_ref, ffn_out_ref) in zip(
                out_groups, state_out_refs):
            for p in range(count):
                s = first + p
                h_out_ref[0, p:p + 1, :] = h_carry[s, :1, :]
                for c in range(LRU_SLABS):
                    for k in range(CONV_LRU - 1):
                        row = LRU_HDR - (CONV_LRU - 1) + k
                        conv_out_ref[0, k, p:p + 1, _lanes(c)] = lru_carry[s, c, row:row + 1, :]
                for c in range(POOL_SLABS):
                    for k in range(POOL_BUF):
                        row = POOL_KEEP - POOL_BUF + k
                        pool_out_ref[0, k, p:p + 1, _lanes(c)] = pool_carry[s, c, row:row + 1, :]
                for c in range(FFN_SLABS):
                    ffn_out_ref[0, p, :, _lanes(c)] = ffn_carry[s, c, FFN_HDR - (CONV_FFN - 1):, :]


def _scratch_shapes(n_streams, rows, cast_ffn):
    S, T = n_streams, rows
    across_steps = [
        pltpu.VMEM((S * T, D_MODEL), BF16),
        pltpu.VMEM((S * T, D_MODEL), F32),
    ] if cast_ffn else []
    return [
        pltpu.VMEM((S, SUBLANES, D_LRU), F32),
        pltpu.VMEM((S, LRU_SLABS, LRU_HDR, LANES), F32),
        pltpu.VMEM((S, POOL_SLABS, POOL_KEEP, LANES), F32),
        pltpu.VMEM((S, FFN_SLABS, FFN_HDR, LANES), F32),
        pltpu.VMEM((S, LRU_SLABS, LRU_HDR + T, LANES), F32),
        pltpu.VMEM((S, POOL_SLABS, POOL_HDR + T, LANES), F32),
        pltpu.VMEM((2, 2, S, CHUNK_SLABS, FFN_HDR + T, LANES), F32),
        pltpu.VMEM((LRU_SLABS, SCAN_HDR + T, LANES), F32),
        pltpu.VMEM((LRU_SLABS, SCAN_HDR + T, LANES), F32),
        pltpu.VMEM((S * T, D_LRU), F32),
        pltpu.VMEM((2, POOL_HDR + T, LANES), F32),
    ] + across_steps


def _vmem_limit(n_streams, rows, weights, cast_ffn):
    m = n_streams * rows
    weight_bytes = sum(w.size * w.dtype.itemsize for w in weights)
    io_bytes = 2 * 2 * m * D_MODEL * 4
    scratch_bytes = 0
    for sc in _scratch_shapes(n_streams, rows, cast_ffn):
        n = jnp.dtype(sc.dtype).itemsize
        for d in sc.shape:
            n *= d
        scratch_bytes += n
    budget = VMEM_BYTES_V7X - VMEM_RESERVE
    assert weight_bytes + io_bytes + scratch_bytes < budget
    return budget


def _state_shapes(count):
    return [
        jax.ShapeDtypeStruct((1, count, D_LRU), F32),
        jax.ShapeDtypeStruct((1, CONV_LRU - 1, count, D_LRU), F32),
        jax.ShapeDtypeStruct((1, POOL_BUF, count, D_POOL), F32),
        jax.ShapeDtypeStruct((1, count, CONV_FFN - 1, 2 * D_FF), F32),
    ]


def _rows_major(a):
    return a.transpose(0, 2, 1, 3)


def _run_streams(xs, states, weights, y_shape, *, n_streams, rows, start_pos, n_fresh, y_skip,
                 out_groups, tiled, name, ffn_f32=None):
    S, T = n_streams, rows
    m = S * T
    cast_ffn = ffn_f32 is not None
    assert not (cast_ffn and tiled)
    n_steps = xs[0].shape[1] // m if tiled else (FFN_CHUNKS if cast_ffn else 1)
    const = lambda shape: pl.BlockSpec(shape, lambda i, _n=len(shape): (0,) * _n)
    whole = lambda a: pl.BlockSpec(a.shape, lambda i, _n=a.ndim: (0,) * _n,
                                   pipeline_mode=pl.Buffered(1))
    tile_spec = pl.BlockSpec((None, m, D_MODEL), lambda i: (0, i, 0))
    x_specs = [tile_spec] if tiled else [whole(a) for a in xs]
    out_shape = [jax.ShapeDtypeStruct(y_shape, F32)]
    for _, count in out_groups:
        out_shape += _state_shapes(count)
    out_specs = [tile_spec if tiled else const(y_shape)] + [const(o.shape) for o in out_shape[1:]]
    weight_specs = [whole(w) for w in weights]
    operands = [*xs, *states, *weights]
    if cast_ffn:
        w_up, w_down, up_gain = ffn_f32
        operands += [w_up, w_up, w_down, up_gain]
        weight_specs += [
            pl.BlockSpec((1, D_MODEL, FFN_CHUNK), lambda j: (0, 0, j)),
            pl.BlockSpec((1, D_MODEL, FFN_CHUNK), lambda j: (0, 0, FFN_CHUNKS + j)),
            pl.BlockSpec((1, FFN_CHUNK, D_MODEL), lambda j: (0, j, 0)),
            const(up_gain.shape)]
        out_shape += [jax.ShapeDtypeStruct((2 * FFN_CHUNKS, D_MODEL, FFN_CHUNK), BF16),
                      jax.ShapeDtypeStruct((FFN_CHUNKS, FFN_CHUNK, D_MODEL), BF16)]
        out_specs += [pl.BlockSpec((2, D_MODEL, FFN_CHUNK), lambda j: (j, 0, 0)),
                      pl.BlockSpec((1, FFN_CHUNK, D_MODEL), lambda j: (j, 0, 0))]
    kernel = functools.partial(
        _stream_kernel, n_streams=S, rows=T, start_pos=start_pos, n_x=len(xs), n_fresh=n_fresh,
        y_skip=y_skip, out_groups=out_groups, cast_ffn=cast_ffn)
    outs = pl.pallas_call(
        kernel,
        grid=(n_steps,),
        in_specs=x_specs + [whole(a) for a in states] + weight_specs,
        out_specs=out_specs,
        out_shape=out_shape,
        scratch_shapes=_scratch_shapes(S, T, cast_ffn),
        compiler_params=pltpu.CompilerParams(
            dimension_semantics=("arbitrary",),
            vmem_limit_bytes=_vmem_limit(S, T, weights, cast_ffn)),
        name=name,
    )(*operands)
    n_state = 4 * len(out_groups)
    return outs[0], [outs[1 + 4 * g:5 + 4 * g] for g in range(len(out_groups))], outs[1 + n_state:]


def _row_gain(gain_row):
    return jnp.broadcast_to(gain_row, (LANES, gain_row.shape[1])).T


def _scale_rows(w, gain_row):
    g = _row_gain(gain_row)
    return jnp.concatenate([w[:, _lanes(c)] * g for c in range(w.shape[1] // LANES)], axis=1)


def _prepare_mixer_kernel(*refs):
    (w_in_ref, w_out_ref, in_gain_ref, lru_gain_ref, pool_gain_ref, gate_a_ref, gate_x_ref,
     pool_w_ref), refs = refs[:8], refs[8:]
    n_vec = len(VEC_SOURCES)
    vec_refs, (w_in_out, w_out_out, w_gate_out, w_pool_out, vec_out) = refs[:n_vec], refs[n_vec:]
    w_in_out[...] = _scale_rows(w_in_ref[0], in_gain_ref[...]).astype(BF16)
    lru_rows = pl.program_id(0) < D_LRU // w_out_out.shape[0]
    out_gain = jnp.where(lru_rows, lru_gain_ref[...], pool_gain_ref[...])
    w_out_out[...] = _scale_rows(w_out_ref[0], out_gain).astype(BF16)

    @pl.when(pl.program_id(0) == 0)
    def _small_weights():
        heads = MXU_DIM // LRU_HEAD
        w_gate_out[...] = jnp.zeros_like(w_gate_out)
        for q in range(D_LRU // MXU_DIM):
            for p in range(heads):
                rows_ = slice(p * LRU_HEAD, (p + 1) * LRU_HEAD)
                for half, src_ref in enumerate((gate_a_ref, gate_x_ref)):
                    c0 = half * MXU_DIM + p * LRU_HEAD
                    w_gate_out[q, rows_, c0:c0 + LRU_HEAD] = src_ref[0, q * heads + p].astype(BF16)
        groups = MXU_DIM // POOL_GROUP
        w_pool_out[...] = jnp.zeros_like(w_pool_out)
        for q in range(D_POOL // MXU_DIM):
            for p in range(groups):
                blk = slice(p * POOL_GROUP, (p + 1) * POOL_GROUP)
                w_pool_out[q, blk, blk] = pool_w_ref[0, q * groups + p].astype(BF16)
        vec_out[...] = jnp.zeros_like(vec_out)
        for (name, pick), ref in zip(VEC_SOURCES, vec_refs):
            row, c0, width = VEC_LAYOUT[name]
            vec_out[row:row + 1, c0:c0 + width] = pick(ref)


VEC_SOURCES = (
    [(f"ffn_w{k}", lambda r, k=k: r[k]) for k in range(CONV_FFN)]
    + [(f"conv_w{k}", lambda r, k=k: r[0, k:k + 1, :]) for k in range(CONV_LRU)]
    + [(name, lambda r: r[...]) for name in (
        "ffn_b", "final_g", "conv_b", "ba", "bx", "lam", "pool_b", "pool_scale")])


def kernel(x_prompt, x_sample, state_lru_h, state_lru_conv, state_pool, state_ffn_conv, meta_tokens, norm_mix_g, w_in, conv_lru_w, conv_lru_b, gate_a_w, gate_a_b, gate_x_w, gate_x_b, lru_lambda, pool_w, pool_b, pool_scale, gn_lru, gn_pool, w_out, norm_ffn_g, w_up, ffn_conv_w, ffn_conv_b, w_down, final_norm_g):
    assert x_prompt.shape[0] == 1 and norm_mix_g.shape[0] == 1
    n_sample, sample_rows = x_sample.shape[:2]
    assert sample_rows == N_META

    vec_params = {
        **{f"ffn_w{k}": ffn_conv_w.transpose(1, 0, 2) for k in range(CONV_FFN)},
        **{f"conv_w{k}": conv_lru_w for k in range(CONV_LRU)},
        "ffn_b": ffn_conv_b, "final_g": final_norm_g.reshape(1, D_MODEL), "conv_b": conv_lru_b,
        "ba": gate_a_b, "bx": gate_x_b, "lam": lru_lambda, "pool_b": pool_b, "pool_scale": pool_scale}
    vec_args = [vec_params[name] for name, _ in VEC_SOURCES]
    mix_steps = 4
    mix_rows = D_MODEL // mix_steps
    group_blocks = D_LRU // mix_rows
    assert D_LRU == D_POOL and group_blocks * mix_rows == D_LRU
    d_in = w_in.shape[2]
    const = lambda shape: pl.BlockSpec(shape, lambda j, _n=len(shape): (0,) * _n)
    w_in_bf, w_out_bf, w_gate, w_pool, vectors = pl.pallas_call(
        _prepare_mixer_kernel,
        grid=(mix_steps,),
        in_specs=[pl.BlockSpec((1, mix_rows, d_in), lambda j: (0, j, 0)),
                  pl.BlockSpec((1, mix_rows, D_MODEL), lambda j: (0, j, 0)),
                  pl.BlockSpec((1, mix_rows), lambda j: (0, j)),
                  pl.BlockSpec((1, mix_rows), lambda j: (0, j % group_blocks)),
                  pl.BlockSpec((1, mix_rows), lambda j: (0, j % group_blocks)),
                  const(gate_a_w.shape), const(gate_x_w.shape), const(pool_w.shape)]
        + [const(a.shape) for a in vec_args],
        out_specs=[pl.BlockSpec((mix_rows, d_in), lambda j: (j, 0)),
                   pl.BlockSpec((mix_rows, D_MODEL), lambda j: (j, 0)),
                   const((D_LRU // MXU_DIM, MXU_DIM, 2 * MXU_DIM)),
                   const((D_POOL // MXU_DIM, MXU_DIM, MXU_DIM)),
                   const((VEC_ROWS, 2 * D_FF))],
        out_shape=[jax.ShapeDtypeStruct((D_MODEL, d_in), BF16),
                   jax.ShapeDtypeStruct((D_MODEL, D_MODEL), BF16),
                   jax.ShapeDtypeStruct((D_LRU // MXU_DIM, MXU_DIM, 2 * MXU_DIM), BF16),
                   jax.ShapeDtypeStruct((D_POOL // MXU_DIM, MXU_DIM, MXU_DIM), BF16),
                   jax.ShapeDtypeStruct((VEC_ROWS, 2 * D_FF), F32)],
        compiler_params=pltpu.CompilerParams(dimension_semantics=("arbitrary",)),
        name="prepare_mixer_weights",
    )(w_in, w_out, norm_mix_g, gn_lru, gn_pool, gate_a_w, gate_x_w, pool_w, *vec_args)
    mixer_weights = (vectors, w_in_bf, w_gate, w_pool, w_out_bf)

    y_s, (meta_state, sample_state), ffn_weights = _run_streams(
        (meta_tokens, x_sample),
        (state_lru_h, _rows_major(state_lru_conv), _rows_major(state_pool), state_ffn_conv),
        mixer_weights, x_sample.shape, n_streams=1 + n_sample, rows=N_META,
        start_pos=(0,) + (PAST_LEN,) * n_sample, n_fresh=1, y_skip=N_META,
        out_groups=((0, 1), (1, n_sample)), tiled=False, name="short_streams",
        ffn_f32=(w_up, w_down, norm_ffn_g))

    y_p, (prompt_state,), _ = _run_streams(
        (x_prompt,), meta_state, (*mixer_weights, *ffn_weights), x_prompt.shape, n_streams=1,
        rows=PROMPT_TILE, start_pos=(N_META,), n_fresh=0, y_skip=0, out_groups=((0, 1),), tiled=True,
        name="prompt_stream")

    restore = lambda st: (st[0], _rows_major(st[1]), _rows_major(st[2]), st[3])
    return (y_p, y_s, *restore(prompt_state), *restore(sample_state))
```

```python
import functools

import jax
import jax.numpy as jnp
from jax import lax
from jax.experimental import pallas as pl
from jax.experimental.pallas import tpu as pltpu

D_MODEL = 1024
N_META = 16
PAST_LEN = 1024
D_LRU = 512
N_LRU_HEADS = 8
LRU_HEAD = D_LRU // N_LRU_HEADS
LRU_C = 8.0
CONV_LRU = 4
D_POOL = 512
POOL_WINDOWS = (2, 4, 8, 16)
POOL_GROUP = D_POOL // len(POOL_WINDOWS)
POOL_BUF = max(POOL_WINDOWS) - 1
D_FF = 3 * D_MODEL
CONV_FFN = 3
EPS = 1e-6

SUBLANES = 8
LANES = 128
MXU_DIM = 256
VMEM_BYTES_V7X = 64 * 1024 * 1024
VMEM_RESERVE = 6 * 1024 * 1024

LRU_HDR = SUBLANES
POOL_PAD = SUBLANES
POOL_KEEP = 2 * SUBLANES
POOL_HDR = POOL_PAD + POOL_KEEP
FFN_HDR = SUBLANES
SCAN_HDR = SUBLANES
SCAN_SHIFTS = (1, 2, 4)
FFN_CHUNK = 512
FFN_CHUNKS = D_FF // FFN_CHUNK
LRU_SLABS = D_LRU // LANES
POOL_SLABS = D_POOL // LANES
CHUNK_SLABS = FFN_CHUNK // LANES
FFN_SLABS = 2 * D_FF // LANES
PROMPT_TILE = 512
TILES_PER_STEP = 2

VEC_ROWS = SUBLANES
VEC_LAYOUT = {}


def _vec_layout():
    for k in range(CONV_FFN):
        VEC_LAYOUT[f"ffn_w{k}"] = (k, 0, 2 * D_FF)
    VEC_LAYOUT["ffn_b"] = (CONV_FFN, 0, 2 * D_FF)
    col = 0
    for name, width in (("final_g", D_MODEL), ("conv_b", D_LRU), ("ba", D_LRU), ("bx", D_LRU),
                        ("lam", D_LRU), ("pool_b", D_POOL), ("pool_scale", D_POOL)):
        VEC_LAYOUT[name] = (CONV_FFN + 1, col, width)
        col += width
    assert col <= 2 * D_FF
    col = 0
    for name, width in [(f"conv_w{k}", D_LRU) for k in range(CONV_LRU)]:
        VEC_LAYOUT[name] = (CONV_FFN + 2, col, width)
        col += width
    assert col <= 2 * D_FF and CONV_FFN + 2 < VEC_ROWS


_vec_layout()

F32 = jnp.float32
BF16 = jnp.bfloat16


def _rmsnorm(x, g=None):
    y = x * lax.rsqrt(jnp.mean(x * x, axis=-1, keepdims=True) + EPS)
    return y if g is None else y * g


def _dot(a, b):
    return jnp.dot(a, b, preferred_element_type=F32)


GELU_C0 = (2.0 / jnp.pi) ** 0.5
GELU_C1 = 0.044715 * GELU_C0


def _gelu(x):
    half = 0.5 * x
    return half + half * jnp.tanh(x * (GELU_C0 + GELU_C1 * (x * x)))


def _lanes(c):
    return slice(c * LANES, (c + 1) * LANES)


def _stream_kernel(*refs, n_streams, rows, start_pos, n_x, n_fresh, y_skip, out_groups, cast_ffn,
                   tiles_per_step):
    S, T = n_streams, rows
    refs = list(refs)
    take = lambda n: [refs.pop(0) for _ in range(n)]
    x_refs = take(n_x)
    h0_ref, conv0_ref, pool0_ref, ffn0_ref = take(4)
    vec_ref, w_in_ref, w_gate_ref, w_pool_ref, w_out_ref = take(5)
    if cast_ffn:
        w_up_a_f32, w_up_b_f32, w_down_f32, up_gain_ref = take(4)
    else:
        w_up_ref, w_down_ref = take(2)
    y_ref, = take(1)
    state_out_refs = [take(4) for _ in out_groups]
    if cast_ffn:
        w_up_out, w_down_out = take(2)
    (h_carry, lru_carry, pool_carry, ffn_carry, ext_lru, ext_pool, ext_up,
     scan_a, scan_b, scan_h, pool_tmp) = refs[:11]
    if cast_ffn:
        xn_buf, acc_buf = refs[11:]
    step = pl.program_id(0)
    last = pl.num_programs(0) - 1

    def vec(name, slab=None):
        row, c0, width = VEC_LAYOUT[name]
        if slab is None:
            return vec_ref[row:row + 1, c0:c0 + width]
        return vec_ref[row:row + 1, c0 + slab * LANES:c0 + (slab + 1) * LANES]

    @pl.when(step == 0)
    def _load_state():
        lru_carry[...] = jnp.zeros_like(lru_carry)
        pool_carry[...] = jnp.zeros_like(pool_carry)
        ext_pool[:, :, :POOL_PAD, :] = jnp.zeros((S, POOL_SLABS, POOL_PAD, LANES), F32)
        pool_tmp[:, :POOL_PAD, :] = jnp.zeros((2, POOL_PAD, LANES), F32)
        ffn_carry[...] = jnp.zeros_like(ffn_carry)
        scan_a[:, :SCAN_HDR, :] = jnp.ones((LRU_SLABS, SCAN_HDR, LANES), F32)
        scan_b[:, :SCAN_HDR, :] = jnp.zeros((LRU_SLABS, SCAN_HDR, LANES), F32)
        for s in range(n_fresh):
            h_carry[s] = jnp.zeros((SUBLANES, D_LRU), F32)
        for s in range(n_fresh, S):
            p = s - n_fresh
            h_carry[s] = jnp.broadcast_to(h0_ref[0, p:p + 1, :], (SUBLANES, D_LRU))
            for c in range(LRU_SLABS):
                for k in range(CONV_LRU - 1):
                    row = LRU_HDR - (CONV_LRU - 1) + k
                    lru_carry[s, c, row:row + 1, :] = conv0_ref[0, k, p:p + 1, _lanes(c)]
            for c in range(POOL_SLABS):
                for k in range(POOL_BUF):
                    row = POOL_KEEP - POOL_BUF + k
                    pool_carry[s, c, row:row + 1, :] = pool0_ref[0, k, p:p + 1, _lanes(c)]
            for c in range(FFN_SLABS):
                ffn_carry[s, c, FFN_HDR - (CONV_FFN - 1):, :] = ffn0_ref[0, p, :, _lanes(c)]

    def mixer(tile=0):
        if tiles_per_step == 1:
            x = jnp.concatenate([r[...].reshape(-1, D_MODEL) for r in x_refs], axis=0)
        else:
            x = x_refs[0][tile * T:(tile + 1) * T, :]

        u = _dot(_rmsnorm(x).astype(BF16), w_in_ref[...])
        u_lru = u[:, :D_LRU]
        g_lru = u[:, D_LRU:2 * D_LRU]
        u_pool = u[:, 2 * D_LRU:]

        lam = vec("lam")
        c_log_sig = LRU_C * (jnp.minimum(lam, 0.0) - jnp.log1p(jnp.exp(-jnp.abs(lam))))

        y_lru_parts = []
        m_parts = []
        for s in range(S):
            r0 = s * T
            c_slabs = []
            for c in range(LRU_SLABS):
                u_c = u_lru[r0:r0 + T, _lanes(c)]
                ext_lru[s, c, :LRU_HDR, :] = lru_carry[s, c]
                ext_lru[s, c, LRU_HDR:, :] = u_c
                lru_carry[s, c] = ext_lru[s, c, T:, :]
                acc_c = vec("conv_b", c) + u_c * vec(f"conv_w{CONV_LRU - 1}", c)
                for k in range(1, CONV_LRU):
                    acc_c = acc_c + (ext_lru[s, c, LRU_HDR - k:LRU_HDR - k + T, :]
                                     * vec(f"conv_w{CONV_LRU - 1 - k}", c))
                c_slabs.append(acc_c)
            c = jnp.concatenate(c_slabs, axis=1)

            cb = c.astype(BF16)
            gates = [_dot(cb[:, q * MXU_DIM:(q + 1) * MXU_DIM], w_gate_ref[q])
                     for q in range(D_LRU // MXU_DIM)]
            r = jax.nn.sigmoid(jnp.concatenate([g[:, :MXU_DIM] for g in gates], axis=1) + vec("ba"))
            i = jax.nn.sigmoid(jnp.concatenate([g[:, MXU_DIM:] for g in gates], axis=1) + vec("bx"))
            log_a = r * c_log_sig
            a = jnp.exp(log_a)
            one_minus_a2 = jnp.tanh(log_a) * (-1.0 - a * a)
            mult = jnp.where(one_minus_a2 > 0.0, one_minus_a2 * lax.rsqrt(one_minus_a2), 0.0)
            b = mult * (i * c)

            for d in SCAN_SHIFTS:
                a_prev, b_prev = [], []
                for cc in range(LRU_SLABS):
                    scan_a[cc, SCAN_HDR:, :] = a[:, _lanes(cc)]
                    scan_b[cc, SCAN_HDR:, :] = b[:, _lanes(cc)]
                    a_prev.append(scan_a[cc, SCAN_HDR - d:SCAN_HDR - d + T, :])
                    b_prev.append(scan_b[cc, SCAN_HDR - d:SCAN_HDR - d + T, :])
                b = a * jnp.concatenate(b_prev, axis=1) + b
                a = a * jnp.concatenate(a_prev, axis=1)
            h = h_carry[s]
            for g in range(T // SUBLANES):
                lo = g * SUBLANES
                h = a[lo:lo + SUBLANES] * h + b[lo:lo + SUBLANES]
                scan_h[r0 + lo:r0 + lo + SUBLANES, :] = h
            h_carry[s] = jnp.broadcast_to(scan_h[r0 + T - 1:r0 + T, :], (SUBLANES, D_LRU))
            y_lru_parts.append(scan_h[r0:r0 + T, :] * _gelu(g_lru[r0:r0 + T]))

            ramp_up = start_pos[s] + 1 < max(POOL_WINDOWS)
            if ramp_up:
                pos1 = (lax.broadcasted_iota(jnp.int32, (T, POOL_GROUP), 0) + (start_pos[s] + 1)).astype(F32)
            means = []
            for gi, w in enumerate(POOL_WINDOWS):
                ext_pool[s, gi, POOL_PAD:POOL_HDR, :] = pool_carry[s, gi]
                ext_pool[s, gi, POOL_HDR:, :] = u_pool[r0:r0 + T, _lanes(gi)]
                pool_carry[s, gi] = ext_pool[s, gi, T + POOL_PAD:, :]
                cur = ext_pool.at[s, gi]
                n_steps_ = w.bit_length() - 1
                for l in range(n_steps_):
                    d = 1 << l
                    lo = POOL_HDR if l == n_steps_ - 1 else POOL_PAD
                    partial = cur[lo:, :] + cur[lo - d:POOL_HDR + T - d, :]
                    if l < n_steps_ - 1:
                        cur = pool_tmp.at[l % 2]
                        cur[POOL_PAD:, :] = partial
                means.append(partial / jnp.minimum(pos1, float(w)) if ramp_up else partial * (1.0 / w))
            m_parts.append(jnp.concatenate(means, axis=1) - u_pool[r0:r0 + T])

        y_lru = y_lru_parts[0] if S == 1 else jnp.concatenate(y_lru_parts, axis=0)
        m = (m_parts[0] if S == 1 else jnp.concatenate(m_parts, axis=0)).astype(BF16)

        y_pool = jnp.concatenate(
            [_dot(m[:, q * MXU_DIM:(q + 1) * MXU_DIM], w_pool_ref[q]) for q in range(D_POOL // MXU_DIM)],
            axis=1)
        y_pool = (y_pool + vec("pool_b")) * vec("pool_scale")

        mix_in = jnp.concatenate(
            [_rmsnorm(y_lru), _rmsnorm(y_pool)], axis=1)
        x1 = x + _dot(mix_in.astype(BF16), w_out_ref[...])
        return x1

    def up_halves(j, xn, w_pair):
        return [_dot(xn, w_pair[half]) for half in range(2)]

    def gate_block(j, ups):
        halves = []
        for half in range(2):
            up = ups[half]
            parts = []
            for s in range(S):
                r0 = s * T
                slabs = []
                for c in range(CHUNK_SLABS):
                    gs = (half * FFN_CHUNKS + j) * CHUNK_SLABS + c
                    ext = ext_up.at[j % 2, half, s, c]
                    up_c = up[r0:r0 + T, _lanes(c)]
                    ext[:FFN_HDR, :] = ffn_carry[s, gs]
                    ext[FFN_HDR:, :] = up_c
                    ffn_carry[s, gs] = ext[T:, :]
                    upc = vec("ffn_b", gs) + up_c * vec(f"ffn_w{CONV_FFN - 1}", gs)
                    for k in range(1, CONV_FFN):
                        upc = upc + ext[FFN_HDR - k:FFN_HDR - k + T, :] * vec(f"ffn_w{CONV_FFN - 1 - k}", gs)
                    slabs.append(upc)
                parts.append(jnp.concatenate(slabs, axis=1))
            halves.append(parts[0] if S == 1 else jnp.concatenate(parts, axis=0))
        return (_gelu(halves[0]) * halves[1]).astype(BF16)

    def emit(rows_out, tile=0):
        y = _rmsnorm(rows_out, vec("final_g"))
        if tiles_per_step == 1:
            y_ref[...] = y[y_skip:].reshape(y_ref.shape)
        else:
            y_ref[tile * T:(tile + 1) * T, :] = y

    if cast_ffn:
        @pl.when(step == 0)
        def _mix():
            x1 = mixer()
            acc_buf[...] = x1
            xn_buf[...] = _rmsnorm(x1).astype(BF16)

        for j in range(FFN_CHUNKS):
            @pl.when(step == j)
            def _block(j=j):
                gain = up_gain_ref[...]
                w_pair = [_scale_rows(w_ref[0], gain).astype(BF16) for w_ref in (w_up_a_f32, w_up_b_f32)]
                w_down_j = w_down_f32[0].astype(BF16)
                for half in range(2):
                    w_up_out[half] = w_pair[half]
                w_down_out[0] = w_down_j
                act = gate_block(j, up_halves(j, xn_buf[...], w_pair))
                acc_buf[...] += _dot(act, w_down_j)

        @pl.when(step == last)
        def _emit():
            emit(acc_buf[...])
    else:
        w_pair = lambda j: [w_up_ref[2 * j + half] for half in range(2)]
        for tile in range(tiles_per_step):
            x1 = mixer(tile)
            xn = _rmsnorm(x1).astype(BF16)
            acc = jnp.zeros((S * T, D_MODEL), F32)
            ups = up_halves(0, xn, w_pair(0))
            for j in range(FFN_CHUNKS):
                ups_next = up_halves(j + 1, xn, w_pair(j + 1)) if j + 1 < FFN_CHUNKS else None
                acc = acc + _dot(gate_block(j, ups), w_down_ref[j])
                ups = ups_next
            emit(x1 + acc, tile)

    @pl.when(step == last)
    def _store_state():
        for (first, count), (h_out_ref, conv_out_ref, pool_out_ref, ffn_out_ref) in zip(
                out_groups, state_out_refs):
            for p in range(count):
                s = first + p
                h_out_ref[0, p:p + 1, :] = h_carry[s, :1, :]
                for c in range(LRU_SLABS):
                    for k in range(CONV_LRU - 1):
                        row = LRU_HDR - (CONV_LRU - 1) + k
                        conv_out_ref[0, k, p:p + 1, _lanes(c)] = lru_carry[s, c, row:row + 1, :]
                for c in range(POOL_SLABS):
                    for k in range(POOL_BUF):
                        row = POOL_KEEP - POOL_BUF + k
                        pool_out_ref[0, k, p:p + 1, _lanes(c)] = pool_carry[s, c, row:row + 1, :]
                for c in range(FFN_SLABS):
                    ffn_out_ref[0, p, :, _lanes(c)] = ffn_carry[s, c, FFN_HDR - (CONV_FFN - 1):, :]


def _scratch_shapes(n_streams, rows, cast_ffn):
    S, T = n_streams, rows
    across_steps = [
        pltpu.VMEM((S * T, D_MODEL), BF16),
        pltpu.VMEM((S * T, D_MODEL), F32),
    ] if cast_ffn else []
    return [
        pltpu.VMEM((S, SUBLANES, D_LRU), F32),
        pltpu.VMEM((S, LRU_SLABS, LRU_HDR, LANES), F32),
        pltpu.VMEM((S, POOL_SLABS, POOL_KEEP, LANES), F32),
        pltpu.VMEM((S, FFN_SLABS, FFN_HDR, LANES), F32),
        pltpu.VMEM((S, LRU_SLABS, LRU_HDR + T, LANES), F32),
        pltpu.VMEM((S, POOL_SLABS, POOL_HDR + T, LANES), F32),
        pltpu.VMEM((2, 2, S, CHUNK_SLABS, FFN_HDR + T, LANES), F32),
        pltpu.VMEM((LRU_SLABS, SCAN_HDR + T, LANES), F32),
        pltpu.VMEM((LRU_SLABS, SCAN_HDR + T, LANES), F32),
        pltpu.VMEM((S * T, D_LRU), F32),
        pltpu.VMEM((2, POOL_HDR + T, LANES), F32),
    ] + across_steps


def _vmem_limit(n_streams, rows, weights, cast_ffn, tiles_per_step):
    m = n_streams * rows
    weight_bytes = sum(w.size * w.dtype.itemsize for w in weights)
    io_bytes = 2 * 2 * tiles_per_step * m * D_MODEL * 4
    scratch_bytes = 0
    for sc in _scratch_shapes(n_streams, rows, cast_ffn):
        n = jnp.dtype(sc.dtype).itemsize
        for d in sc.shape:
            n *= d
        scratch_bytes += n
    budget = VMEM_BYTES_V7X - VMEM_RESERVE
    assert weight_bytes + io_bytes + scratch_bytes < budget
    return budget


def _state_shapes(count):
    return [
        jax.ShapeDtypeStruct((1, count, D_LRU), F32),
        jax.ShapeDtypeStruct((1, CONV_LRU - 1, count, D_LRU), F32),
        jax.ShapeDtypeStruct((1, POOL_BUF, count, D_POOL), F32),
        jax.ShapeDtypeStruct((1, count, CONV_FFN - 1, 2 * D_FF), F32),
    ]


def _rows_major(a):
    return a.transpose(0, 2, 1, 3)


def _run_streams(xs, states, weights, y_shape, *, n_streams, rows, start_pos, n_fresh, y_skip,
                 out_groups, tiled, name, ffn_f32=None):
    S, T = n_streams, rows
    m = S * T
    cast_ffn = ffn_f32 is not None
    assert not (cast_ffn and tiled)
    tiles_per_step = TILES_PER_STEP if tiled else 1
    block_rows = tiles_per_step * m
    n_steps = xs[0].shape[1] // block_rows if tiled else (FFN_CHUNKS if cast_ffn else 1)
    const = lambda shape: pl.BlockSpec(shape, lambda i, _n=len(shape): (0,) * _n)
    whole = lambda a: pl.BlockSpec(a.shape, lambda i, _n=a.ndim: (0,) * _n,
                                   pipeline_mode=pl.Buffered(1))
    tile_spec = pl.BlockSpec((None, block_rows, D_MODEL), lambda i: (0, i, 0))
    x_specs = [tile_spec] if tiled else [whole(a) for a in xs]
    out_shape = [jax.ShapeDtypeStruct(y_shape, F32)]
    for _, count in out_groups:
        out_shape += _state_shapes(count)
    out_specs = [tile_spec if tiled else const(y_shape)] + [const(o.shape) for o in out_shape[1:]]
    weight_specs = [whole(w) for w in weights]
    operands = [*xs, *states, *weights]
    if cast_ffn:
        w_up, w_down, up_gain = ffn_f32
        operands += [w_up, w_up, w_down, up_gain]
        weight_specs += [
            pl.BlockSpec((1, D_MODEL, FFN_CHUNK), lambda j: (0, 0, j)),
            pl.BlockSpec((1, D_MODEL, FFN_CHUNK), lambda j: (0, 0, FFN_CHUNKS + j)),
            pl.BlockSpec((1, FFN_CHUNK, D_MODEL), lambda j: (0, j, 0)),
            const(up_gain.shape)]
        out_shape += [jax.ShapeDtypeStruct((2 * FFN_CHUNKS, D_MODEL, FFN_CHUNK), BF16),
                      jax.ShapeDtypeStruct((FFN_CHUNKS, FFN_CHUNK, D_MODEL), BF16)]
        out_specs += [pl.BlockSpec((2, D_MODEL, FFN_CHUNK), lambda j: (j, 0, 0)),
                      pl.BlockSpec((1, FFN_CHUNK, D_MODEL), lambda j: (j, 0, 0))]
    kernel = functools.partial(
        _stream_kernel, n_streams=S, rows=T, start_pos=start_pos, n_x=len(xs), n_fresh=n_fresh,
        y_skip=y_skip, out_groups=out_groups, cast_ffn=cast_ffn, tiles_per_step=tiles_per_step)
    outs = pl.pallas_call(
        kernel,
        grid=(n_steps,),
        in_specs=x_specs + [whole(a) for a in states] + weight_specs,
        out_specs=out_specs,
        out_shape=out_shape,
        scratch_shapes=_scratch_shapes(S, T, cast_ffn),
        compiler_params=pltpu.CompilerParams(
            dimension_semantics=("arbitrary",),
            vmem_limit_bytes=_vmem_limit(S, T, weights, cast_ffn, tiles_per_step)),
        name=name,
    )(*operands)
    n_state = 4 * len(out_groups)
    return outs[0], [outs[1 + 4 * g:5 + 4 * g] for g in range(len(out_groups))], outs[1 + n_state:]


def _row_gain(gain_row):
    return jnp.broadcast_to(gain_row, (LANES, gain_row.shape[1])).T


def _scale_rows(w, gain_row):
    g = _row_gain(gain_row)
    return jnp.concatenate([w[:, _lanes(c)] * g for c in range(w.shape[1] // LANES)], axis=1)


def _prepare_mixer_kernel(*refs):
    (w_in_ref, w_out_ref, in_gain_ref, lru_gain_ref, pool_gain_ref, gate_a_ref, gate_x_ref,
     pool_w_ref), refs = refs[:8], refs[8:]
    n_vec = len(VEC_SOURCES)
    vec_refs, (w_in_out, w_out_out, w_gate_out, w_pool_out, vec_out) = refs[:n_vec], refs[n_vec:]
    w_in_out[...] = _scale_rows(w_in_ref[0], in_gain_ref[...]).astype(BF16)
    lru_rows = pl.program_id(0) < D_LRU // w_out_out.shape[0]
    out_gain = jnp.where(lru_rows, lru_gain_ref[...], pool_gain_ref[...])
    w_out_out[...] = _scale_rows(w_out_ref[0], out_gain).astype(BF16)

    @pl.when(pl.program_id(0) == 0)
    def _small_weights():
        heads = MXU_DIM // LRU_HEAD
        w_gate_out[...] = jnp.zeros_like(w_gate_out)
        for q in range(D_LRU // MXU_DIM):
            for p in range(heads):
                rows_ = slice(p * LRU_HEAD, (p + 1) * LRU_HEAD)
                for half, src_ref in enumerate((gate_a_ref, gate_x_ref)):
                    c0 = half * MXU_DIM + p * LRU_HEAD
                    w_gate_out[q, rows_, c0:c0 + LRU_HEAD] = src_ref[0, q * heads + p].astype(BF16)
        groups = MXU_DIM // POOL_GROUP
        w_pool_out[...] = jnp.zeros_like(w_pool_out)
        for q in range(D_POOL // MXU_DIM):
            for p in range(groups):
                blk = slice(p * POOL_GROUP, (p + 1) * POOL_GROUP)
                w_pool_out[q, blk, blk] = pool_w_ref[0, q * groups + p].astype(BF16)
        vec_out[...] = jnp.zeros_like(vec_out)
        for (name, pick), ref in zip(VEC_SOURCES, vec_refs):
            row, c0, width = VEC_LAYOUT[name]
            vec_out[row:row + 1, c0:c0 + width] = pick(ref)


VEC_SOURCES = (
    [(f"ffn_w{k}", lambda r, k=k: r[k]) for k in range(CONV_FFN)]
    + [(f"conv_w{k}", lambda r, k=k: r[0, k:k + 1, :]) for k in range(CONV_LRU)]
    + [(name, lambda r: r[...]) for name in (
        "ffn_b", "final_g", "conv_b", "ba", "bx", "lam", "pool_b", "pool_scale")])


def kernel(x_prompt, x_sample, state_lru_h, state_lru_conv, state_pool, state_ffn_conv, meta_tokens, norm_mix_g, w_in, conv_lru_w, conv_lru_b, gate_a_w, gate_a_b, gate_x_w, gate_x_b, lru_lambda, pool_w, pool_b, pool_scale, gn_lru, gn_pool, w_out, norm_ffn_g, w_up, ffn_conv_w, ffn_conv_b, w_down, final_norm_g):
    assert x_prompt.shape[0] == 1 and norm_mix_g.shape[0] == 1
    n_sample, sample_rows = x_sample.shape[:2]
    assert sample_rows == N_META

    vec_params = {
        **{f"ffn_w{k}": ffn_conv_w.transpose(1, 0, 2) for k in range(CONV_FFN)},
        **{f"conv_w{k}": conv_lru_w for k in range(CONV_LRU)},
        "ffn_b": ffn_conv_b, "final_g": final_norm_g.reshape(1, D_MODEL), "conv_b": conv_lru_b,
        "ba": gate_a_b, "bx": gate_x_b, "lam": lru_lambda, "pool_b": pool_b, "pool_scale": pool_scale}
    vec_args = [vec_params[name] for name, _ in VEC_SOURCES]
    mix_steps = 4
    mix_rows = D_MODEL // mix_steps
    group_blocks = D_LRU // mix_rows
    assert D_LRU == D_POOL and group_blocks * mix_rows == D_LRU
    d_in = w_in.shape[2]
    const = lambda shape: pl.BlockSpec(shape, lambda j, _n=len(shape): (0,) * _n)
    w_in_bf, w_out_bf, w_gate, w_pool, vectors = pl.pallas_call(
        _prepare_mixer_kernel,
        grid=(mix_steps,),
        in_specs=[pl.BlockSpec((1, mix_rows, d_in), lambda j: (0, j, 0)),
                  pl.BlockSpec((1, mix_rows, D_MODEL), lambda j: (0, j, 0)),
                  pl.BlockSpec((1, mix_rows), lambda j: (0, j)),
                  pl.BlockSpec((1, mix_rows), lambda j: (0, j % group_blocks)),
                  pl.BlockSpec((1, mix_rows), lambda j: (0, j % group_blocks)),
                  const(gate_a_w.shape), const(gate_x_w.shape), const(pool_w.shape)]
        + [const(a.shape) for a in vec_args],
        out_specs=[pl.BlockSpec((mix_rows, d_in), lambda j: (j, 0)),
                   pl.BlockSpec((mix_rows, D_MODEL), lambda j: (j, 0)),
                   const((D_LRU // MXU_DIM, MXU_DIM, 2 * MXU_DIM)),
                   const((D_POOL // MXU_DIM, MXU_DIM, MXU_DIM)),
                   const((VEC_ROWS, 2 * D_FF))],
        out_shape=[jax.ShapeDtypeStruct((D_MODEL, d_in), BF16),
                   jax.ShapeDtypeStruct((D_MODEL, D_MODEL), BF16),
                   jax.ShapeDtypeStruct((D_LRU // MXU_DIM, MXU_DIM, 2 * MXU_DIM), BF16),
                   jax.ShapeDtypeStruct((D_POOL // MXU_DIM, MXU_DIM, MXU_DIM), BF16),
                   jax.ShapeDtypeStruct((VEC_ROWS, 2 * D_FF), F32)],
        compiler_params=pltpu.CompilerParams(dimension_semantics=("arbitrary",)),
        name="prepare_mixer_weights",
    )(w_in, w_out, norm_mix_g, gn_lru, gn_pool, gate_a_w, gate_x_w, pool_w, *vec_args)
    mixer_weights = (vectors, w_in_bf, w_gate, w_pool, w_out_bf)

    y_s, (meta_state, sample_state), ffn_weights = _run_streams(
        (meta_tokens, x_sample),
        (state_lru_h, _rows_major(state_lru_conv), _rows_major(state_pool), state_ffn_conv),
        mixer_weights, x_sample.shape, n_streams=1 + n_sample, rows=N_META,
        start_pos=(0,) + (PAST_LEN,) * n_sample, n_fresh=1, y_skip=N_META,
        out_groups=((0, 1), (1, n_sample)), tiled=False, name="short_streams",
        ffn_f32=(w_up, w_down, norm_ffn_g))

    y_p, (prompt_state,), _ = _run_streams(
        (x_prompt,), meta_state, (*mixer_weights, *ffn_weights), x_prompt.shape, n_streams=1,
        rows=PROMPT_TILE, start_pos=(N_META,), n_fresh=0, y_skip=0, out_groups=((0, 1),), tiled=True,
        name="prompt_stream")

    restore = lambda st: (st[0], _rows_major(st[1]), _rows_major(st[2]), st[3])
    return (y_p, y_s, *restore(prompt_state), *restore(sample_state))
```

```python
import functools

import jax
import jax.numpy as jnp
from jax import lax
from jax.experimental import pallas as pl
from jax.experimental.pallas import tpu as pltpu

D_MODEL = 1024
N_META = 16
PAST_LEN = 1024
D_LRU = 512
N_LRU_HEADS = 8
LRU_HEAD = D_LRU // N_LRU_HEADS
LRU_C = 8.0
CONV_LRU = 4
D_POOL = 512
POOL_WINDOWS = (2, 4, 8, 16)
POOL_GROUP = D_POOL // len(POOL_WINDOWS)
POOL_BUF = max(POOL_WINDOWS) - 1
D_FF = 3 * D_MODEL
CONV_FFN = 3
EPS = 1e-6

SUBLANES = 8
LANES = 128
MXU_DIM = 256
VMEM_BYTES_V7X = 64 * 1024 * 1024
VMEM_RESERVE = 6 * 1024 * 1024

LRU_HDR = SUBLANES
POOL_PAD = SUBLANES
POOL_KEEP = 2 * SUBLANES
POOL_HDR = POOL_PAD + POOL_KEEP
FFN_HDR = SUBLANES
SCAN_HDR = SUBLANES
SCAN_SHIFTS = (1, 2, 4)
FFN_CHUNK = 512
FFN_CHUNKS = D_FF // FFN_CHUNK
LRU_SLABS = D_LRU // LANES
POOL_SLABS = D_POOL // LANES
CHUNK_SLABS = FFN_CHUNK // LANES
FFN_SLABS = 2 * D_FF // LANES
PROMPT_TILE = 512

VEC_ROWS = SUBLANES
VEC_LAYOUT = {}


def _vec_layout():
    for k in range(CONV_FFN):
        VEC_LAYOUT[f"ffn_w{k}"] = (k, 0, 2 * D_FF)
    VEC_LAYOUT["ffn_b"] = (CONV_FFN, 0, 2 * D_FF)
    col = 0
    for name, width in (("final_g", D_MODEL), ("conv_b", D_LRU), ("ba", D_LRU), ("bx", D_LRU),
                        ("lam", D_LRU), ("pool_b", D_POOL), ("pool_scale", D_POOL)):
        VEC_LAYOUT[name] = (CONV_FFN + 1, col, width)
        col += width
    assert col <= 2 * D_FF
    col = 0
    for name, width in [(f"conv_w{k}", D_LRU) for k in range(CONV_LRU)]:
        VEC_LAYOUT[name] = (CONV_FFN + 2, col, width)
        col += width
    assert col <= 2 * D_FF and CONV_FFN + 2 < VEC_ROWS


_vec_layout()

F32 = jnp.float32
BF16 = jnp.bfloat16


def _rmsnorm(x, g=None):
    y = x * lax.rsqrt(jnp.mean(x * x, axis=-1, keepdims=True) + EPS)
    return y if g is None else y * g


def _dot(a, b):
    return jnp.dot(a, b, preferred_element_type=F32)


GELU_C0 = (2.0 / jnp.pi) ** 0.5
GELU_C1 = 0.044715 * GELU_C0


def _gelu(x):
    half = 0.5 * x
    return half + half * jnp.tanh(x * (GELU_C0 + GELU_C1 * (x * x)))


def _lanes(c):
    return slice(c * LANES, (c + 1) * LANES)


def _stream_kernel(*refs, n_streams, rows, start_pos, n_x, n_fresh, y_skip, out_groups, cast_ffn):
    S, T = n_streams, rows
    refs = list(refs)
    take = lambda n: [refs.pop(0) for _ in range(n)]
    x_refs = take(n_x)
    h0_ref, conv0_ref, pool0_ref, ffn0_ref = take(4)
    vec_ref, w_in_ref, w_gate_ref, w_pool_ref, w_out_ref = take(5)
    if cast_ffn:
        w_up_a_f32, w_up_b_f32, w_down_f32, up_gain_ref = take(4)
    else:
        w_up_ref, w_down_ref = take(2)
    y_ref, = take(1)
    state_out_refs = [take(4) for _ in out_groups]
    if cast_ffn:
        w_up_out, w_down_out = take(2)
    (h_carry, lru_carry, pool_carry, ffn_carry, ext_lru, ext_pool, ext_up,
     scan_a, scan_b, scan_h, pool_tmp) = refs[:11]
    if cast_ffn:
        xn_buf, acc_buf = refs[11:]
    step = pl.program_id(0)
    last = pl.num_programs(0) - 1

    def vec(name, slab=None):
        row, c0, width = VEC_LAYOUT[name]
        if slab is None:
            return vec_ref[row:row + 1, c0:c0 + width]
        return vec_ref[row:row + 1, c0 + slab * LANES:c0 + (slab + 1) * LANES]

    @pl.when(step == 0)
    def _load_state():
        lru_carry[...] = jnp.zeros_like(lru_carry)
        pool_carry[...] = jnp.zeros_like(pool_carry)
        ext_pool[:, :, :POOL_PAD, :] = jnp.zeros((S, POOL_SLABS, POOL_PAD, LANES), F32)
        pool_tmp[:, :POOL_PAD, :] = jnp.zeros((2, POOL_PAD, LANES), F32)
        ffn_carry[...] = jnp.zeros_like(ffn_carry)
        scan_a[:, :SCAN_HDR, :] = jnp.ones((LRU_SLABS, SCAN_HDR, LANES), F32)
        scan_b[:, :SCAN_HDR, :] = jnp.zeros((LRU_SLABS, SCAN_HDR, LANES), F32)
        for s in range(n_fresh):
            h_carry[s] = jnp.zeros((SUBLANES, D_LRU), F32)
        for s in range(n_fresh, S):
            p = s - n_fresh
            h_carry[s] = jnp.broadcast_to(h0_ref[0, p:p + 1, :], (SUBLANES, D_LRU))
            for c in range(LRU_SLABS):
                for k in range(CONV_LRU - 1):
                    row = LRU_HDR - (CONV_LRU - 1) + k
                    lru_carry[s, c, row:row + 1, :] = conv0_ref[0, k, p:p + 1, _lanes(c)]
            for c in range(POOL_SLABS):
                for k in range(POOL_BUF):
                    row = POOL_KEEP - POOL_BUF + k
                    pool_carry[s, c, row:row + 1, :] = pool0_ref[0, k, p:p + 1, _lanes(c)]
            for c in range(FFN_SLABS):
                ffn_carry[s, c, FFN_HDR - (CONV_FFN - 1):, :] = ffn0_ref[0, p, :, _lanes(c)]

    def mixer():
        x = jnp.concatenate([r[...].reshape(-1, D_MODEL) for r in x_refs], axis=0)

        u = _dot(_rmsnorm(x).astype(BF16), w_in_ref[...])
        u_lru = u[:, :D_LRU]
        g_lru = u[:, D_LRU:2 * D_LRU]
        u_pool = u[:, 2 * D_LRU:]

        lam = vec("lam")
        c_log_sig = LRU_C * (jnp.minimum(lam, 0.0) - jnp.log1p(jnp.exp(-jnp.abs(lam))))

        y_lru_parts = []
        m_parts = []
        for s in range(S):
            r0 = s * T
            c_slabs = []
            for c in range(LRU_SLABS):
                u_c = u_lru[r0:r0 + T, _lanes(c)]
                ext_lru[s, c, :LRU_HDR, :] = lru_carry[s, c]
                ext_lru[s, c, LRU_HDR:, :] = u_c
                lru_carry[s, c] = ext_lru[s, c, T:, :]
                acc_c = vec("conv_b", c) + u_c * vec(f"conv_w{CONV_LRU - 1}", c)
                for k in range(1, CONV_LRU):
                    acc_c = acc_c + (ext_lru[s, c, LRU_HDR - k:LRU_HDR - k + T, :]
                                     * vec(f"conv_w{CONV_LRU - 1 - k}", c))
                c_slabs.append(acc_c)
            c = jnp.concatenate(c_slabs, axis=1)

            cb = c.astype(BF16)
            gates = [_dot(cb[:, q * MXU_DIM:(q + 1) * MXU_DIM], w_gate_ref[q])
                     for q in range(D_LRU // MXU_DIM)]
            r = jax.nn.sigmoid(jnp.concatenate([g[:, :MXU_DIM] for g in gates], axis=1) + vec("ba"))
            i = jax.nn.sigmoid(jnp.concatenate([g[:, MXU_DIM:] for g in gates], axis=1) + vec("bx"))
            log_a = r * c_log_sig
            a = jnp.exp(log_a)
            one_minus_a2 = jnp.tanh(log_a) * (-1.0 - a * a)
            mult = jnp.where(one_minus_a2 > 0.0, one_minus_a2 * lax.rsqrt(one_minus_a2), 0.0)
            b = mult * (i * c)

            for d in SCAN_SHIFTS:
                a_prev, b_prev = [], []
                for cc in range(LRU_SLABS):
                    scan_a[cc, SCAN_HDR:, :] = a[:, _lanes(cc)]
                    scan_b[cc, SCAN_HDR:, :] = b[:, _lanes(cc)]
                    a_prev.append(scan_a[cc, SCAN_HDR - d:SCAN_HDR - d + T, :])
                    b_prev.append(scan_b[cc, SCAN_HDR - d:SCAN_HDR - d + T, :])
                b = a * jnp.concatenate(b_prev, axis=1) + b
                a = a * jnp.concatenate(a_prev, axis=1)
            h = h_carry[s]
            for g in range(T // SUBLANES):
                lo = g * SUBLANES
                h = a[lo:lo + SUBLANES] * h + b[lo:lo + SUBLANES]
                scan_h[r0 + lo:r0 + lo + SUBLANES, :] = h
            h_carry[s] = jnp.broadcast_to(scan_h[r0 + T - 1:r0 + T, :], (SUBLANES, D_LRU))
            y_lru_parts.append(scan_h[r0:r0 + T, :] * _gelu(g_lru[r0:r0 + T]))

            ramp_up = start_pos[s] + 1 < max(POOL_WINDOWS)
            if ramp_up:
                pos1 = (lax.broadcasted_iota(jnp.int32, (T, POOL_GROUP), 0) + (start_pos[s] + 1)).astype(F32)
            means = []
            for gi, w in enumerate(POOL_WINDOWS):
                ext_pool[s, gi, POOL_PAD:POOL_HDR, :] = pool_carry[s, gi]
                ext_pool[s, gi, POOL_HDR:, :] = u_pool[r0:r0 + T, _lanes(gi)]
                pool_carry[s, gi] = ext_pool[s, gi, T + POOL_PAD:, :]
                cur = ext_pool.at[s, gi]
                n_steps_ = w.bit_length() - 1
                for l in range(n_steps_):
                    d = 1 << l
                    lo = POOL_HDR if l == n_steps_ - 1 else POOL_PAD
                    partial = cur[lo:, :] + cur[lo - d:POOL_HDR + T - d, :]
                    if l < n_steps_ - 1:
                        cur = pool_tmp.at[l % 2]
                        cur[POOL_PAD:, :] = partial
                means.append(partial / jnp.minimum(pos1, float(w)) if ramp_up else partial * (1.0 / w))
            m_parts.append(jnp.concatenate(means, axis=1) - u_pool[r0:r0 + T])

        y_lru = y_lru_parts[0] if S == 1 else jnp.concatenate(y_lru_parts, axis=0)
        m = (m_parts[0] if S == 1 else jnp.concatenate(m_parts, axis=0)).astype(BF16)

        y_pool = jnp.concatenate(
            [_dot(m[:, q * MXU_DIM:(q + 1) * MXU_DIM], w_pool_ref[q]) for q in range(D_POOL // MXU_DIM)],
            axis=1)
        y_pool = (y_pool + vec("pool_b")) * vec("pool_scale")

        mix_in = jnp.concatenate(
            [_rmsnorm(y_lru), _rmsnorm(y_pool)], axis=1)
        x1 = x + _dot(mix_in.astype(BF16), w_out_ref[...])
        return x1

    def up_halves(j, xn, w_pair):
        return [_dot(xn, w_pair[half]) for half in range(2)]

    def gate_block(j, ups):
        halves = []
        for half in range(2):
            up = ups[half]
            parts = []
            for s in range(S):
                r0 = s * T
                slabs = []
                for c in range(CHUNK_SLABS):
                    gs = (half * FFN_CHUNKS + j) * CHUNK_SLABS + c
                    ext = ext_up.at[j % 2, half, s, c]
                    up_c = up[r0:r0 + T, _lanes(c)]
                    ext[:FFN_HDR, :] = ffn_carry[s, gs]
                    ext[FFN_HDR:, :] = up_c
                    ffn_carry[s, gs] = ext[T:, :]
                    taps = [up_c * vec(f"ffn_w{CONV_FFN - 1}", gs)] + [
                        ext[FFN_HDR - k:FFN_HDR - k + T, :] * vec(f"ffn_w{CONV_FFN - 1 - k}", gs)
                        for k in range(1, CONV_FFN)]
                    slabs.append((taps[0] + taps[1]) + (taps[2] + vec("ffn_b", gs)))
                parts.append(jnp.concatenate(slabs, axis=1))
            halves.append(parts[0] if S == 1 else jnp.concatenate(parts, axis=0))
        return (_gelu(halves[0]) * halves[1]).astype(BF16)

    def emit(rows_out):
        y = _rmsnorm(rows_out, vec("final_g"))
        y_ref[...] = y[y_skip:].reshape(y_ref.shape)

    if cast_ffn:
        @pl.when(step == 0)
        def _mix():
            x1 = mixer()
            acc_buf[...] = x1
            xn_buf[...] = _rmsnorm(x1).astype(BF16)

        for j in range(FFN_CHUNKS):
            @pl.when(step == j)
            def _block(j=j):
                gain = up_gain_ref[...]
                w_pair = [_scale_rows(w_ref[0], gain).astype(BF16) for w_ref in (w_up_a_f32, w_up_b_f32)]
                w_down_j = w_down_f32[0].astype(BF16)
                for half in range(2):
                    w_up_out[half] = w_pair[half]
                w_down_out[0] = w_down_j
                act = gate_block(j, up_halves(j, xn_buf[...], w_pair))
                acc_buf[...] += _dot(act, w_down_j)

        @pl.when(step == last)
        def _emit():
            emit(acc_buf[...])
    else:
        x1 = mixer()
        xn = _rmsnorm(x1).astype(BF16)
        w_pair = lambda j: [w_up_ref[2 * j + half] for half in range(2)]
        acc = jnp.zeros((S * T, D_MODEL), F32)
        ups = up_halves(0, xn, w_pair(0))
        for j in range(FFN_CHUNKS):
            ups_next = up_halves(j + 1, xn, w_pair(j + 1)) if j + 1 < FFN_CHUNKS else None
            acc = acc + _dot(gate_block(j, ups), w_down_ref[j])
            ups = ups_next
        emit(x1 + acc)

    @pl.when(step == last)
    def _store_state():
        for (first, count), (h_out_ref, conv_out_ref, pool_out_ref, ffn_out_ref) in zip(
                out_groups, state_out_refs):
            for p in range(count):
                s = first + p
                h_out_ref[0, p:p + 1, :] = h_carry[s, :1, :]
                for c in range(LRU_SLABS):
                    for k in range(CONV_LRU - 1):
                        row = LRU_HDR - (CONV_LRU - 1) + k
                        conv_out_ref[0, k, p:p + 1, _lanes(c)] = lru_carry[s, c, row:row + 1, :]
                for c in range(POOL_SLABS):
                    for k in range(POOL_BUF):
                        row = POOL_KEEP - POOL_BUF + k
                        pool_out_ref[0, k, p:p + 1, _lanes(c)] = pool_carry[s, c, row:row + 1, :]
                for c in range(FFN_SLABS):
                    ffn_out_ref[0, p, :, _lanes(c)] = ffn_carry[s, c, FFN_HDR - (CONV_FFN - 1):, :]


def _scratch_shapes(n_streams, rows, cast_ffn):
    S, T = n_streams, rows
    across_steps = [
        pltpu.VMEM((S * T, D_MODEL), BF16),
        pltpu.VMEM((S * T, D_MODEL), F32),
    ] if cast_ffn else []
    return [
        pltpu.VMEM((S, SUBLANES, D_LRU), F32),
        pltpu.VMEM((S, LRU_SLABS, LRU_HDR, LANES), F32),
        pltpu.VMEM((S, POOL_SLABS, POOL_KEEP, LANES), F32),
        pltpu.VMEM((S, FFN_SLABS, FFN_HDR, LANES), F32),
        pltpu.VMEM((S, LRU_SLABS, LRU_HDR + T, LANES), F32),
        pltpu.VMEM((S, POOL_SLABS, POOL_HDR + T, LANES), F32),
        pltpu.VMEM((2, 2, S, CHUNK_SLABS, FFN_HDR + T, LANES), F32),
        pltpu.VMEM((LRU_SLABS, SCAN_HDR + T, LANES), F32),
        pltpu.VMEM((LRU_SLABS, SCAN_HDR + T, LANES), F32),
        pltpu.VMEM((S * T, D_LRU), F32),
        pltpu.VMEM((2, POOL_HDR + T, LANES), F32),
    ] + across_steps


def _vmem_limit(n_streams, rows, weights, cast_ffn):
    m = n_streams * rows
    weight_bytes = sum(w.size * w.dtype.itemsize for w in weights)
    io_bytes = 2 * 2 * m * D_MODEL * 4
    scratch_bytes = 0
    for sc in _scratch_shapes(n_streams, rows, cast_ffn):
        n = jnp.dtype(sc.dtype).itemsize
        for d in sc.shape:
            n *= d
        scratch_bytes += n
    budget = VMEM_BYTES_V7X - VMEM_RESERVE
    assert weight_bytes + io_bytes + scratch_bytes < budget
    return budget


def _state_shapes(count):
    return [
        jax.ShapeDtypeStruct((1, count, D_LRU), F32),
        jax.ShapeDtypeStruct((1, CONV_LRU - 1, count, D_LRU), F32),
        jax.ShapeDtypeStruct((1, POOL_BUF, count, D_POOL), F32),
        jax.ShapeDtypeStruct((1, count, CONV_FFN - 1, 2 * D_FF), F32),
    ]


def _rows_major(a):
    return a.transpose(0, 2, 1, 3)


def _run_streams(xs, states, weights, y_shape, *, n_streams, rows, start_pos, n_fresh, y_skip,
                 out_groups, tiled, name, ffn_f32=None):
    S, T = n_streams, rows
    m = S * T
    cast_ffn = ffn_f32 is not None
    assert not (cast_ffn and tiled)
    n_steps = xs[0].shape[1] // m if tiled else (FFN_CHUNKS if cast_ffn else 1)
    const = lambda shape: pl.BlockSpec(shape, lambda i, _n=len(shape): (0,) * _n)
    whole = lambda a: pl.BlockSpec(a.shape, lambda i, _n=a.ndim: (0,) * _n,
                                   pipeline_mode=pl.Buffered(1))
    tile_spec = pl.BlockSpec((None, m, D_MODEL), lambda i: (0, i, 0))
    x_specs = [tile_spec] if tiled else [whole(a) for a in xs]
    out_shape = [jax.ShapeDtypeStruct(y_shape, F32)]
    for _, count in out_groups:
        out_shape += _state_shapes(count)
    out_specs = [tile_spec if tiled else const(y_shape)] + [const(o.shape) for o in out_shape[1:]]
    weight_specs = [whole(w) for w in weights]
    operands = [*xs, *states, *weights]
    if cast_ffn:
        w_up, w_down, up_gain = ffn_f32
        operands += [w_up, w_up, w_down, up_gain]
        weight_specs += [
            pl.BlockSpec((1, D_MODEL, FFN_CHUNK), lambda j: (0, 0, j)),
            pl.BlockSpec((1, D_MODEL, FFN_CHUNK), lambda j: (0, 0, FFN_CHUNKS + j)),
            pl.BlockSpec((1, FFN_CHUNK, D_MODEL), lambda j: (0, j, 0)),
            const(up_gain.shape)]
        out_shape += [jax.ShapeDtypeStruct((2 * FFN_CHUNKS, D_MODEL, FFN_CHUNK), BF16),
                      jax.ShapeDtypeStruct((FFN_CHUNKS, FFN_CHUNK, D_MODEL), BF16)]
        out_specs += [pl.BlockSpec((2, D_MODEL, FFN_CHUNK), lambda j: (j, 0, 0)),
                      pl.BlockSpec((1, FFN_CHUNK, D_MODEL), lambda j: (j, 0, 0))]
    kernel = functools.partial(
        _stream_kernel, n_streams=S, rows=T, start_pos=start_pos, n_x=len(xs), n_fresh=n_fresh,
        y_skip=y_skip, out_groups=out_groups, cast_ffn=cast_ffn)
    outs = pl.pallas_call(
        kernel,
        grid=(n_steps,),
        in_specs=x_specs + [whole(a) for a in states] + weight_specs,
        out_specs=out_specs,
        out_shape=out_shape,
        scratch_shapes=_scratch_shapes(S, T, cast_ffn),
        compiler_params=pltpu.CompilerParams(
            dimension_semantics=("arbitrary",),
            vmem_limit_bytes=_vmem_limit(S, T, weights, cast_ffn)),
        name=name,
    )(*operands)
    n_state = 4 * len(out_groups)
    return outs[0], [outs[1 + 4 * g:5 + 4 * g] for g in range(len(out_groups))], outs[1 + n_state:]


def _row_gain(gain_row):
    return jnp.broadcast_to(gain_row, (LANES, gain_row.shape[1])).T


def _scale_rows(w, gain_row):
    g = _row_gain(gain_row)
    return jnp.concatenate([w[:, _lanes(c)] * g for c in range(w.shape[1] // LANES)], axis=1)


def _prepare_mixer_kernel(*refs):
    (w_in_ref, w_out_ref, in_gain_ref, lru_gain_ref, pool_gain_ref, gate_a_ref, gate_x_ref,
     pool_w_ref), refs = refs[:8], refs[8:]
    n_vec = len(VEC_SOURCES)
    vec_refs, (w_in_out, w_out_out, w_gate_out, w_pool_out, vec_out) = refs[:n_vec], refs[n_vec:]
    w_in_out[...] = _scale_rows(w_in_ref[0], in_gain_ref[...]).astype(BF16)
    lru_rows = pl.program_id(0) < D_LRU // w_out_out.shape[0]
    out_gain = jnp.where(lru_rows, lru_gain_ref[...], pool_gain_ref[...])
    w_out_out[...] = _scale_rows(w_out_ref[0], out_gain).astype(BF16)

    @pl.when(pl.program_id(0) == 0)
    def _small_weights():
        heads = MXU_DIM // LRU_HEAD
        w_gate_out[...] = jnp.zeros_like(w_gate_out)
        for q in range(D_LRU // MXU_DIM):
            for p in range(heads):
                rows_ = slice(p * LRU_HEAD, (p + 1) * LRU_HEAD)
                for half, src_ref in enumerate((gate_a_ref, gate_x_ref)):
                    c0 = half * MXU_DIM + p * LRU_HEAD
                    w_gate_out[q, rows_, c0:c0 + LRU_HEAD] = src_ref[0, q * heads + p].astype(BF16)
        groups = MXU_DIM // POOL_GROUP
        w_pool_out[...] = jnp.zeros_like(w_pool_out)
        for q in range(D_POOL // MXU_DIM):
            for p in range(groups):
                blk = slice(p * POOL_GROUP, (p + 1) * POOL_GROUP)
                w_pool_out[q, blk, blk] = pool_w_ref[0, q * groups + p].astype(BF16)
        vec_out[...] = jnp.zeros_like(vec_out)
        for (name, pick), ref in zip(VEC_SOURCES, vec_refs):
            row, c0, width = VEC_LAYOUT[name]
            vec_out[row:row + 1, c0:c0 + width] = pick(ref)


VEC_SOURCES = (
    [(f"ffn_w{k}", lambda r, k=k: r[k]) for k in range(CONV_FFN)]
    + [(f"conv_w{k}", lambda r, k=k: r[0, k:k + 1, :]) for k in range(CONV_LRU)]
    + [(name, lambda r: r[...]) for name in (
        "ffn_b", "final_g", "conv_b", "ba", "bx", "lam", "pool_b", "pool_scale")])


def kernel(x_prompt, x_sample, state_lru_h, state_lru_conv, state_pool, state_ffn_conv, meta_tokens, norm_mix_g, w_in, conv_lru_w, conv_lru_b, gate_a_w, gate_a_b, gate_x_w, gate_x_b, lru_lambda, pool_w, pool_b, pool_scale, gn_lru, gn_pool, w_out, norm_ffn_g, w_up, ffn_conv_w, ffn_conv_b, w_down, final_norm_g):
    assert x_prompt.shape[0] == 1 and norm_mix_g.shape[0] == 1
    n_sample, sample_rows = x_sample.shape[:2]
    assert sample_rows == N_META

    vec_params = {
        **{f"ffn_w{k}": ffn_conv_w.transpose(1, 0, 2) for k in range(CONV_FFN)},
        **{f"conv_w{k}": conv_lru_w for k in range(CONV_LRU)},
        "ffn_b": ffn_conv_b, "final_g": final_norm_g.reshape(1, D_MODEL), "conv_b": conv_lru_b,
        "ba": gate_a_b, "bx": gate_x_b, "lam": lru_lambda, "pool_b": pool_b, "pool_scale": pool_scale}
    vec_args = [vec_params[name] for name, _ in VEC_SOURCES]
    mix_steps = 4
    mix_rows = D_MODEL // mix_steps
    group_blocks = D_LRU // mix_rows
    assert D_LRU == D_POOL and group_blocks * mix_rows == D_LRU
    d_in = w_in.shape[2]
    const = lambda shape: pl.BlockSpec(shape, lambda j, _n=len(shape): (0,) * _n)
    w_in_bf, w_out_bf, w_gate, w_pool, vectors = pl.pallas_call(
        _prepare_mixer_kernel,
        grid=(mix_steps,),
        in_specs=[pl.BlockSpec((1, mix_rows, d_in), lambda j: (0, j, 0)),
                  pl.BlockSpec((1, mix_rows, D_MODEL), lambda j: (0, j, 0)),
                  pl.BlockSpec((1, mix_rows), lambda j: (0, j)),
                  pl.BlockSpec((1, mix_rows), lambda j: (0, j % group_blocks)),
                  pl.BlockSpec((1, mix_rows), lambda j: (0, j % group_blocks)),
                  const(gate_a_w.shape), const(gate_x_w.shape), const(pool_w.shape)]
        + [const(a.shape) for a in vec_args],
        out_specs=[pl.BlockSpec((mix_rows, d_in), lambda j: (j, 0)),
                   pl.BlockSpec((mix_rows, D_MODEL), lambda j: (j, 0)),
                   const((D_LRU // MXU_DIM, MXU_DIM, 2 * MXU_DIM)),
                   const((D_POOL // MXU_DIM, MXU_DIM, MXU_DIM)),
                   const((VEC_ROWS, 2 * D_FF))],
        out_shape=[jax.ShapeDtypeStruct((D_MODEL, d_in), BF16),
                   jax.ShapeDtypeStruct((D_MODEL, D_MODEL), BF16),
                   jax.ShapeDtypeStruct((D_LRU // MXU_DIM, MXU_DIM, 2 * MXU_DIM), BF16),
                   jax.ShapeDtypeStruct((D_POOL // MXU_DIM, MXU_DIM, MXU_DIM), BF16),
                   jax.ShapeDtypeStruct((VEC_ROWS, 2 * D_FF), F32)],
        compiler_params=pltpu.CompilerParams(dimension_semantics=("arbitrary",)),
        name="prepare_mixer_weights",
    )(w_in, w_out, norm_mix_g, gn_lru, gn_pool, gate_a_w, gate_x_w, pool_w, *vec_args)
    mixer_weights = (vectors, w_in_bf, w_gate, w_pool, w_out_bf)

    y_s, (meta_state, sample_state), ffn_weights = _run_streams(
        (meta_tokens, x_sample),
        (state_lru_h, _rows_major(state_lru_conv), _rows_major(state_pool), state_ffn_conv),
        mixer_weights, x_sample.shape, n_streams=1 + n_sample, rows=N_META,
        start_pos=(0,) + (PAST_LEN,) * n_sample, n_fresh=1, y_skip=N_META,
        out_groups=((0, 1), (1, n_sample)), tiled=False, name="short_streams",
        ffn_f32=(w_up, w_down, norm_ffn_g))

    y_p, (prompt_state,), _ = _run_streams(
        (x_prompt,), meta_state, (*mixer_weights, *ffn_weights), x_prompt.shape, n_streams=1,
        rows=PROMPT_TILE, start_pos=(N_META,), n_fresh=0, y_skip=0, out_groups=((0, 1),), tiled=True,
        name="prompt_stream")

    restore = lambda st: (st[0], _rows_major(st[1]), _rows_major(st[2]), st[3])
    return (y_p, y_s, *restore(prompt_state), *restore(sample_state))
```

```python
import functools

import jax
import jax.numpy as jnp
from jax import lax
from jax.experimental import pallas as pl
from jax.experimental.pallas import tpu as pltpu

D_MODEL = 1024
N_META = 16
PAST_LEN = 1024
D_LRU = 512
N_LRU_HEADS = 8
LRU_HEAD = D_LRU // N_LRU_HEADS
LRU_C = 8.0
CONV_LRU = 4
D_POOL = 512
POOL_WINDOWS = (2, 4, 8, 16)
POOL_GROUP = D_POOL // len(POOL_WINDOWS)
POOL_BUF = max(POOL_WINDOWS) - 1
D_FF = 3 * D_MODEL
CONV_FFN = 3
EPS = 1e-6

SUBLANES = 8
LANES = 128
MXU_DIM = 256
VMEM_BYTES_V7X = 64 * 1024 * 1024
VMEM_RESERVE = 6 * 1024 * 1024

LRU_HDR = SUBLANES
POOL_PAD = SUBLANES
POOL_KEEP = 2 * SUBLANES
POOL_HDR = POOL_PAD + POOL_KEEP
FFN_HDR = SUBLANES
SCAN_HDR = SUBLANES
SCAN_SHIFTS = (1, 2, 4)
FFN_CHUNK = 512
FFN_CHUNKS = D_FF // FFN_CHUNK
LRU_SLABS = D_LRU // LANES
POOL_SLABS = D_POOL // LANES
CHUNK_SLABS = FFN_CHUNK // LANES
FFN_SLABS = 2 * D_FF // LANES
PROMPT_TILE = 512

VEC_ROWS = SUBLANES
VEC_LAYOUT = {}


def _vec_layout():
    for k in range(CONV_FFN):
        VEC_LAYOUT[f"ffn_w{k}"] = (k, 0, 2 * D_FF)
    VEC_LAYOUT["ffn_b"] = (CONV_FFN, 0, 2 * D_FF)
    col = 0
    for name, width in (("final_g", D_MODEL), ("conv_b", D_LRU), ("ba", D_LRU), ("bx", D_LRU),
                        ("lam", D_LRU), ("pool_b", D_POOL), ("pool_scale", D_POOL)):
        VEC_LAYOUT[name] = (CONV_FFN + 1, col, width)
        col += width
    assert col <= 2 * D_FF
    col = 0
    for name, width in [(f"conv_w{k}", D_LRU) for k in range(CONV_LRU)]:
        VEC_LAYOUT[name] = (CONV_FFN + 2, col, width)
        col += width
    assert col <= 2 * D_FF and CONV_FFN + 2 < VEC_ROWS


_vec_layout()

F32 = jnp.float32
BF16 = jnp.bfloat16


def _rmsnorm(x, g=None):
    y = x * lax.rsqrt(jnp.mean(x * x, axis=-1, keepdims=True) + EPS)
    return y if g is None else y * g


def _dot(a, b):
    return jnp.dot(a, b, preferred_element_type=F32)


GELU_C0 = (2.0 / jnp.pi) ** 0.5
GELU_C1 = 0.044715 * GELU_C0


def _gelu(x):
    half = 0.5 * x
    return half + half * jnp.tanh(x * (GELU_C0 + GELU_C1 * (x * x)))


def _sigmoid(x):
    return 0.5 * jnp.tanh(0.5 * x) + 0.5


def _lanes(c):
    return slice(c * LANES, (c + 1) * LANES)


def _stream_kernel(*refs, n_streams, rows, start_pos, n_x, n_fresh, y_skip, out_groups, cast_ffn):
    S, T = n_streams, rows
    refs = list(refs)
    take = lambda n: [refs.pop(0) for _ in range(n)]
    x_refs = take(n_x)
    h0_ref, conv0_ref, pool0_ref, ffn0_ref = take(4)
    vec_ref, w_in_ref, w_gate_ref, w_pool_ref, w_out_ref = take(5)
    if cast_ffn:
        w_up_a_f32, w_up_b_f32, w_down_f32, up_gain_ref = take(4)
    else:
        w_up_ref, w_down_ref = take(2)
    y_ref, = take(1)
    state_out_refs = [take(4) for _ in out_groups]
    if cast_ffn:
        w_up_out, w_down_out = take(2)
    (h_carry, lru_carry, pool_carry, ffn_carry, ext_lru, ext_pool, ext_up,
     scan_a, scan_b, scan_h, pool_tmp) = refs[:11]
    if cast_ffn:
        xn_buf, acc_buf = refs[11:]
    step = pl.program_id(0)
    last = pl.num_programs(0) - 1

    def vec(name, slab=None):
        row, c0, width = VEC_LAYOUT[name]
        if slab is None:
            return vec_ref[row:row + 1, c0:c0 + width]
        return vec_ref[row:row + 1, c0 + slab * LANES:c0 + (slab + 1) * LANES]

    @pl.when(step == 0)
    def _load_state():
        lru_carry[...] = jnp.zeros_like(lru_carry)
        pool_carry[...] = jnp.zeros_like(pool_carry)
        ext_pool[:, :, :POOL_PAD, :] = jnp.zeros((S, POOL_SLABS, POOL_PAD, LANES), F32)
        pool_tmp[:, :POOL_PAD, :] = jnp.zeros((2, POOL_PAD, LANES), F32)
        ffn_carry[...] = jnp.zeros_like(ffn_carry)
        scan_a[:, :, :SCAN_HDR, :] = jnp.ones((len(SCAN_SHIFTS), LRU_SLABS, SCAN_HDR, LANES), F32)
        scan_b[:, :, :SCAN_HDR, :] = jnp.zeros((len(SCAN_SHIFTS), LRU_SLABS, SCAN_HDR, LANES), F32)
        for s in range(n_fresh):
            h_carry[s] = jnp.zeros((SUBLANES, D_LRU), F32)
        for s in range(n_fresh, S):
            p = s - n_fresh
            h_carry[s] = jnp.broadcast_to(h0_ref[0, p:p + 1, :], (SUBLANES, D_LRU))
            for c in range(LRU_SLABS):
                for k in range(CONV_LRU - 1):
                    row = LRU_HDR - (CONV_LRU - 1) + k
                    lru_carry[s, c, row:row + 1, :] = conv0_ref[0, k, p:p + 1, _lanes(c)]
            for c in range(POOL_SLABS):
                for k in range(POOL_BUF):
                    row = POOL_KEEP - POOL_BUF + k
                    pool_carry[s, c, row:row + 1, :] = pool0_ref[0, k, p:p + 1, _lanes(c)]
            for c in range(FFN_SLABS):
                ffn_carry[s, c, FFN_HDR - (CONV_FFN - 1):, :] = ffn0_ref[0, p, :, _lanes(c)]

    def mixer():
        x = jnp.concatenate([r[...].reshape(-1, D_MODEL) for r in x_refs], axis=0)

        u = _dot(_rmsnorm(x).astype(BF16), w_in_ref[...])
        u_lru = u[:, :D_LRU]
        g_lru = u[:, D_LRU:2 * D_LRU]
        u_pool = u[:, 2 * D_LRU:]

        lam = vec("lam")
        c_log_sig = LRU_C * (jnp.minimum(lam, 0.0) - jnp.log1p(jnp.exp(-jnp.abs(lam))))

        y_lru_parts = []
        m_parts = []
        for s in range(S):
            r0 = s * T
            c_slabs = []
            for c in range(LRU_SLABS):
                u_c = u_lru[r0:r0 + T, _lanes(c)]
                ext_lru[s, c, :LRU_HDR, :] = lru_carry[s, c]
                ext_lru[s, c, LRU_HDR:, :] = u_c
                lru_carry[s, c] = ext_lru[s, c, T:, :]
                acc_c = vec("conv_b", c) + u_c * vec(f"conv_w{CONV_LRU - 1}", c)
                for k in range(1, CONV_LRU):
                    acc_c = acc_c + (ext_lru[s, c, LRU_HDR - k:LRU_HDR - k + T, :]
                                     * vec(f"conv_w{CONV_LRU - 1 - k}", c))
                c_slabs.append(acc_c)
            c = jnp.concatenate(c_slabs, axis=1)

            cb = c.astype(BF16)
            gates = [_dot(cb[:, q * MXU_DIM:(q + 1) * MXU_DIM], w_gate_ref[q])
                     for q in range(D_LRU // MXU_DIM)]
            r = _sigmoid(jnp.concatenate([g[:, :MXU_DIM] for g in gates], axis=1) + vec("ba"))
            i = _sigmoid(jnp.concatenate([g[:, MXU_DIM:] for g in gates], axis=1) + vec("bx"))
            log_a = r * c_log_sig
            a = jnp.exp(log_a)
            one_minus_a2 = jnp.tanh(log_a) * (-1.0 - a * a)
            mult = jnp.where(one_minus_a2 > 0.0, one_minus_a2 * lax.rsqrt(one_minus_a2), 0.0)
            b = mult * (i * c)

            for lvl, d in enumerate(SCAN_SHIFTS):
                a_prev, b_prev = [], []
                for cc in range(LRU_SLABS):
                    scan_a[lvl, cc, SCAN_HDR:, :] = a[:, _lanes(cc)]
                    scan_b[lvl, cc, SCAN_HDR:, :] = b[:, _lanes(cc)]
                    a_prev.append(scan_a[lvl, cc, SCAN_HDR - d:SCAN_HDR - d + T, :])
                    b_prev.append(scan_b[lvl, cc, SCAN_HDR - d:SCAN_HDR - d + T, :])
                b = a * jnp.concatenate(b_prev, axis=1) + b
                a = a * jnp.concatenate(a_prev, axis=1)
            h = h_carry[s]
            for g in range(T // SUBLANES):
                lo = g * SUBLANES
                h = a[lo:lo + SUBLANES] * h + b[lo:lo + SUBLANES]
                scan_h[r0 + lo:r0 + lo + SUBLANES, :] = h
            h_carry[s] = jnp.broadcast_to(scan_h[r0 + T - 1:r0 + T, :], (SUBLANES, D_LRU))
            y_lru_parts.append(scan_h[r0:r0 + T, :] * _gelu(g_lru[r0:r0 + T]))

            ramp_up = start_pos[s] + 1 < max(POOL_WINDOWS)
            if ramp_up:
                pos1 = (lax.broadcasted_iota(jnp.int32, (T, POOL_GROUP), 0) + (start_pos[s] + 1)).astype(F32)
            means = []
            for gi, w in enumerate(POOL_WINDOWS):
                ext_pool[s, gi, POOL_PAD:POOL_HDR, :] = pool_carry[s, gi]
                ext_pool[s, gi, POOL_HDR:, :] = u_pool[r0:r0 + T, _lanes(gi)]
                pool_carry[s, gi] = ext_pool[s, gi, T + POOL_PAD:, :]
                cur = ext_pool.at[s, gi]
                n_steps_ = w.bit_length() - 1
                for l in range(n_steps_):
                    d = 1 << l
                    lo = POOL_HDR if l == n_steps_ - 1 else POOL_PAD
                    partial = cur[lo:, :] + cur[lo - d:POOL_HDR + T - d, :]
                    if l < n_steps_ - 1:
                        cur = pool_tmp.at[l % 2]
                        cur[POOL_PAD:, :] = partial
                means.append(partial / jnp.minimum(pos1, float(w)) if ramp_up else partial * (1.0 / w))
            m_parts.append(jnp.concatenate(means, axis=1) - u_pool[r0:r0 + T])

        y_lru = y_lru_parts[0] if S == 1 else jnp.concatenate(y_lru_parts, axis=0)
        m = (m_parts[0] if S == 1 else jnp.concatenate(m_parts, axis=0)).astype(BF16)

        y_pool = jnp.concatenate(
            [_dot(m[:, q * MXU_DIM:(q + 1) * MXU_DIM], w_pool_ref[q]) for q in range(D_POOL // MXU_DIM)],
            axis=1)
        y_pool = (y_pool + vec("pool_b")) * vec("pool_scale")

        mix_in = jnp.concatenate(
            [_rmsnorm(y_lru), _rmsnorm(y_pool)], axis=1)
        x1 = x + _dot(mix_in.astype(BF16), w_out_ref[...])
        return x1

    def up_halves(j, xn, w_pair):
        return [_dot(xn, w_pair[half]) for half in range(2)]

    def gate_block(j, ups):
        halves = []
        for half in range(2):
            up = ups[half]
            parts = []
            for s in range(S):
                r0 = s * T
                slabs = []
                for c in range(CHUNK_SLABS):
                    gs = (half * FFN_CHUNKS + j) * CHUNK_SLABS + c
                    ext = ext_up.at[j % 2, half, s, c]
                    up_c = up[r0:r0 + T, _lanes(c)]
                    ext[:FFN_HDR, :] = ffn_carry[s, gs]
                    ext[FFN_HDR:, :] = up_c
                    ffn_carry[s, gs] = ext[T:, :]
                    taps = [up_c * vec(f"ffn_w{CONV_FFN - 1}", gs)] + [
                        ext[FFN_HDR - k:FFN_HDR - k + T, :] * vec(f"ffn_w{CONV_FFN - 1 - k}", gs)
                        for k in range(1, CONV_FFN)]
                    slabs.append((taps[0] + taps[1]) + (taps[2] + vec("ffn_b", gs)))
                parts.append(jnp.concatenate(slabs, axis=1))
            halves.append(parts[0] if S == 1 else jnp.concatenate(parts, axis=0))
        return (_gelu(halves[0]) * halves[1]).astype(BF16)

    def emit(rows_out):
        y = _rmsnorm(rows_out, vec("final_g"))
        y_ref[...] = y[y_skip:].reshape(y_ref.shape)

    if cast_ffn:
        @pl.when(step == 0)
        def _mix():
            x1 = mixer()
            acc_buf[...] = x1
            xn_buf[...] = _rmsnorm(x1).astype(BF16)

        for j in range(FFN_CHUNKS):
            @pl.when(step == j)
            def _block(j=j):
                gain = up_gain_ref[...]
                w_pair = [_scale_rows(w_ref[0], gain).astype(BF16) for w_ref in (w_up_a_f32, w_up_b_f32)]
                w_down_j = w_down_f32[0].astype(BF16)
                for half in range(2):
                    w_up_out[half] = w_pair[half]
                w_down_out[0] = w_down_j
                act = gate_block(j, up_halves(j, xn_buf[...], w_pair))
                acc_buf[...] += _dot(act, w_down_j)

        @pl.when(step == last)
        def _emit():
            emit(acc_buf[...])
    else:
        x1 = mixer()
        xn = _rmsnorm(x1).astype(BF16)
        w_pair = lambda j: [w_up_ref[2 * j + half] for half in range(2)]
        acc = jnp.zeros((S * T, D_MODEL), F32)
        ups = up_halves(0, xn, w_pair(0))
        for j in range(FFN_CHUNKS):
            ups_next = up_halves(j + 1, xn, w_pair(j + 1)) if j + 1 < FFN_CHUNKS else None
            acc = acc + _dot(gate_block(j, ups), w_down_ref[j])
            ups = ups_next
        emit(x1 + acc)

    @pl.when(step == last)
    def _store_state():
        for (first, count), (h_out_ref, conv_out_ref, pool_out_ref, ffn_out_ref) in zip(
                out_groups, state_out_refs):
            for p in range(count):
                s = first + p
                h_out_ref[0, p:p + 1, :] = h_carry[s, :1, :]
                for c in range(LRU_SLABS):
                    for k in range(CONV_LRU - 1):
                        row = LRU_HDR - (CONV_LRU - 1) + k
                        conv_out_ref[0, k, p:p + 1, _lanes(c)] = lru_carry[s, c, row:row + 1, :]
                for c in range(POOL_SLABS):
                    for k in range(POOL_BUF):
                        row = POOL_KEEP - POOL_BUF + k
                        pool_out_ref[0, k, p:p + 1, _lanes(c)] = pool_carry[s, c, row:row + 1, :]
                for c in range(FFN_SLABS):
                    ffn_out_ref[0, p, :, _lanes(c)] = ffn_carry[s, c, FFN_HDR - (CONV_FFN - 1):, :]


def _scratch_shapes(n_streams, rows, cast_ffn):
    S, T = n_streams, rows
    across_steps = [
        pltpu.VMEM((S * T, D_MODEL), BF16),
        pltpu.VMEM((S * T, D_MODEL), F32),
    ] if cast_ffn else []
    return [
        pltpu.VMEM((S, SUBLANES, D_LRU), F32),
        pltpu.VMEM((S, LRU_SLABS, LRU_HDR, LANES), F32),
        pltpu.VMEM((S, POOL_SLABS, POOL_KEEP, LANES), F32),
        pltpu.VMEM((S, FFN_SLABS, FFN_HDR, LANES), F32),
        pltpu.VMEM((S, LRU_SLABS, LRU_HDR + T, LANES), F32),
        pltpu.VMEM((S, POOL_SLABS, POOL_HDR + T, LANES), F32),
        pltpu.VMEM((2, 2, S, CHUNK_SLABS, FFN_HDR + T, LANES), F32),
        pltpu.VMEM((len(SCAN_SHIFTS), LRU_SLABS, SCAN_HDR + T, LANES), F32),
        pltpu.VMEM((len(SCAN_SHIFTS), LRU_SLABS, SCAN_HDR + T, LANES), F32),
        pltpu.VMEM((S * T, D_LRU), F32),
        pltpu.VMEM((2, POOL_HDR + T, LANES), F32),
    ] + across_steps


def _vmem_limit(n_streams, rows, weights, cast_ffn):
    m = n_streams * rows
    weight_bytes = sum(w.size * w.dtype.itemsize for w in weights)
    io_bytes = 2 * 2 * m * D_MODEL * 4
    scratch_bytes = 0
    for sc in _scratch_shapes(n_streams, rows, cast_ffn):
        n = jnp.dtype(sc.dtype).itemsize
        for d in sc.shape:
            n *= d
        scratch_bytes += n
    budget = VMEM_BYTES_V7X - VMEM_RESERVE
    assert weight_bytes + io_bytes + scratch_bytes < budget
    return budget


def _state_shapes(count):
    return [
        jax.ShapeDtypeStruct((1, count, D_LRU), F32),
        jax.ShapeDtypeStruct((1, CONV_LRU - 1, count, D_LRU), F32),
        jax.ShapeDtypeStruct((1, POOL_BUF, count, D_POOL), F32),
        jax.ShapeDtypeStruct((1, count, CONV_FFN - 1, 2 * D_FF), F32),
    ]


def _rows_major(a):
    return a.transpose(0, 2, 1, 3)


def _run_streams(xs, states, weights, y_shape, *, n_streams, rows, start_pos, n_fresh, y_skip,
                 out_groups, tiled, name, ffn_f32=None):
    S, T = n_streams, rows
    m = S * T
    cast_ffn = ffn_f32 is not None
    assert not (cast_ffn and tiled)
    n_steps = xs[0].shape[1] // m if tiled else (FFN_CHUNKS if cast_ffn else 1)
    const = lambda shape: pl.BlockSpec(shape, lambda i, _n=len(shape): (0,) * _n)
    whole = lambda a: pl.BlockSpec(a.shape, lambda i, _n=a.ndim: (0,) * _n,
                                   pipeline_mode=pl.Buffered(1))
    tile_spec = pl.BlockSpec((None, m, D_MODEL), lambda i: (0, i, 0))
    x_specs = [tile_spec] if tiled else [whole(a) for a in xs]
    out_shape = [jax.ShapeDtypeStruct(y_shape, F32)]
    for _, count in out_groups:
        out_shape += _state_shapes(count)
    out_specs = [tile_spec if tiled else const(y_shape)] + [const(o.shape) for o in out_shape[1:]]
    weight_specs = [whole(w) for w in weights]
    operands = [*xs, *states, *weights]
    if cast_ffn:
        w_up, w_down, up_gain = ffn_f32
        operands += [w_up, w_up, w_down, up_gain]
        weight_specs += [
            pl.BlockSpec((1, D_MODEL, FFN_CHUNK), lambda j: (0, 0, j)),
            pl.BlockSpec((1, D_MODEL, FFN_CHUNK), lambda j: (0, 0, FFN_CHUNKS + j)),
            pl.BlockSpec((1, FFN_CHUNK, D_MODEL), lambda j: (0, j, 0)),
            const(up_gain.shape)]
        out_shape += [jax.ShapeDtypeStruct((2 * FFN_CHUNKS, D_MODEL, FFN_CHUNK), BF16),
                      jax.ShapeDtypeStruct((FFN_CHUNKS, FFN_CHUNK, D_MODEL), BF16)]
        out_specs += [pl.BlockSpec((2, D_MODEL, FFN_CHUNK), lambda j: (j, 0, 0)),
                      pl.BlockSpec((1, FFN_CHUNK, D_MODEL), lambda j: (j, 0, 0))]
    kernel = functools.partial(
        _stream_kernel, n_streams=S, rows=T, start_pos=start_pos, n_x=len(xs), n_fresh=n_fresh,
        y_skip=y_skip, out_groups=out_groups, cast_ffn=cast_ffn)
    outs = pl.pallas_call(
        kernel,
        grid=(n_steps,),
        in_specs=x_specs + [whole(a) for a in states] + weight_specs,
        out_specs=out_specs,
        out_shape=out_shape,
        scratch_shapes=_scratch_shapes(S, T, cast_ffn),
        compiler_params=pltpu.CompilerParams(
            dimension_semantics=("arbitrary",),
            vmem_limit_bytes=_vmem_limit(S, T, weights, cast_ffn)),
        name=name,
    )(*operands)
    n_state = 4 * len(out_groups)
    return outs[0], [outs[1 + 4 * g:5 + 4 * g] for g in range(len(out_groups))], outs[1 + n_state:]


def _row_gain(gain_row):
    return jnp.broadcast_to(gain_row, (LANES, gain_row.shape[1])).T


def _scale_rows(w, gain_row):
    g = _row_gain(gain_row)
    return jnp.concatenate([w[:, _lanes(c)] * g for c in range(w.shape[1] // LANES)], axis=1)


def _prepare_mixer_kernel(*refs):
    (w_in_ref, w_out_ref, in_gain_ref, lru_gain_ref, pool_gain_ref, gate_a_ref, gate_x_ref,
     pool_w_ref), refs = refs[:8], refs[8:]
    n_vec = len(VEC_SOURCES)
    vec_refs, (w_in_out, w_out_out, w_gate_out, w_pool_out, vec_out) = refs[:n_vec], refs[n_vec:]
    w_in_out[...] = _scale_rows(w_in_ref[0], in_gain_ref[...]).astype(BF16)
    lru_rows = pl.program_id(0) < D_LRU // w_out_out.shape[0]
    out_gain = jnp.where(lru_rows, lru_gain_ref[...], pool_gain_ref[...])
    w_out_out[...] = _scale_rows(w_out_ref[0], out_gain).astype(BF16)

    @pl.when(pl.program_id(0) == 0)
    def _small_weights():
        heads = MXU_DIM // LRU_HEAD
        w_gate_out[...] = jnp.zeros_like(w_gate_out)
        for q in range(D_LRU // MXU_DIM):
            for p in range(heads):
                rows_ = slice(p * LRU_HEAD, (p + 1) * LRU_HEAD)
                for half, src_ref in enumerate((gate_a_ref, gate_x_ref)):
                    c0 = half * MXU_DIM + p * LRU_HEAD
                    w_gate_out[q, rows_, c0:c0 + LRU_HEAD] = src_ref[0, q * heads + p].astype(BF16)
        groups = MXU_DIM // POOL_GROUP
        w_pool_out[...] = jnp.zeros_like(w_pool_out)
        for q in range(D_POOL // MXU_DIM):
            for p in range(groups):
                blk = slice(p * POOL_GROUP, (p + 1) * POOL_GROUP)
                w_pool_out[q, blk, blk] = pool_w_ref[0, q * groups + p].astype(BF16)
        vec_out[...] = jnp.zeros_like(vec_out)
        for (name, pick), ref in zip(VEC_SOURCES, vec_refs):
            row, c0, width = VEC_LAYOUT[name]
            vec_out[row:row + 1, c0:c0 + width] = pick(ref)


VEC_SOURCES = (
    [(f"ffn_w{k}", lambda r, k=k: r[k]) for k in range(CONV_FFN)]
    + [(f"conv_w{k}", lambda r, k=k: r[0, k:k + 1, :]) for k in range(CONV_LRU)]
    + [(name, lambda r: r[...]) for name in (
        "ffn_b", "final_g", "conv_b", "ba", "bx", "lam", "pool_b", "pool_scale")])


def kernel(x_prompt, x_sample, state_lru_h, state_lru_conv, state_pool, state_ffn_conv, meta_tokens, norm_mix_g, w_in, conv_lru_w, conv_lru_b, gate_a_w, gate_a_b, gate_x_w, gate_x_b, lru_lambda, pool_w, pool_b, pool_scale, gn_lru, gn_pool, w_out, norm_ffn_g, w_up, ffn_conv_w, ffn_conv_b, w_down, final_norm_g):
    assert x_prompt.shape[0] == 1 and norm_mix_g.shape[0] == 1
    n_sample, sample_rows = x_sample.shape[:2]
    assert sample_rows == N_META

    vec_params = {
        **{f"ffn_w{k}": ffn_conv_w.transpose(1, 0, 2) for k in range(CONV_FFN)},
        **{f"conv_w{k}": conv_lru_w for k in range(CONV_LRU)},
        "ffn_b": ffn_conv_b, "final_g": final_norm_g.reshape(1, D_MODEL), "conv_b": conv_lru_b,
        "ba": gate_a_b, "bx": gate_x_b, "lam": lru_lambda, "pool_b": pool_b, "pool_scale": pool_scale}
    vec_args = [vec_params[name] for name, _ in VEC_SOURCES]
    mix_steps = 2
    mix_rows = D_MODEL // mix_steps
    group_blocks = D_LRU // mix_rows
    assert D_LRU == D_POOL and group_blocks * mix_rows == D_LRU
    d_in = w_in.shape[2]
    const = lambda shape: pl.BlockSpec(shape, lambda j, _n=len(shape): (0,) * _n)
    w_in_bf, w_out_bf, w_gate, w_pool, vectors = pl.pallas_call(
        _prepare_mixer_kernel,
        grid=(mix_steps,),
        in_specs=[pl.BlockSpec((1, mix_rows, d_in), lambda j: (0, j, 0)),
                  pl.BlockSpec((1, mix_rows, D_MODEL), lambda j: (0, j, 0)),
                  pl.BlockSpec((1, mix_rows), lambda j: (0, j)),
                  pl.BlockSpec((1, mix_rows), lambda j: (0, j % group_blocks)),
                  pl.BlockSpec((1, mix_rows), lambda j: (0, j % group_blocks)),
                  const(gate_a_w.shape), const(gate_x_w.shape), const(pool_w.shape)]
        + [const(a.shape) for a in vec_args],
        out_specs=[pl.BlockSpec((mix_rows, d_in), lambda j: (j, 0)),
                   pl.BlockSpec((mix_rows, D_MODEL), lambda j: (j, 0)),
                   const((D_LRU // MXU_DIM, MXU_DIM, 2 * MXU_DIM)),
                   const((D_POOL // MXU_DIM, MXU_DIM, MXU_DIM)),
                   const((VEC_ROWS, 2 * D_FF))],
        out_shape=[jax.ShapeDtypeStruct((D_MODEL, d_in), BF16),
                   jax.ShapeDtypeStruct((D_MODEL, D_MODEL), BF16),
                   jax.ShapeDtypeStruct((D_LRU // MXU_DIM, MXU_DIM, 2 * MXU_DIM), BF16),
                   jax.ShapeDtypeStruct((D_POOL // MXU_DIM, MXU_DIM, MXU_DIM), BF16),
                   jax.ShapeDtypeStruct((VEC_ROWS, 2 * D_FF), F32)],
        compiler_params=pltpu.CompilerParams(dimension_semantics=("arbitrary",)),
        name="prepare_mixer_weights",
    )(w_in, w_out, norm_mix_g, gn_lru, gn_pool, gate_a_w, gate_x_w, pool_w, *vec_args)
    mixer_weights = (vectors, w_in_bf, w_gate, w_pool, w_out_bf)

    y_s, (meta_state, sample_state), ffn_weights = _run_streams(
        (meta_tokens, x_sample),
        (state_lru_h, _rows_major(state_lru_conv), _rows_major(state_pool), state_ffn_conv),
        mixer_weights, x_sample.shape, n_streams=1 + n_sample, rows=N_META,
        start_pos=(0,) + (PAST_LEN,) * n_sample, n_fresh=1, y_skip=N_META,
        out_groups=((0, 1), (1, n_sample)), tiled=False, name="short_streams",
        ffn_f32=(w_up, w_down, norm_ffn_g))

    y_p, (prompt_state,), _ = _run_streams(
        (x_prompt,), meta_state, (*mixer_weights, *ffn_weights), x_prompt.shape, n_streams=1,
        rows=PROMPT_TILE, start_pos=(N_META,), n_fresh=0, y_skip=0, out_groups=((0, 1),), tiled=True,
        name="prompt_stream")

    restore = lambda st: (st[0], _rows_major(st[1]), _rows_major(st[2]), st[3])
    return (y_p, y_s, *restore(prompt_state), *restore(sample_state))
```

```python
import functools

import jax
import jax.numpy as jnp
from jax import lax
from jax.experimental import pallas as pl
from jax.experimental.pallas import tpu as pltpu

D_MODEL = 1024
N_META = 16
PAST_LEN = 1024
D_LRU = 512
N_LRU_HEADS = 8
LRU_HEAD = D_LRU // N_LRU_HEADS
LRU_C = 8.0
CONV_LRU = 4
D_POOL = 512
POOL_WINDOWS = (2, 4, 8, 16)
POOL_GROUP = D_POOL // len(POOL_WINDOWS)
POOL_BUF = max(POOL_WINDOWS) - 1
D_FF = 3 * D_MODEL
CONV_FFN = 3
EPS = 1e-6

SUBLANES = 8
LANES = 128
MXU_DIM = 256
VMEM_BYTES_V7X = 64 * 1024 * 1024
VMEM_RESERVE = 6 * 1024 * 1024

LRU_HDR = SUBLANES
POOL_PAD = SUBLANES
POOL_KEEP = 2 * SUBLANES
POOL_HDR = POOL_PAD + POOL_KEEP
FFN_HDR = SUBLANES
SCAN_HDR = SUBLANES
SCAN_SHIFTS = (1, 2, 4)
FFN_CHUNK = 512
FFN_CHUNKS = D_FF // FFN_CHUNK
LRU_SLABS = D_LRU // LANES
POOL_SLABS = D_POOL // LANES
CHUNK_SLABS = FFN_CHUNK // LANES
FFN_SLABS = 2 * D_FF // LANES
PROMPT_TILE = 512

VEC_ROWS = SUBLANES
VEC_LAYOUT = {}


def _vec_layout():
    for k in range(CONV_FFN):
        VEC_LAYOUT[f"ffn_w{k}"] = (k, 0, 2 * D_FF)
    VEC_LAYOUT["ffn_b"] = (CONV_FFN, 0, 2 * D_FF)
    col = 0
    for name, width in (("final_g", D_MODEL), ("conv_b", D_LRU), ("ba", D_LRU), ("bx", D_LRU),
                        ("lam", D_LRU), ("pool_b", D_POOL), ("pool_scale", D_POOL)):
        VEC_LAYOUT[name] = (CONV_FFN + 1, col, width)
        col += width
    assert col <= 2 * D_FF
    col = 0
    for name, width in [(f"conv_w{k}", D_LRU) for k in range(CONV_LRU)]:
        VEC_LAYOUT[name] = (CONV_FFN + 2, col, width)
        col += width
    assert col <= 2 * D_FF and CONV_FFN + 2 < VEC_ROWS


_vec_layout()

F32 = jnp.float32
BF16 = jnp.bfloat16


def _rmsnorm(x, g=None):
    y = x * lax.rsqrt(jnp.mean(x * x, axis=-1, keepdims=True) + EPS)
    return y if g is None else y * g


def _dot(a, b):
    return jnp.dot(a, b, preferred_element_type=F32)


GELU_C0 = (2.0 / jnp.pi) ** 0.5
GELU_C1 = 0.044715 * GELU_C0


def _gelu(x):
    half = 0.5 * x
    return half + half * jnp.tanh(x * (GELU_C0 + GELU_C1 * (x * x)))


def _sigmoid(x):
    return 0.5 * jnp.tanh(0.5 * x) + 0.5


def _lanes(c):
    return slice(c * LANES, (c + 1) * LANES)


def _stream_kernel(*refs, n_streams, rows, start_pos, n_x, n_fresh, y_skip, out_groups, cast_ffn):
    S, T = n_streams, rows
    refs = list(refs)
    take = lambda n: [refs.pop(0) for _ in range(n)]
    x_refs = take(n_x)
    h0_ref, conv0_ref, pool0_ref, ffn0_ref = take(4)
    vec_ref, w_in_ref, w_gate_ref, w_pool_ref, w_out_ref = take(5)
    if cast_ffn:
        w_up_a_f32, w_up_b_f32, w_down_f32, up_gain_ref = take(4)
    else:
        w_up_ref, w_down_ref = take(2)
    y_ref, = take(1)
    state_out_refs = [take(4) for _ in out_groups]
    if cast_ffn:
        w_up_out, w_down_out = take(2)
    (h_carry, lru_carry, pool_carry, ffn_carry, ext_lru, ext_pool, ext_up,
     scan_a, scan_b, scan_h, pool_tmp) = refs[:11]
    if cast_ffn:
        xn_buf, acc_buf = refs[11:]
    step = pl.program_id(0)
    last = pl.num_programs(0) - 1

    def vec(name, slab=None):
        row, c0, width = VEC_LAYOUT[name]
        if slab is None:
            return vec_ref[row:row + 1, c0:c0 + width]
        return vec_ref[row:row + 1, c0 + slab * LANES:c0 + (slab + 1) * LANES]

    @pl.when(step == 0)
    def _load_state():
        lru_carry[...] = jnp.zeros_like(lru_carry)
        pool_carry[...] = jnp.zeros_like(pool_carry)
        ext_pool[:, :, :POOL_PAD, :] = jnp.zeros((S, POOL_SLABS, POOL_PAD, LANES), F32)
        pool_tmp[:, :POOL_PAD, :] = jnp.zeros((2, POOL_PAD, LANES), F32)
        ffn_carry[...] = jnp.zeros_like(ffn_carry)
        scan_a[:, :, :SCAN_HDR, :] = jnp.ones((len(SCAN_SHIFTS), LRU_SLABS, SCAN_HDR, LANES), F32)
        scan_b[:, :, :SCAN_HDR, :] = jnp.zeros((len(SCAN_SHIFTS), LRU_SLABS, SCAN_HDR, LANES), F32)
        for s in range(n_fresh):
            h_carry[s] = jnp.zeros((SUBLANES, D_LRU), F32)
        for s in range(n_fresh, S):
            p = s - n_fresh
            h_carry[s] = jnp.broadcast_to(h0_ref[0, p:p + 1, :], (SUBLANES, D_LRU))
            for c in range(LRU_SLABS):
                for k in range(CONV_LRU - 1):
                    row = LRU_HDR - (CONV_LRU - 1) + k
                    lru_carry[s, c, row:row + 1, :] = conv0_ref[0, k, p:p + 1, _lanes(c)]
            for c in range(POOL_SLABS):
                for k in range(POOL_BUF):
                    row = POOL_KEEP - POOL_BUF + k
                    pool_carry[s, c, row:row + 1, :] = pool0_ref[0, k, p:p + 1, _lanes(c)]
            for c in range(FFN_SLABS):
                ffn_carry[s, c, FFN_HDR - (CONV_FFN - 1):, :] = ffn0_ref[0, p, :, _lanes(c)]

    def mixer():
        x = jnp.concatenate([r[...].reshape(-1, D_MODEL) for r in x_refs], axis=0)

        u = _dot(_rmsnorm(x).astype(BF16), w_in_ref[...])
        u_lru = u[:, :D_LRU]
        g_lru = u[:, D_LRU:2 * D_LRU]
        u_pool = u[:, 2 * D_LRU:]

        lam = vec("lam")
        c_log_sig = LRU_C * (jnp.minimum(lam, 0.0) - jnp.log1p(jnp.exp(-jnp.abs(lam))))

        y_lru_parts = []
        m_parts = []
        for s in range(S):
            r0 = s * T
            c_slabs = []
            for c in range(LRU_SLABS):
                u_c = u_lru[r0:r0 + T, _lanes(c)]
                ext_lru[s, c, :LRU_HDR, :] = lru_carry[s, c]
                ext_lru[s, c, LRU_HDR:, :] = u_c
                lru_carry[s, c] = ext_lru[s, c, T:, :]
                acc_c = vec("conv_b", c) + u_c * vec(f"conv_w{CONV_LRU - 1}", c)
                for k in range(1, CONV_LRU):
                    acc_c = acc_c + (ext_lru[s, c, LRU_HDR - k:LRU_HDR - k + T, :]
                                     * vec(f"conv_w{CONV_LRU - 1 - k}", c))
                c_slabs.append(acc_c)
            c = jnp.concatenate(c_slabs, axis=1)

            cb = c.astype(BF16)
            half_slabs = MXU_DIM // LANES
            y_halves = []
            for q in range(D_LRU // MXU_DIM):
                ch = slice(q * MXU_DIM, (q + 1) * MXU_DIM)
                gate = _dot(cb[:, ch], w_gate_ref[q])
                r = _sigmoid(gate[:, :MXU_DIM] + vec("ba")[:, ch])
                i = _sigmoid(gate[:, MXU_DIM:] + vec("bx")[:, ch])
                log_a = r * c_log_sig[:, ch]
                a = jnp.exp(log_a)
                one_minus_a2 = jnp.tanh(log_a) * (-1.0 - a * a)
                mult = jnp.where(one_minus_a2 > 0.0, one_minus_a2 * lax.rsqrt(one_minus_a2), 0.0)
                b = mult * (i * c[:, ch])

                for lvl, d in enumerate(SCAN_SHIFTS):
                    a_prev, b_prev = [], []
                    for k in range(half_slabs):
                        cc = q * half_slabs + k
                        scan_a[lvl, cc, SCAN_HDR:, :] = a[:, _lanes(k)]
                        scan_b[lvl, cc, SCAN_HDR:, :] = b[:, _lanes(k)]
                        a_prev.append(scan_a[lvl, cc, SCAN_HDR - d:SCAN_HDR - d + T, :])
                        b_prev.append(scan_b[lvl, cc, SCAN_HDR - d:SCAN_HDR - d + T, :])
                    b = a * jnp.concatenate(b_prev, axis=1) + b
                    a = a * jnp.concatenate(a_prev, axis=1)
                h = h_carry[s, :, ch]
                for g in range(T // SUBLANES):
                    lo = g * SUBLANES
                    h = a[lo:lo + SUBLANES] * h + b[lo:lo + SUBLANES]
                    scan_h[r0 + lo:r0 + lo + SUBLANES, ch] = h
                h_carry[s, :, ch] = jnp.broadcast_to(scan_h[r0 + T - 1:r0 + T, ch], (SUBLANES, MXU_DIM))
                y_halves.append(scan_h[r0:r0 + T, ch] * _gelu(g_lru[r0:r0 + T, ch]))
            y_lru_parts.append(jnp.concatenate(y_halves, axis=1))

            ramp_up = start_pos[s] + 1 < max(POOL_WINDOWS)
            if ramp_up:
                pos1 = (lax.broadcasted_iota(jnp.int32, (T, POOL_GROUP), 0) + (start_pos[s] + 1)).astype(F32)
            means = []
            for gi, w in enumerate(POOL_WINDOWS):
                ext_pool[s, gi, POOL_PAD:POOL_HDR, :] = pool_carry[s, gi]
                ext_pool[s, gi, POOL_HDR:, :] = u_pool[r0:r0 + T, _lanes(gi)]
                pool_carry[s, gi] = ext_pool[s, gi, T + POOL_PAD:, :]
                cur = ext_pool.at[s, gi]
                n_steps_ = w.bit_length() - 1
                for l in range(n_steps_):
                    d = 1 << l
                    lo = POOL_HDR if l == n_steps_ - 1 else POOL_PAD
                    partial = cur[lo:, :] + cur[lo - d:POOL_HDR + T - d, :]
                    if l < n_steps_ - 1:
                        cur = pool_tmp.at[l % 2]
                        cur[POOL_PAD:, :] = partial
                means.append(partial / jnp.minimum(pos1, float(w)) if ramp_up else partial * (1.0 / w))
            m_parts.append(jnp.concatenate(means, axis=1) - u_pool[r0:r0 + T])

        y_lru = y_lru_parts[0] if S == 1 else jnp.concatenate(y_lru_parts, axis=0)
        m = (m_parts[0] if S == 1 else jnp.concatenate(m_parts, axis=0)).astype(BF16)

        y_pool = jnp.concatenate(
            [_dot(m[:, q * MXU_DIM:(q + 1) * MXU_DIM], w_pool_ref[q]) for q in range(D_POOL // MXU_DIM)],
            axis=1)
        y_pool = (y_pool + vec("pool_b")) * vec("pool_scale")

        mix_in = jnp.concatenate(
            [_rmsnorm(y_lru), _rmsnorm(y_pool)], axis=1)
        x1 = x + _dot(mix_in.astype(BF16), w_out_ref[...])
        return x1

    def up_halves(j, xn, w_pair):
        return [_dot(xn, w_pair[half]) for half in range(2)]

    def gate_block(j, ups):
        halves = []
        for half in range(2):
            up = ups[half]
            parts = []
            for s in range(S):
                r0 = s * T
                slabs = []
                for c in range(CHUNK_SLABS):
                    gs = (half * FFN_CHUNKS + j) * CHUNK_SLABS + c
                    ext = ext_up.at[j % 2, half, s, c]
                    up_c = up[r0:r0 + T, _lanes(c)]
                    ext[:FFN_HDR, :] = ffn_carry[s, gs]
                    ext[FFN_HDR:, :] = up_c
                    ffn_carry[s, gs] = ext[T:, :]
                    taps = [up_c * vec(f"ffn_w{CONV_FFN - 1}", gs)] + [
                        ext[FFN_HDR - k:FFN_HDR - k + T, :] * vec(f"ffn_w{CONV_FFN - 1 - k}", gs)
                        for k in range(1, CONV_FFN)]
                    slabs.append((taps[0] + taps[1]) + (taps[2] + vec("ffn_b", gs)))
                parts.append(jnp.concatenate(slabs, axis=1))
            halves.append(parts[0] if S == 1 else jnp.concatenate(parts, axis=0))
        return (_gelu(halves[0]) * halves[1]).astype(BF16)

    def emit(rows_out):
        y = _rmsnorm(rows_out, vec("final_g"))
        y_ref[...] = y[y_skip:].reshape(y_ref.shape)

    if cast_ffn:
        @pl.when(step == 0)
        def _mix():
            x1 = mixer()
            acc_buf[...] = x1
            xn_buf[...] = _rmsnorm(x1).astype(BF16)

        for j in range(FFN_CHUNKS):
            @pl.when(step == j)
            def _block(j=j):
                gain = up_gain_ref[...]
                w_pair = [_scale_rows(w_ref[0], gain).astype(BF16) for w_ref in (w_up_a_f32, w_up_b_f32)]
                w_down_j = w_down_f32[0].astype(BF16)
                for half in range(2):
                    w_up_out[half] = w_pair[half]
                w_down_out[0] = w_down_j
                act = gate_block(j, up_halves(j, xn_buf[...], w_pair))
                acc_buf[...] += _dot(act, w_down_j)

        @pl.when(step == last)
        def _emit():
            emit(acc_buf[...])
    else:
        x1 = mixer()
        xn = _rmsnorm(x1).astype(BF16)
        w_pair = lambda j: [w_up_ref[2 * j + half] for half in range(2)]
        acc = jnp.zeros((S * T, D_MODEL), F32)
        ups = up_halves(0, xn, w_pair(0))
        for j in range(FFN_CHUNKS):
            ups_next = up_halves(j + 1, xn, w_pair(j + 1)) if j + 1 < FFN_CHUNKS else None
            acc = acc + _dot(gate_block(j, ups), w_down_ref[j])
            ups = ups_next
        emit(x1 + acc)

    @pl.when(step == last)
    def _store_state():
        for (first, count), (h_out_ref, conv_out_ref, pool_out_ref, ffn_out_ref) in zip(
                out_groups, state_out_refs):
            for p in range(count):
                s = first + p
                h_out_ref[0, p:p + 1, :] = h_carry[s, :1, :]
                for c in range(LRU_SLABS):
                    for k in range(CONV_LRU - 1):
                        row = LRU_HDR - (CONV_LRU - 1) + k
                        conv_out_ref[0, k, p:p + 1, _lanes(c)] = lru_carry[s, c, row:row + 1, :]
                for c in range(POOL_SLABS):
                    for k in range(POOL_BUF):
                        row = POOL_KEEP - POOL_BUF + k
                        pool_out_ref[0, k, p:p + 1, _lanes(c)] = pool_carry[s, c, row:row + 1, :]
                for c in range(FFN_SLABS):
                    ffn_out_ref[0, p, :, _lanes(c)] = ffn_carry[s, c, FFN_HDR - (CONV_FFN - 1):, :]


def _scratch_shapes(n_streams, rows, cast_ffn):
    S, T = n_streams, rows
    across_steps = [
        pltpu.VMEM((S * T, D_MODEL), BF16),
        pltpu.VMEM((S * T, D_MODEL), F32),
    ] if cast_ffn else []
    return [
        pltpu.VMEM((S, SUBLANES, D_LRU), F32),
        pltpu.VMEM((S, LRU_SLABS, LRU_HDR, LANES), F32),
        pltpu.VMEM((S, POOL_SLABS, POOL_KEEP, LANES), F32),
        pltpu.VMEM((S, FFN_SLABS, FFN_HDR, LANES), F32),
        pltpu.VMEM((S, LRU_SLABS, LRU_HDR + T, LANES), F32),
        pltpu.VMEM((S, POOL_SLABS, POOL_HDR + T, LANES), F32),
        pltpu.VMEM((2, 2, S, CHUNK_SLABS, FFN_HDR + T, LANES), F32),
        pltpu.VMEM((len(SCAN_SHIFTS), LRU_SLABS, SCAN_HDR + T, LANES), F32),
        pltpu.VMEM((len(SCAN_SHIFTS), LRU_SLABS, SCAN_HDR + T, LANES), F32),
        pltpu.VMEM((S * T, D_LRU), F32),
        pltpu.VMEM((2, POOL_HDR + T, LANES), F32),
    ] + across_steps


def _vmem_limit(n_streams, rows, weights, cast_ffn):
    m = n_streams * rows
    weight_bytes = sum(w.size * w.dtype.itemsize for w in weights)
    io_bytes = 2 * 2 * m * D_MODEL * 4
    scratch_bytes = 0
    for sc in _scratch_shapes(n_streams, rows, cast_ffn):
        n = jnp.dtype(sc.dtype).itemsize
        for d in sc.shape:
            n *= d
        scratch_bytes += n
    budget = VMEM_BYTES_V7X - VMEM_RESERVE
    assert weight_bytes + io_bytes + scratch_bytes < budget
    return budget


def _state_shapes(count):
    return [
        jax.ShapeDtypeStruct((1, count, D_LRU), F32),
        jax.ShapeDtypeStruct((1, CONV_LRU - 1, count, D_LRU), F32),
        jax.ShapeDtypeStruct((1, POOL_BUF, count, D_POOL), F32),
        jax.ShapeDtypeStruct((1, count, CONV_FFN - 1, 2 * D_FF), F32),
    ]


def _rows_major(a):
    return a.transpose(0, 2, 1, 3)


def _run_streams(xs, states, weights, y_shape, *, n_streams, rows, start_pos, n_fresh, y_skip,
                 out_groups, tiled, name, ffn_f32=None):
    S, T = n_streams, rows
    m = S * T
    cast_ffn = ffn_f32 is not None
    assert not (cast_ffn and tiled)
    n_steps = xs[0].shape[1] // m if tiled else (FFN_CHUNKS if cast_ffn else 1)
    const = lambda shape: pl.BlockSpec(shape, lambda i, _n=len(shape): (0,) * _n)
    whole = lambda a: pl.BlockSpec(a.shape, lambda i, _n=a.ndim: (0,) * _n,
                                   pipeline_mode=pl.Buffered(1))
    tile_spec = pl.BlockSpec((None, m, D_MODEL), lambda i: (0, i, 0))
    x_specs = [tile_spec] if tiled else [whole(a) for a in xs]
    out_shape = [jax.ShapeDtypeStruct(y_shape, F32)]
    for _, count in out_groups:
        out_shape += _state_shapes(count)
    out_specs = [tile_spec if tiled else const(y_shape)] + [const(o.shape) for o in out_shape[1:]]
    weight_specs = [whole(w) for w in weights]
    operands = [*xs, *states, *weights]
    if cast_ffn:
        w_up, w_down, up_gain = ffn_f32
        operands += [w_up, w_up, w_down, up_gain]
        weight_specs += [
            pl.BlockSpec((1, D_MODEL, FFN_CHUNK), lambda j: (0, 0, j)),
            pl.BlockSpec((1, D_MODEL, FFN_CHUNK), lambda j: (0, 0, FFN_CHUNKS + j)),
            pl.BlockSpec((1, FFN_CHUNK, D_MODEL), lambda j: (0, j, 0)),
            const(up_gain.shape)]
        out_shape += [jax.ShapeDtypeStruct((2 * FFN_CHUNKS, D_MODEL, FFN_CHUNK), BF16),
                      jax.ShapeDtypeStruct((FFN_CHUNKS, FFN_CHUNK, D_MODEL), BF16)]
        out_specs += [pl.BlockSpec((2, D_MODEL, FFN_CHUNK), lambda j: (j, 0, 0)),
                      pl.BlockSpec((1, FFN_CHUNK, D_MODEL), lambda j: (j, 0, 0))]
    kernel = functools.partial(
        _stream_kernel, n_streams=S, rows=T, start_pos=start_pos, n_x=len(xs), n_fresh=n_fresh,
        y_skip=y_skip, out_groups=out_groups, cast_ffn=cast_ffn)
    outs = pl.pallas_call(
        kernel,
        grid=(n_steps,),
        in_specs=x_specs + [whole(a) for a in states] + weight_specs,
        out_specs=out_specs,
        out_shape=out_shape,
        scratch_shapes=_scratch_shapes(S, T, cast_ffn),
        compiler_params=pltpu.CompilerParams(
            dimension_semantics=("arbitrary",),
            vmem_limit_bytes=_vmem_limit(S, T, weights, cast_ffn)),
        name=name,
    )(*operands)
    n_state = 4 * len(out_groups)
    return outs[0], [outs[1 + 4 * g:5 + 4 * g] for g in range(len(out_groups))], outs[1 + n_state:]


def _row_gain(gain_row):
    return jnp.broadcast_to(gain_row, (LANES, gain_row.shape[1])).T


def _scale_rows(w, gain_row):
    g = _row_gain(gain_row)
    return jnp.concatenate([w[:, _lanes(c)] * g for c in range(w.shape[1] // LANES)], axis=1)


def _prepare_mixer_kernel(*refs):
    (w_in_ref, w_out_ref, in_gain_ref, lru_gain_ref, pool_gain_ref, gate_a_ref, gate_x_ref,
     pool_w_ref), refs = refs[:8], refs[8:]
    n_vec = len(VEC_SOURCES)
    vec_refs, (w_in_out, w_out_out, w_gate_out, w_pool_out, vec_out) = refs[:n_vec], refs[n_vec:]
    w_in_out[...] = _scale_rows(w_in_ref[0], in_gain_ref[...]).astype(BF16)
    lru_rows = pl.program_id(0) < D_LRU // w_out_out.shape[0]
    out_gain = jnp.where(lru_rows, lru_gain_ref[...], pool_gain_ref[...])
    w_out_out[...] = _scale_rows(w_out_ref[0], out_gain).astype(BF16)

    @pl.when(pl.program_id(0) == 0)
    def _small_weights():
        heads = MXU_DIM // LRU_HEAD
        w_gate_out[...] = jnp.zeros_like(w_gate_out)
        for q in range(D_LRU // MXU_DIM):
            for p in range(heads):
                rows_ = slice(p * LRU_HEAD, (p + 1) * LRU_HEAD)
                for half, src_ref in enumerate((gate_a_ref, gate_x_ref)):
                    c0 = half * MXU_DIM + p * LRU_HEAD
                    w_gate_out[q, rows_, c0:c0 + LRU_HEAD] = src_ref[0, q * heads + p].astype(BF16)
        groups = MXU_DIM // POOL_GROUP
        w_pool_out[...] = jnp.zeros_like(w_pool_out)
        for q in range(D_POOL // MXU_DIM):
            for p in range(groups):
                blk = slice(p * POOL_GROUP, (p + 1) * POOL_GROUP)
                w_pool_out[q, blk, blk] = pool_w_ref[0, q * groups + p].astype(BF16)
        vec_out[...] = jnp.zeros_like(vec_out)
        for (name, pick), ref in zip(VEC_SOURCES, vec_refs):
            row, c0, width = VEC_LAYOUT[name]
            vec_out[row:row + 1, c0:c0 + width] = pick(ref)


VEC_SOURCES = (
    [(f"ffn_w{k}", lambda r, k=k: r[k]) for k in range(CONV_FFN)]
    + [(f"conv_w{k}", lambda r, k=k: r[0, k:k + 1, :]) for k in range(CONV_LRU)]
    + [(name, lambda r: r[...]) for name in (
        "ffn_b", "final_g", "conv_b", "ba", "bx", "lam", "pool_b", "pool_scale")])


def kernel(x_prompt, x_sample, state_lru_h, state_lru_conv, state_pool, state_ffn_conv, meta_tokens, norm_mix_g, w_in, conv_lru_w, conv_lru_b, gate_a_w, gate_a_b, gate_x_w, gate_x_b, lru_lambda, pool_w, pool_b, pool_scale, gn_lru, gn_pool, w_out, norm_ffn_g, w_up, ffn_conv_w, ffn_conv_b, w_down, final_norm_g):
    assert x_prompt.shape[0] == 1 and norm_mix_g.shape[0] == 1
    n_sample, sample_rows = x_sample.shape[:2]
    assert sample_rows == N_META

    vec_params = {
        **{f"ffn_w{k}": ffn_conv_w.transpose(1, 0, 2) for k in range(CONV_FFN)},
        **{f"conv_w{k}": conv_lru_w for k in range(CONV_LRU)},
        "ffn_b": ffn_conv_b, "final_g": final_norm_g.reshape(1, D_MODEL), "conv_b": conv_lru_b,
        "ba": gate_a_b, "bx": gate_x_b, "lam": lru_lambda, "pool_b": pool_b, "pool_scale": pool_scale}
    vec_args = [vec_params[name] for name, _ in VEC_SOURCES]
    mix_steps = 2
    mix_rows = D_MODEL // mix_steps
    group_blocks = D_LRU // mix_rows
    assert D_LRU == D_POOL and group_blocks * mix_rows == D_LRU
    d_in = w_in.shape[2]
    const = lambda shape: pl.BlockSpec(shape, lambda j, _n=len(shape): (0,) * _n)
    w_in_bf, w_out_bf, w_gate, w_pool, vectors = pl.pallas_call(
        _prepare_mixer_kernel,
        grid=(mix_steps,),
        in_specs=[pl.BlockSpec((1, mix_rows, d_in), lambda j: (0, j, 0)),
                  pl.BlockSpec((1, mix_rows, D_MODEL), lambda j: (0, j, 0)),
                  pl.BlockSpec((1, mix_rows), lambda j: (0, j)),
                  pl.BlockSpec((1, mix_rows), lambda j: (0, j % group_blocks)),
                  pl.BlockSpec((1, mix_rows), lambda j: (0, j % group_blocks)),
                  const(gate_a_w.shape), const(gate_x_w.shape), const(pool_w.shape)]
        + [const(a.shape) for a in vec_args],
        out_specs=[pl.BlockSpec((mix_rows, d_in), lambda j: (j, 0)),
                   pl.BlockSpec((mix_rows, D_MODEL), lambda j: (j, 0)),
                   const((D_LRU // MXU_DIM, MXU_DIM, 2 * MXU_DIM)),
                   const((D_POOL // MXU_DIM, MXU_DIM, MXU_DIM)),
                   const((VEC_ROWS, 2 * D_FF))],
        out_shape=[jax.ShapeDtypeStruct((D_MODEL, d_in), BF16),
                   jax.ShapeDtypeStruct((D_MODEL, D_MODEL), BF16),
                   jax.ShapeDtypeStruct((D_LRU // MXU_DIM, MXU_DIM, 2 * MXU_DIM), BF16),
                   jax.ShapeDtypeStruct((D_POOL // MXU_DIM, MXU_DIM, MXU_DIM), BF16),
                   jax.ShapeDtypeStruct((VEC_ROWS, 2 * D_FF), F32)],
        compiler_params=pltpu.CompilerParams(dimension_semantics=("arbitrary",)),
        name="prepare_mixer_weights",
    )(w_in, w_out, norm_mix_g, gn_lru, gn_pool, gate_a_w, gate_x_w, pool_w, *vec_args)
    mixer_weights = (vectors, w_in_bf, w_gate, w_pool, w_out_bf)

    y_s, (meta_state, sample_state), ffn_weights = _run_streams(
        (meta_tokens, x_sample),
        (state_lru_h, _rows_major(state_lru_conv), _rows_major(state_pool), state_ffn_conv),
        mixer_weights, x_sample.shape, n_streams=1 + n_sample, rows=N_META,
        start_pos=(0,) + (PAST_LEN,) * n_sample, n_fresh=1, y_skip=N_META,
        out_groups=((0, 1), (1, n_sample)), tiled=False, name="short_streams",
        ffn_f32=(w_up, w_down, norm_ffn_g))

    y_p, (prompt_state,), _ = _run_streams(
        (x_prompt,), meta_state, (*mixer_weights, *ffn_weights), x_prompt.shape, n_streams=1,
        rows=PROMPT_TILE, start_pos=(N_META,), n_fresh=0, y_skip=0, out_groups=((0, 1),), tiled=True,
        name="prompt_stream")

    restore = lambda st: (st[0], _rows_major(st[1]), _rows_major(st[2]), st[3])
    return (y_p, y_s, *restore(prompt_state), *restore(sample_state))
```
